```python
import math
import jax, jax.numpy as jnp
from jax import lax
import numpy as np

D_MODEL = 2048
BATCH = 4
SEQ = 4096
DEPTH = 2

PLE_DIM = 256
ROPE_THETA = 10000.0
EPS = 1e-6
Q_BLOCK = 128
DIFF_QK_DIM = 64
DIFF_V_DIM = 2 * DIFF_QK_DIM
DIFF_WIDTH = D_MODEL // 4
DIFF_HEADS = DIFF_WIDTH // DIFF_V_DIM
DSA_HEAD_DIM = 128
DSA_WIDTH = D_MODEL // 4
DSA_HEADS = DSA_WIDTH // DSA_HEAD_DIM
IDX_HEADS = 8
IDX_DIM = 64
DSA_TOPK_MAX = 256
SSM_WIDTH = D_MODEL - DIFF_WIDTH - DSA_WIDTH
SSM_GROUP = 16
SSM_GROUPS = SSM_WIDTH // SSM_GROUP
SSM_STATE = 64
D_FF = 4 * D_MODEL

IN_SPLITS = (
    DIFF_HEADS * 2 * DIFF_QK_DIM,
    DIFF_HEADS * 2 * DIFF_QK_DIM,
    DIFF_WIDTH,
    DSA_HEADS * DSA_HEAD_DIM,
    DSA_HEAD_DIM,
    DSA_HEAD_DIM,
    IDX_HEADS * IDX_DIM,
    IDX_DIM,
    IDX_HEADS,
    SSM_WIDTH,
)
IN_WIDTH = sum(IN_SPLITS)
SPLIT_IDX = [int(v) for v in np.cumsum(IN_SPLITS)[:-1]]

kernel_name = "hymba_style_diff_dsa_s5_hybrid"


def rmsnorm(x, g):
    xf = x.astype(jnp.float32)
    y = xf * lax.rsqrt(jnp.mean(xf * xf, axis=-1, keepdims=True) + EPS)
    return (y * g.astype(jnp.float32)).astype(x.dtype)


def rope(x, positions):
    d = x.shape[-1]
    freqs = ROPE_THETA ** (-jnp.arange(0, d, 2, dtype=jnp.float32) / d)
    ang = positions.astype(jnp.float32)[..., None] * freqs
    ang = ang.reshape(ang.shape[:2] + (1,) * (x.ndim - 3) + (d // 2,))
    cos, sin = jnp.cos(ang).astype(x.dtype), jnp.sin(ang).astype(x.dtype)
    x1, x2 = x[..., : d // 2], x[..., d // 2:]
    return jnp.concatenate([x1 * cos - x2 * sin, x2 * cos + x1 * sin], axis=-1)


def to_blocks(a):
    b, l = a.shape[:2]
    return jnp.moveaxis(a.reshape((b, l // Q_BLOCK, Q_BLOCK) + a.shape[2:]), 1, 0)


def from_blocks(a):
    nb, b, q = a.shape[:3]
    return jnp.moveaxis(a, 0, 1).reshape((b, nb * q) + a.shape[3:])


def diff_attention(q, k, v, positions, lq1, lk1, lq2, lk2, subln_g, lambda_init):
    b, l, _ = q.shape
    q = rope(q.reshape(b, l, DIFF_HEADS, 2, DIFF_QK_DIM), positions)
    k = rope(k.reshape(b, l, DIFF_HEADS, 2, DIFF_QK_DIM), positions)
    v = v.reshape(b, l, DIFF_HEADS, DIFF_V_DIM)
    f32 = jnp.float32
    lam = (jnp.exp(jnp.sum(lq1.astype(f32) * lk1.astype(f32)))
           - jnp.exp(jnp.sum(lq2.astype(f32) * lk2.astype(f32))) + lambda_init)
    scale = DIFF_QK_DIM ** -0.5
    kpos = jnp.arange(l)

    def block(args):
        qb, t0 = args
        logits = jnp.einsum('bqhcd,bshcd->bhcqs', qb, k).astype(f32) * scale
        causal = kpos[None, :] <= (t0 + jnp.arange(Q_BLOCK))[:, None]
        pr = jax.nn.softmax(jnp.where(causal, logits, -jnp.inf), axis=-1)
        attn = pr[:, :, 0] - lam * pr[:, :, 1]
        return jnp.einsum('bhqs,bshe->bqhe', attn.astype(v.dtype), v)

    nb = l // Q_BLOCK
    out = from_blocks(lax.map(block, (to_blocks(q), jnp.arange(nb) * Q_BLOCK)))
    out = rmsnorm(out, subln_g) * (1.0 - lambda_init)
    return out.reshape(b, l, DIFF_WIDTH)


def dsa_attention(q, k, v, iq, ik, iw, positions):
    b, l, _ = q.shape
    f32 = jnp.float32
    q = rope(q.reshape(b, l, DSA_HEADS, DSA_HEAD_DIM), positions)
    k = rope(k, positions)
    iq = rope(iq.reshape(b, l, IDX_HEADS, IDX_DIM), positions)
    ik = rope(ik, positions)
    iw = iw * (IDX_HEADS ** -0.5)
    topk = min(DSA_TOPK_MAX, l // 4)
    scale = DSA_HEAD_DIM ** -0.5
    idx_scale = IDX_DIM ** -0.5
    kpos = jnp.arange(l)
    gather = jax.vmap(lambda arr, ind: arr[ind])

    def block(args):
        qb, iqb, iwb, t0 = args
        qi = t0 + jnp.arange(Q_BLOCK)
        rel = jax.nn.relu(jnp.einsum('bqhd,bsd->bqhs', iqb, ik) * idx_scale)
        score = jnp.einsum('bqhs,bqh->bqs', rel, iwb).astype(f32)
        score = jnp.where(kpos[None, None, :] <= qi[None, :, None], score, -jnp.inf)
        _, sel = lax.top_k(score, topk)
        kg = gather(k, sel)
        vg = gather(v, sel)
        logits = jnp.einsum('bqhd,bqkd->bhqk', qb, kg).astype(f32) * scale
        valid = (sel <= qi[None, :, None])[:, None]
        pr = jax.nn.softmax(jnp.where(valid, logits, -jnp.inf), axis=-1)
        return jnp.einsum('bhqk,bqkd->bqhd', pr.astype(vg.dtype), vg)

    nb = l // Q_BLOCK
    out = lax.map(block, (to_blocks(q), to_blocks(iq), to_blocks(iw), jnp.arange(nb) * Q_BLOCK))
    return from_blocks(out).reshape(b, l, DSA_WIDTH)


def s5_glu(u, lam_re, lam_im, log_step, b_re, b_im, c_re, c_im, d_skip, w_glu):
    bsz, l, _ = u.shape
    f32 = jnp.float32
    ug = u.astype(f32).reshape(bsz, l, SSM_GROUPS, SSM_GROUP)
    step = jnp.exp(log_step.astype(f32))[:, None]
    lr, li = lam_re.astype(f32), lam_im.astype(f32)
    mag = jnp.exp(lr * step)
    ab_re, ab_im = mag * jnp.cos(li * step), mag * jnp.sin(li * step)
    den = lr * lr + li * li
    nr, ni = ab_re - 1.0, ab_im
    f_re, f_im = (nr * lr + ni * li) / den, (ni * lr - nr * li) / den
    br, bi = b_re.astype(f32), b_im.astype(f32)
    bb_re = f_re[..., None] * br - f_im[..., None] * bi
    bb_im = f_re[..., None] * bi + f_im[..., None] * br
    bu_re = jnp.einsum('blgc,gpc->blgp', ug, bb_re)
    bu_im = jnp.einsum('blgc,gpc->blgp', ug, bb_im)
    a_re = jnp.broadcast_to(ab_re, (1, l, SSM_GROUPS, SSM_STATE))
    a_im = jnp.broadcast_to(ab_im, (1, l, SSM_GROUPS, SSM_STATE))

    def combine(e1, e2):
        a1r, a1i, b1r, b1i = e1
        a2r, a2i, b2r, b2i = e2
        return (a2r * a1r - a2i * a1i, a2r * a1i + a2i * a1r,
                a2r * b1r - a2i * b1i + b2r, a2r * b1i + a2i * b1r + b2i)

    _, _, xr, xi = lax.associative_scan(combine, (a_re, a_im, bu_re, bu_im), axis=1)
    y = (jnp.einsum('blgp,gcp->blgc', xr, c_re.astype(f32))
         - jnp.einsum('blgp,gcp->blgc', xi, c_im.astype(f32))
         + d_skip.astype(f32).reshape(SSM_GROUPS, SSM_GROUP) * ug)
    y = jax.nn.gelu(y.reshape(bsz, l, SSM_WIDTH))
    y = y * jax.nn.sigmoid(y @ w_glu.astype(f32))
    return y.astype(u.dtype)


def hybrid_layer(h, p_i, positions, lambda_init, norm_mix_g, w_in, w_out,
                 lq1, lk1, lq2, lk2, subln_g,
                 lam_re, lam_im, log_step, b_re, b_im, c_re, c_im, d_skip, w_glu,
                 norm_mlp_g, w_up, w_down, norm_ple_g, w_ple_gate, w_ple_proj):
    xn = rmsnorm(h, norm_mix_g)
    z = xn @ w_in
    a_q, a_k, a_v, b_q, b_k, b_v, i_q, i_k, i_w, c_u = jnp.split(z, SPLIT_IDX, axis=-1)
    y_a = diff_attention(a_q, a_k, a_v, positions, lq1, lk1, lq2, lk2, subln_g, lambda_init)
    y_b = dsa_attention(b_q, b_k, b_v, i_q, i_k, i_w, positions)
    y_c = s5_glu(c_u, lam_re, lam_im, log_step, b_re, b_im, c_re, c_im, d_skip, w_glu)
    h = h + jnp.concatenate([y_a, y_b, y_c], axis=-1) @ w_out
    hid = jnp.square(jax.nn.relu(rmsnorm(h, norm_mlp_g) @ w_up))
    h = h + hid @ w_down
    gate = jax.nn.sigmoid(rmsnorm(h, norm_ple_g) @ w_ple_gate)
    return h + gate * (p_i @ w_ple_proj)


def setup_inputs(seed: int = 0) -> dict:
    key = jax.random.key(seed)
    ks = iter(jax.random.split(key, 40))

    def nrm(shape, scale):
        return jax.random.normal(next(ks), shape, jnp.float32) * scale

    def gain(shape):
        return 1.0 + nrm(shape, 0.02)

    G, P, C = SSM_GROUPS, SSM_STATE, SSM_GROUP
    lam_im = (jnp.pi * jnp.arange(P, dtype=jnp.float32))[None, None, :] + nrm((DEPTH, G, P), 0.01)
    return {
        "x": nrm((BATCH, SEQ, D_MODEL), 1.0),
        "p": nrm((DEPTH, BATCH, SEQ, PLE_DIM), 1.0),
        "positions": jnp.broadcast_to(jnp.arange(SEQ, dtype=jnp.int32), (BATCH, SEQ)),
        "norm_mix_g": gain((DEPTH, D_MODEL)),
        "w_in": nrm((DEPTH, D_MODEL, IN_WIDTH), D_MODEL ** -0.5),
        "w_out": nrm((DEPTH, D_MODEL, D_MODEL), D_MODEL ** -0.5),
        "diff_lq1": nrm((DEPTH, DIFF_QK_DIM), 0.1),
        "diff_lk1": nrm((DEPTH, DIFF_QK_DIM), 0.1),
        "diff_lq2": nrm((DEPTH, DIFF_QK_DIM), 0.1),
        "diff_lk2": nrm((DEPTH, DIFF_QK_DIM), 0.1),
        "diff_subln_g": gain((DEPTH, DIFF_V_DIM)),
        "ssm_lambda_re": -0.5 + nrm((DEPTH, G, P), 0.01),
        "ssm_lambda_im": lam_im,
        "ssm_log_step": jax.random.uniform(next(ks), (DEPTH, G), jnp.float32,
                                           math.log(1e-3), math.log(1e-1)),
        "ssm_B_re": nrm((DEPTH, G, P, C), (2.0 * C) ** -0.5),
        "ssm_B_im": nrm((DEPTH, G, P, C), (2.0 * C) ** -0.5),
        "ssm_C_re": nrm((DEPTH, G, C, P), (2.0 * P) ** -0.5),
        "ssm_C_im": nrm((DEPTH, G, C, P), (2.0 * P) ** -0.5),
        "ssm_D": nrm((DEPTH, SSM_WIDTH), 1.0),
        "ssm_w_glu": nrm((DEPTH, SSM_WIDTH, SSM_WIDTH), SSM_WIDTH ** -0.5),
        "norm_mlp_g": gain((DEPTH, D_MODEL)),
        "w_up": nrm((DEPTH, D_MODEL, D_FF), D_MODEL ** -0.5),
        "w_down": nrm((DEPTH, D_FF, D_MODEL), D_FF ** -0.5),
        "norm_ple_g": gain((DEPTH, D_MODEL)),
        "w_ple_gate": nrm((DEPTH, D_MODEL, D_MODEL), D_MODEL ** -0.5),
        "w_ple_proj": nrm((DEPTH, PLE_DIM, D_MODEL), PLE_DIM ** -0.5),
        "final_g": gain((D_MODEL,)),
    }


def reference(x, p, positions, norm_mix_g, w_in, w_out, diff_lq1, diff_lk1, diff_lq2, diff_lk2,
              diff_subln_g, ssm_lambda_re, ssm_lambda_im, ssm_log_step, ssm_B_re, ssm_B_im,
              ssm_C_re, ssm_C_im, ssm_D, ssm_w_glu, norm_mlp_g, w_up, w_down, norm_ple_g,
              w_ple_gate, w_ple_proj, final_g):
    h = x
    for i in range(DEPTH):
        lambda_init = 0.8 - 0.6 * math.exp(-0.3 * i)
        h = hybrid_layer(h, p[i], positions, lambda_init, norm_mix_g[i], w_in[i], w_out[i],
                         diff_lq1[i], diff_lk1[i], diff_lq2[i], diff_lk2[i], diff_subln_g[i],
                         ssm_lambda_re[i], ssm_lambda_im[i], ssm_log_step[i], ssm_B_re[i],
                         ssm_B_im[i], ssm_C_re[i], ssm_C_im[i], ssm_D[i], ssm_w_glu[i],
                         norm_mlp_g[i], w_up[i], w_down[i], norm_ple_g[i], w_ple_gate[i],
                         w_ple_proj[i])
    return rmsnorm(h, final_g)
```

```python
import functools
import math

import jax
import jax.numpy as jnp
from jax import lax
from jax.experimental import pallas as pl
from jax.experimental.pallas import tpu as pltpu

F32 = jnp.float32
BF16 = jnp.bfloat16
I32 = jnp.int32

LANES = 128
EPS = 1e-6
ROPE_THETA = 10000.0

DIFF_QK = 64
DIFF_V = 128
DIFF_HEADS = 4
DSA_DIM = 128
DSA_HEADS = 4
IDX_HEADS = 8
IDX_DIM = 64
TOPK_MAX = 256
SSM_C = 16
SSM_P = 64

COL_AQ, COL_AK, COL_IQ, COL_BQ, COL_AV = 0, 4, 8, 12, 16
COL_BK, COL_BV, COL_IKW, COL_CU = 20, 21, 22, 24
Z_WIDTH = 32 * LANES
IN_TN = 512

NEG_BIG = -1e30
VMEM_LIMIT = 56 * 1024 * 1024


def _cparams(sem):
    return pltpu.CompilerParams(dimension_semantics=sem, vmem_limit_bytes=VMEM_LIMIT)


def _rope_tables_kernel(pos_ref, f64_ref, f128_ref, c64_ref, s64_ref, c128_ref, s128_ref):
    pos = pos_ref[...]
    a64 = pos * f64_ref[...]
    a128 = pos * f128_ref[...]
    lane = lax.broadcasted_iota(I32, a64.shape, 1)
    c64_ref[...] = jnp.cos(a64)
    sn = jnp.sin(a64)
    s64_ref[...] = jnp.where((lane & 32) == 0, -sn, sn)
    c128_ref[...] = jnp.cos(a128)
    sn = jnp.sin(a128)
    s128_ref[...] = jnp.where((lane & 64) == 0, -sn, sn)


def _rope_tables(positions):
    t = positions.size
    pos = positions.reshape(t, 1).astype(F32)
    fr64 = ROPE_THETA ** (-jnp.arange(0, 64, 2, dtype=F32) / 64)
    fr128 = ROPE_THETA ** (-jnp.arange(0, 128, 2, dtype=F32) / 128)
    f64 = jnp.tile(fr64, 4).reshape(1, LANES)
    f128 = jnp.tile(fr128, 2).reshape(1, LANES)
    tm = min(t, 1024)
    tab = jax.ShapeDtypeStruct((t, LANES), F32)
    row = pl.BlockSpec((tm, LANES), lambda i: (i, 0))
    return pl.pallas_call(
        _rope_tables_kernel,
        out_shape=(tab, tab, tab, tab),
        grid=(t // tm,),
        in_specs=[pl.BlockSpec((tm, 1), lambda i: (i, 0)),
                  pl.BlockSpec((1, LANES), lambda i: (0, 0)),
                  pl.BlockSpec((1, LANES), lambda i: (0, 0))],
        out_specs=(row, row, row, row),
        compiler_params=_cparams(("parallel",)),
        name="rope_tables",
    )(pos, f64, f128)


def _rms(x, g):
    ms = jnp.mean(x * x, axis=-1, keepdims=True)
    return x * lax.rsqrt(ms + EPS) * g


def _swap_halves(a, half):
    lane = lax.broadcasted_iota(I32, a.shape, 1)
    return jnp.where((lane & half) == 0,
                     pltpu.roll(a, LANES - half, 1), pltpu.roll(a, half, 1))


def _inproj_kernel(x_ref, g_ref, w_ref, c64_ref, s64_ref, c128_ref, s128_ref, z_ref, xn_ref):
    n = pl.program_id(1)

    @pl.when(n == 0)
    def _():
        xn_ref[...] = _rms(x_ref[...], g_ref[...]).astype(BF16)

    acc = jnp.dot(xn_ref[...], w_ref[...], preferred_element_type=F32)
    groups = acc.shape[1] // LANES

    def rope(a, half):
        if half == 32:
            return a * c64_ref[...] + _swap_halves(a, 32) * s64_ref[...]
        return a * c128_ref[...] + _swap_halves(a, 64) * s128_ref[...]

    def grp(j):
        return acc[:, j * LANES:(j + 1) * LANES]

    def put(j, v):
        z_ref[:, j * LANES:(j + 1) * LANES] = v.astype(z_ref.dtype)

    @pl.when(n < 3)
    def _():
        for j in range(groups):
            put(j, rope(grp(j), 32))

    @pl.when(n == 3)
    def _():
        for j in range(groups):
            put(j, rope(grp(j), 64))

    @pl.when(n == 5)
    def _():
        put(0, rope(grp(0), 64))
        put(1, grp(1))
        a = grp(2)
        lane = lax.broadcasted_iota(I32, a.shape, 1)
        put(2, jnp.where(lane < IDX_DIM, rope(a, 32), a))
        put(3, grp(3))

    @pl.when((n == 4) | (n > 5))
    def _():
        z_ref[...] = acc.astype(z_ref.dtype)


def _inproj(h2d, g, w, tabs):
    t, d = h2d.shape
    tm = min(t, 512)
    tn = IN_TN
    c64, s64, c128, s128 = tabs
    tab = pl.BlockSpec((tm, LANES), lambda m, n: (m, 0))
    return pl.pallas_call(
        _inproj_kernel,
        out_shape=jax.ShapeDtypeStruct((t, Z_WIDTH), BF16),
        grid=(t // tm, Z_WIDTH // tn),
        in_specs=[pl.BlockSpec((tm, d), lambda m, n: (m, 0)),
                  pl.BlockSpec((1, d), lambda m, n: (0, 0)),
                  pl.BlockSpec((d, tn), lambda m, n: (0, n)),
                  tab, tab, tab, tab],
        out_specs=pl.BlockSpec((tm, tn), lambda m, n: (m, n)),
        scratch_shapes=[pltpu.VMEM((tm, d), BF16)],
        compiler_params=_cparams(("parallel", "arbitrary")),
        name="norm_inproj_rope",
    )(h2d, g.reshape(1, d), w, c64, s64, c128, s128)


def _permute_w_in(w):
    d = w.shape[0]
    pad = jnp.zeros((d, 56 + LANES), w.dtype)
    return jnp.concatenate(
        [w[:, 0:512], w[:, 512:1024], w[:, 2304:2816], w[:, 1536:2048], w[:, 1024:1536],
         w[:, 2048:2176], w[:, 2176:2304], w[:, 2816:2888], pad, w[:, 2888:3912]],
        axis=1).astype(BF16)


def _flash_step(carry, s, vb):
    m, l, acc = carry
    m_new = jnp.maximum(m, jnp.max(s, axis=1, keepdims=True))
    alpha = jnp.exp(m - m_new)
    p = jnp.exp(s - m_new)
    l = alpha * l + jnp.sum(p, axis=1, keepdims=True)
    acc = alpha * acc + jnp.dot(p.astype(BF16), vb, preferred_element_type=F32)
    return m_new, l, acc


def _dot_nt(a, b):
    return lax.dot_general(a, b, (((1,), (1,)), ((), ())), preferred_element_type=F32)


def _diff_attn_kernel(q_ref, k_ref, v_ref, lq1_ref, lk1_ref, lq2_ref, lk2_ref, g_ref, o_ref,
                      *, tq, lambda_init):
    i = pl.program_id(2)
    q = q_ref[0].astype(F32) * (DIFF_QK ** -0.5)
    lane = lax.broadcasted_iota(I32, q.shape, 1)
    q2 = jnp.concatenate([jnp.where(lane < DIFF_QK, q, 0.0),
                          jnp.where(lane >= DIFF_QK, q, 0.0)], axis=0).astype(BF16)

    def block(j, carry, diag):
        kb = k_ref[0, pl.ds(j * tq, tq), :]
        vb = v_ref[0, pl.ds(j * tq, tq), :]
        s = _dot_nt(q2, kb)
        if diag:
            row = lax.broadcasted_iota(I32, s.shape, 0)
            row = jnp.where(row >= tq, row - tq, row)
            col = lax.broadcasted_iota(I32, s.shape, 1)
            s = jnp.where(col <= row, s, NEG_BIG)
        return _flash_step(carry, s, vb)

    init = (jnp.full((2 * tq, 1), NEG_BIG, F32), jnp.zeros((2 * tq, 1), F32),
            jnp.zeros((2 * tq, DIFF_V), F32))
    carry = lax.fori_loop(0, i, lambda j, c: block(j, c, False), init)
    _, l, acc = block(i, carry, True)
    o = acc / l
    lam = (jnp.exp(jnp.sum(lq1_ref[...] * lk1_ref[...], keepdims=True))
           - jnp.exp(jnp.sum(lq2_ref[...] * lk2_ref[...], keepdims=True)) + lambda_init)
    out = o[:tq] - lam * o[tq:]
    out = _rms(out, g_ref[...]) * (1.0 - lambda_init)
    o_ref[0] = out.astype(o_ref.dtype)


def _diff_attn(z3, lq1, lk1, lq2, lk2, subln_g, lambda_init):
    b, l, _ = z3.shape
    tq = min(l, 256)
    vec = pl.BlockSpec((1, DIFF_QK), lambda bi, h, i: (0, 0))
    return pl.pallas_call(
        functools.partial(_diff_attn_kernel, tq=tq, lambda_init=lambda_init),
        out_shape=jax.ShapeDtypeStruct((b, l, DIFF_HEADS * DIFF_V), BF16),
        grid=(b, DIFF_HEADS, l // tq),
        in_specs=[pl.BlockSpec((1, tq, LANES), lambda bi, h, i: (bi, i, COL_AQ + h)),
                  pl.BlockSpec((1, l, LANES), lambda bi, h, i: (bi, 0, COL_AK + h)),
                  pl.BlockSpec((1, l, LANES), lambda bi, h, i: (bi, 0, COL_AV + h)),
                  vec, vec, vec, vec,
                  pl.BlockSpec((1, DIFF_V), lambda bi, h, i: (0, 0))],
        out_specs=pl.BlockSpec((1, tq, DIFF_V), lambda bi, h, i: (bi, i, h)),
        compiler_params=_cparams(("parallel", "parallel", "arbitrary")),
        name="diff_attention",
    )(z3, z3, z3, lq1.reshape(1, -1), lk1.reshape(1, -1), lq2.reshape(1, -1),
      lk2.reshape(1, -1), subln_g.reshape(1, -1))


def _sortable_key(s):
    bits = lax.bitcast_convert_type(s, I32)
    return bits ^ ((bits >> 31) & jnp.int32(0x7FFFFFFF))


def _dsa_kernel(bq_ref, iq_ref, iwq_ref, bk_ref, bv_ref, ik_ref, o_ref, keys_ref,
                *, tq, tk, topk, seq_bits):
    i = pl.program_id(1)
    nkb = ((i + 1) * tq + tk - 1) // tk
    groups = tk // LANES

    iq = iq_ref[0]
    iw = iwq_ref[0][:, IDX_DIM:IDX_DIM + IDX_HEADS].astype(F32) * (IDX_HEADS ** -0.5)
    row_g = i * tq + lax.broadcasted_iota(I32, (tq, tk), 0)
    col_l = lax.broadcasted_iota(I32, (tq, tk), 1)

    def score_block(j, _):
        ikb = ik_ref[0, pl.ds(j * tk, tk), :][:, :IDX_DIM]
        sc = jnp.zeros((tq, tk), F32)
        for h in range(IDX_HEADS):
            r = _dot_nt(iq[:, h * IDX_DIM:(h + 1) * IDX_DIM], ikb)
            sc = sc + jnp.maximum(r * (IDX_DIM ** -0.5), 0.0) * iw[:, h:h + 1]
        sc = jnp.where(j * tk + col_l <= row_g, sc + 0.0, -jnp.inf)
        keys_ref[j] = _sortable_key(sc)
        return 0

    lax.fori_loop(0, nkb, score_block, 0)

    lane_l = lax.broadcasted_iota(I32, (tq, LANES), 1)

    def count(pred_fn):
        def blk(j, cnt):
            for gi in range(groups):
                kk = keys_ref[j, :, gi * LANES:(gi + 1) * LANES]
                colg = lane_l + (j * tk + gi * LANES)
                cnt = cnt + jnp.where(pred_fn(kk, colg), 1, 0)
            return cnt
        cnt = lax.fori_loop(0, nkb, blk, jnp.zeros((tq, LANES), I32))
        return jnp.sum(cnt, axis=1, keepdims=True)

    int_min = jnp.int32(-2 ** 31)
    c0 = count(lambda kk, c: kk >= 0)
    thr = jnp.where(c0 >= topk, jnp.zeros((tq, 1), I32), jnp.full((tq, 1), int_min, I32))

    def bit_step(it, thr):
        cand = thr | (jnp.int32(1) << (30 - it))
        c = count(lambda kk, _: kk >= cand)
        return jnp.where(c >= topk, cand, thr)

    thr = lax.fori_loop(0, 31, bit_step, thr)

    c_gt = count(lambda kk, _: kk > thr)
    c_eq = count(lambda kk, _: kk == thr)
    need = topk - c_gt
    has_excess = jnp.max(jnp.where(c_eq > need, 1, 0)) > 0

    def tie_limit():
        def step(it, q):
            cand = q + (jnp.int32(1) << (seq_bits - 1 - it))
            c = count(lambda kk, colg: (kk == thr) & (colg < cand))
            return jnp.where(c < need, cand, q)
        return lax.fori_loop(0, seq_bits, step, jnp.zeros((tq, 1), I32))

    jlim = lax.cond(has_excess, tie_limit,
                    lambda: jnp.full((tq, 1), 2 ** seq_bits, I32))

    q = bq_ref[0]
    q4 = jnp.concatenate([q[:, h * DSA_DIM:(h + 1) * DSA_DIM] for h in range(DSA_HEADS)], axis=0)
    scale = DSA_DIM ** -0.5

    def attn_block(j, carry):
        kb = bk_ref[0, pl.ds(j * tk, tk), :]
        vb = bv_ref[0, pl.ds(j * tk, tk), :]
        kk = keys_ref[j]
        colg = j * tk + col_l
        sel = ((kk > thr) | ((kk == thr) & (colg <= jlim))) & (colg <= row_g)
        bias = jnp.where(sel, 0.0, NEG_BIG)
        s = _dot_nt(q4, kb) * scale
        s = s + jnp.concatenate([bias] * DSA_HEADS, axis=0)
        return _flash_step(carry, s, vb)

    rows = DSA_HEADS * tq
    init = (jnp.full((rows, 1), NEG_BIG, F32), jnp.zeros((rows, 1), F32),
            jnp.zeros((rows, DSA_DIM), F32))
    _, l, acc = lax.fori_loop(0, nkb, attn_block, init)
    o = acc / l
    for h in range(DSA_HEADS):
        o_ref[0, :, h * DSA_DIM:(h + 1) * DSA_DIM] = o[h * tq:(h + 1) * tq].astype(o_ref.dtype)


def _dsa_attn(z3):
    b, l, _ = z3.shape
    tq = min(l, 128)
    tk = min(l, 512)
    topk = min(TOPK_MAX, l // 4)
    seq_bits = max(1, (l - 1).bit_length())
    wide = 4
    return pl.pallas_call(
        functools.partial(_dsa_kernel, tq=tq, tk=tk, topk=topk, seq_bits=seq_bits),
        out_shape=jax.ShapeDtypeStruct((b, l, DSA_HEADS * DSA_DIM), BF16),
        grid=(b, l // tq),
        in_specs=[pl.BlockSpec((1, tq, 4 * LANES), lambda bi, i: (bi, i, COL_BQ // wide)),
                  pl.BlockSpec((1, tq, 4 * LANES), lambda bi, i: (bi, i, COL_IQ // wide)),
                  pl.BlockSpec((1, tq, LANES), lambda bi, i: (bi, i, COL_IKW)),
                  pl.BlockSpec((1, l, LANES), lambda bi, i: (bi, 0, COL_BK)),
                  pl.BlockSpec((1, l, LANES), lambda bi, i: (bi, 0, COL_BV)),
                  pl.BlockSpec((1, l, LANES), lambda bi, i: (bi, 0, COL_IKW))],
        out_specs=pl.BlockSpec((1, tq, DSA_HEADS * DSA_DIM), lambda bi, i: (bi, i, 0)),
        scratch_shapes=[pltpu.VMEM((l // tk, tq, tk), I32)],
        compiler_params=_cparams(("parallel", "arbitrary")),
        name="dsa_attention",
    )(z3, z3, z3, z3, z3, z3)


def _cmul(ar, ai, br, bi):
    return ar * br - ai * bi, ar * bi + ai * br


def _ssm_prep_kernel(lre_ref, lim_ref, lstep_ref, btr_ref, bti_ref, cr_ref, ci_ref,
                     w1_ref, mt_ref, tzt_ref, at_ref, tz_ref, *, tc, gb):
    c = SSM_C
    p = SSM_P
    for gi in range(gb):
        lr = lre_ref[gi].reshape(1, 1, p)
        li = lim_ref[gi].reshape(1, 1, p)
        step = jnp.exp(lstep_ref[gi]).reshape(1, 1, 1)
        mag = jnp.exp(lr * step)
        a_re, a_im = mag * jnp.cos(li * step), mag * jnp.sin(li * step)
        den = lr * lr + li * li
        nr, ni = a_re - 1.0, a_im
        f_re, f_im = (nr * lr + ni * li) / den, (ni * lr - nr * li) / den
        bt_r, bt_i = btr_ref[gi][None], bti_ref[gi][None]
        bb_re = f_re * bt_r - f_im * bt_i
        bb_im = f_re * bt_i + f_im * bt_r
        fr, fi = jnp.ones_like(a_re), jnp.zeros_like(a_im)
        rr, ri = fr, fi
        pr, pi = a_re, a_im
        n = 1
        while n < tc:
            xr, xi = _cmul(fr, fi, pr, pi)
            fr, fi = jnp.concatenate([fr, xr], 0), jnp.concatenate([fi, xi], 0)
            xr, xi = _cmul(rr, ri, pr, pi)
            rr, ri = jnp.concatenate([xr, rr], 0), jnp.concatenate([xi, ri], 0)
            pr, pi = _cmul(pr, pi, pr, pi)
            n *= 2
        at_ref[gi] = jnp.concatenate([pr[0], pi[0]], axis=1)
        wr, wi = _cmul(rr, ri, bb_re, bb_im)
        w1 = jnp.concatenate([wr, wi], axis=2).reshape(tc * c, 2 * p)
        w1_ref[gi] = w1.astype(w1_ref.dtype)
        f1r, f1i = _cmul(fr, fi, a_re, a_im)
        c_re, c_im = cr_ref[gi][None], ci_ref[gi][None]
        mr, mi = _cmul(f1r, f1i, c_re, c_im)
        mt_ref[gi] = jnp.concatenate([mr, -mi], axis=2).reshape(tc * c, 2 * p).astype(mt_ref.dtype)
        er, ei = _cmul(fr, fi, c_re, c_im)
        e2 = jnp.concatenate([er, -ei], axis=2).reshape(tc * c, 2 * p)
        bcat = jnp.concatenate([bb_re[0], bb_im[0]], axis=1)
        kflat = lax.dot_general(e2, bcat, (((1,), (1,)), ((), ())),
                                precision=lax.Precision.HIGHEST,
                                preferred_element_type=F32)
        tz_ref[...] = jnp.zeros(tz_ref.shape, tz_ref.dtype)
        for s in range(tc):
            tz_ref[s * c:, s * c:(s + 1) * c] = kflat[:(tc - s) * c, :]
        tzt_ref[gi] = tz_ref[...].astype(tzt_ref.dtype)


def _ssm_prep(lam_re, lam_im, log_step, b_re, b_im, c_re, c_im, tc, gb):
    g, p = lam_re.shape
    c = SSM_C
    n = tc * c
    vecp = pl.BlockSpec((gb, 1, p), lambda i: (i, 0, 0))
    mat = pl.BlockSpec((gb, c, p), lambda i: (i, 0, 0))
    op = pl.BlockSpec((gb, n, 2 * p), lambda i: (i, 0, 0))
    return pl.pallas_call(
        functools.partial(_ssm_prep_kernel, tc=tc, gb=gb),
        out_shape=(jax.ShapeDtypeStruct((g, n, 2 * p), BF16),
                   jax.ShapeDtypeStruct((g, n, 2 * p), BF16),
                   jax.ShapeDtypeStruct((g, n, n), BF16),
                   jax.ShapeDtypeStruct((g, 1, 2 * p), F32)),
        grid=(g // gb,),
        in_specs=[vecp, vecp, pl.BlockSpec((gb, 1, 1), lambda i: (i, 0, 0)), mat, mat, mat, mat],
        out_specs=(op, op, pl.BlockSpec((gb, n, n), lambda i: (i, 0, 0)),
                   pl.BlockSpec((gb, 1, 2 * p), lambda i: (i, 0, 0))),
        scratch_shapes=[pltpu.VMEM((n, n), F32)],
        compiler_params=_cparams(("parallel",)),
        name="ssm_prep",
    )(lam_re.reshape(g, 1, p), lam_im.reshape(g, 1, p), log_step.reshape(g, 1, 1),
      jnp.swapaxes(b_re, 1, 2), jnp.swapaxes(b_im, 1, 2), c_re, c_im)


def _ssm_chunk_state_kernel(u_ref, w1_ref, s_ref, *, gb):
    for gi in range(gb):
        s_ref[gi] = jnp.dot(u_ref[gi], w1_ref[gi], preferred_element_type=F32)


def _ssm_scan_kernel(s_ref, at_ref, x_ref, st_ref, *, nchunks, bsz):
    a = jnp.broadcast_to(at_ref[...], st_ref.shape)
    lane = lax.broadcasted_iota(I32, a.shape, 2)
    a_sw = pltpu.roll(a, SSM_P, 2)
    a_re = jnp.where(lane < SSM_P, a, a_sw)
    a_im_s = jnp.where(lane < SSM_P, -a_sw, a)
    st_ref[...] = jnp.zeros(st_ref.shape, st_ref.dtype)

    def step(c, _):
        x = st_ref[...]
        x_ref[:, pl.ds(c * bsz, bsz), :] = x
        s = s_ref[:, pl.ds(c * bsz, bsz), :]
        st_ref[...] = x * a_re + pltpu.roll(x, SSM_P, 2) * a_im_s + s
        return 0

    lax.fori_loop(0, nchunks, step, 0)


def _ssm_out_kernel(u_ref, x_ref, tzt_ref, mt_ref, d_ref, y_ref, *, gb):
    for gi in range(gb):
        u = u_ref[gi]
        y = _dot_nt(u, tzt_ref[gi]) + _dot_nt(x_ref[gi].astype(BF16), mt_ref[gi])
        y = y + u.astype(F32) * d_ref[gi]
        y_ref[gi] = jax.nn.gelu(y).astype(y_ref.dtype)


def _s5(u3, ops, d_skip, tc, gb):
    w1, mt, tzt, at = ops
    b, l, width = u3.shape
    c = SSM_C
    g = width // c
    nch = l // tc
    rows = nch * b
    n = tc * c
    p2 = 2 * SSM_P
    ug = u3.reshape(b, nch, tc, g, c).transpose(3, 1, 0, 2, 4).reshape(g, rows, n)
    blk_u = pl.BlockSpec((gb, rows, n), lambda i: (i, 0, 0))
    blk_s = pl.BlockSpec((gb, rows, p2), lambda i: (i, 0, 0))
    blk_op = pl.BlockSpec((gb, n, p2), lambda i: (i, 0, 0))
    s = pl.pallas_call(
        functools.partial(_ssm_chunk_state_kernel, gb=gb),
        out_shape=jax.ShapeDtypeStruct((g, rows, p2), F32),
        grid=(g // gb,),
        in_specs=[blk_u, blk_op],
        out_specs=blk_s,
        compiler_params=_cparams(("parallel",)),
        name="ssm_chunk_state",
    )(ug, w1)
    xprev = pl.pallas_call(
        functools.partial(_ssm_scan_kernel, nchunks=nch, bsz=b),
        out_shape=jax.ShapeDtypeStruct((g, rows, p2), F32),
        grid=(g // gb,),
        in_specs=[blk_s, pl.BlockSpec((gb, 1, p2), lambda i: (i, 0, 0))],
        out_specs=blk_s,
        scratch_shapes=[pltpu.VMEM((gb, b, p2), F32)],
        compiler_params=_cparams(("parallel",)),
        name="ssm_chunk_scan",
    )(s, at)
    dt = jnp.tile(d_skip.reshape(g, 1, c), (1, 1, tc)).astype(F32)
    y = pl.pallas_call(
        functools.partial(_ssm_out_kernel, gb=gb),
        out_shape=jax.ShapeDtypeStruct((g, rows, n), BF16),
        grid=(g // gb,),
        in_specs=[blk_u, blk_s, pl.BlockSpec((gb, n, n), lambda i: (i, 0, 0)), blk_op,
                  pl.BlockSpec((gb, 1, n), lambda i: (i, 0, 0))],
        out_specs=blk_u,
        compiler_params=_cparams(("parallel",)),
        name="ssm_chunk_out",
    )(ug, xprev, tzt, mt, dt)
    return y.reshape(g, nch, b, tc, c).transpose(2, 1, 3, 0, 4).reshape(b, l, width)


def _glu_kernel(y_ref, yn_ref, w_ref, o_ref):
    gate = jnp.dot(y_ref[...], w_ref[...], preferred_element_type=F32)
    o_ref[...] = (yn_ref[...].astype(F32) * jax.nn.sigmoid(gate)).astype(o_ref.dtype)


def _glu(y2d, w):
    t, k = y2d.shape
    tm = min(t, 1024)
    tn = min(k, 512)
    return pl.pallas_call(
        _glu_kernel,
        out_shape=jax.ShapeDtypeStruct((t, k), BF16),
        grid=(t // tm, k // tn),
        in_specs=[pl.BlockSpec((tm, k), lambda m, n: (m, 0)),
                  pl.BlockSpec((tm, tn), lambda m, n: (m, n)),
                  pl.BlockSpec((k, tn), lambda m, n: (0, n))],
        out_specs=pl.BlockSpec((tm, tn), lambda m, n: (m, n)),
        compiler_params=_cparams(("parallel", "arbitrary")),
        name="ssm_glu",
    )(y2d, y2d, w)


def _outproj_kernel(h_ref, ya_ref, yb_ref, yc_ref, wa_ref, wb_ref, wc_ref, o_ref):
    acc = jnp.dot(ya_ref[...], wa_ref[...], preferred_element_type=F32)
    acc += jnp.dot(yb_ref[...], wb_ref[...], preferred_element_type=F32)
    acc += jnp.dot(yc_ref[...], wc_ref[...], preferred_element_type=F32)
    o_ref[...] = h_ref[...] + acc


def _outproj(h2d, ya, yb, yc, w):
    t, d = h2d.shape
    ka, kb, kc = ya.shape[1], yb.shape[1], yc.shape[1]
    tm = min(t, 512)
    tn = min(d, 512)
    return pl.pallas_call(
        _outproj_kernel,
        out_shape=jax.ShapeDtypeStruct((t, d), F32),
        grid=(t // tm, d // tn),
        in_specs=[pl.BlockSpec((tm, tn), lambda m, n: (m, n)),
                  pl.BlockSpec((tm, ka), lambda m, n: (m, 0)),
                  pl.BlockSpec((tm, kb), lambda m, n: (m, 0)),
                  pl.BlockSpec((tm, kc), lambda m, n: (m, 0)),
                  pl.BlockSpec((ka, tn), lambda m, n: (0, n)),
                  pl.BlockSpec((kb, tn), lambda m, n: (0, n)),
                  pl.BlockSpec((kc, tn), lambda m, n: (0, n))],
        out_specs=pl.BlockSpec((tm, tn), lambda m, n: (m, n)),
        compiler_params=_cparams(("parallel", "arbitrary")),
        name="out_proj",
    )(h2d, ya, yb, yc, w[:ka], w[ka:ka + kb], w[ka + kb:])


def _mlp_kernel(h_ref, g_ref, wu_ref, wd_ref, o_ref, xn_ref, acc_ref):
    f = pl.program_id(1)

    @pl.when(f == 0)
    def _():
        xn_ref[...] = _rms(h_ref[...], g_ref[...]).astype(BF16)
        acc_ref[...] = h_ref[...]

    hid = jnp.dot(xn_ref[...], wu_ref[...], preferred_element_type=F32)
    hid = jnp.square(jnp.maximum(hid, 0.0)).astype(BF16)
    acc_ref[...] += jnp.dot(hid, wd_ref[...], preferred_element_type=F32)

    @pl.when(f == pl.num_programs(1) - 1)
    def _():
        o_ref[...] = acc_ref[...]


def _mlp(h2d, g, w_up, w_down):
    t, d = h2d.shape
    ff = w_up.shape[1]
    tm = min(t, 512)
    tf = min(ff, 512)
    return pl.pallas_call(
        _mlp_kernel,
        out_shape=jax.ShapeDtypeStruct((t, d), F32),
        grid=(t // tm, ff // tf),
        in_specs=[pl.BlockSpec((tm, d), lambda m, f: (m, 0)),
                  pl.BlockSpec((1, d), lambda m, f: (0, 0)),
                  pl.BlockSpec((d, tf), lambda m, f: (0, f)),
                  pl.BlockSpec((tf, d), lambda m, f: (f, 0))],
        out_specs=pl.BlockSpec((tm, d), lambda m, f: (m, 0)),
        scratch_shapes=[pltpu.VMEM((tm, d), BF16), pltpu.VMEM((tm, d), F32)],
        compiler_params=_cparams(("parallel", "arbitrary")),
        name="mlp_relu2",
    )(h2d, g.reshape(1, d), w_up, w_down)


def _ple_kernel(h_ref, g_ref, p_ref, wg_ref, wp_ref, fg_ref, o_ref, *, tn, final):
    h = h_ref[...]
    xn = _rms(h, g_ref[...]).astype(BF16)
    pb = p_ref[...]
    d = h.shape[1]
    for j in range(d // tn):
        sl = slice(j * tn, (j + 1) * tn)
        gate = jax.nn.sigmoid(jnp.dot(xn, wg_ref[:, sl], preferred_element_type=F32))
        proj = jnp.dot(pb, wp_ref[:, sl], preferred_element_type=F32)
        o_ref[:, sl] = h[:, sl] + gate * proj
    if final:
        o_ref[...] = _rms(o_ref[...], fg_ref[...])


def _ple(h2d, g, p2d, wg, wp, final_g, final):
    t, d = h2d.shape
    pd = p2d.shape[1]
    tm = min(t, 256)
    return pl.pallas_call(
        functools.partial(_ple_kernel, tn=min(d, 512), final=final),
        out_shape=jax.ShapeDtypeStruct((t, d), F32),
        grid=(t // tm,),
        in_specs=[pl.BlockSpec((tm, d), lambda m: (m, 0)),
                  pl.BlockSpec((1, d), lambda m: (0, 0)),
                  pl.BlockSpec((tm, pd), lambda m: (m, 0)),
                  pl.BlockSpec((d, d), lambda m: (0, 0)),
                  pl.BlockSpec((pd, d), lambda m: (0, 0)),
                  pl.BlockSpec((1, d), lambda m: (0, 0))],
        out_specs=pl.BlockSpec((tm, d), lambda m: (m, 0)),
        compiler_params=_cparams(("parallel",)),
        name="ple_gate",
    )(h2d, g.reshape(1, d), p2d, wg, wp, final_g.reshape(1, d))


SSM_CHUNK = 32
SSM_GROUP_BLOCK = 8


def kernel(x, p, positions, norm_mix_g, w_in, w_out, diff_lq1, diff_lk1, diff_lq2, diff_lk2,
           diff_subln_g, ssm_lambda_re, ssm_lambda_im, ssm_log_step, ssm_B_re, ssm_B_im,
           ssm_C_re, ssm_C_im, ssm_D, ssm_w_glu, norm_mlp_g, w_up, w_down, norm_ple_g,
           w_ple_gate, w_ple_proj, final_g):
    b, l, d = x.shape
    depth = w_in.shape[0]
    t = b * l
    tc = min(SSM_CHUNK, l)
    tabs = _rope_tables(positions)
    h = x.reshape(t, d)
    for i in range(depth):
        lambda_init = 0.8 - 0.6 * math.exp(-0.3 * i)
        z = _inproj(h, norm_mix_g[i], _permute_w_in(w_in[i]), tabs)
        z3 = z.reshape(b, l, Z_WIDTH)
        y_a = _diff_attn(z3, diff_lq1[i], diff_lk1[i], diff_lq2[i], diff_lk2[i],
                         diff_subln_g[i], lambda_init)
        y_b = _dsa_attn(z3)
        ops = _ssm_prep(ssm_lambda_re[i], ssm_lambda_im[i], ssm_log_step[i], ssm_B_re[i],
                        ssm_B_im[i], ssm_C_re[i], ssm_C_im[i], tc, SSM_GROUP_BLOCK)
        y_s = _s5(z3[:, :, COL_CU * LANES:], ops, ssm_D[i], tc, SSM_GROUP_BLOCK)
        y_c = _glu(y_s.reshape(t, -1), ssm_w_glu[i].astype(BF16))
        h = _outproj(h, y_a.reshape(t, -1), y_b.reshape(t, -1), y_c, w_out[i].astype(BF16))
        h = _mlp(h, norm_mlp_g[i], w_up[i].astype(BF16), w_down[i].astype(BF16))
        h = _ple(h, norm_ple_g[i], p[i].reshape(t, -1).astype(BF16), w_ple_gate[i].astype(BF16),
                 w_ple_proj[i].astype(BF16), final_g, final=(i == depth - 1))
    return h.reshape(b, l, d)
```

```python
import functools
import math

import jax
import jax.numpy as jnp
from jax import lax
from jax.experimental import pallas as pl
from jax.experimental.pallas import tpu as pltpu

F32 = jnp.float32
BF16 = jnp.bfloat16
I32 = jnp.int32

LANES = 128
EPS = 1e-6
ROPE_THETA = 10000.0

DIFF_QK = 64
DIFF_V = 128
DIFF_HEADS = 4
DSA_DIM = 128
DSA_HEADS = 4
IDX_HEADS = 8
IDX_DIM = 64
TOPK_MAX = 256
SSM_C = 16
SSM_P = 64

COL_AQ, COL_AK, COL_IQ, COL_BQ, COL_AV = 0, 4, 8, 12, 16
COL_BK, COL_BV, COL_IKW, COL_CU = 20, 21, 22, 24
Z_WIDTH = 32 * LANES
IN_TN = 512

DSA_TQ = 512
NEG_BIG = -1e30
VMEM_LIMIT = 56 * 1024 * 1024


def _cparams(sem):
    return pltpu.CompilerParams(dimension_semantics=sem, vmem_limit_bytes=VMEM_LIMIT)


def _rope_tables_kernel(pos_ref, f64_ref, f128_ref, c64_ref, s64_ref, c128_ref, s128_ref):
    pos = pos_ref[...]
    a64 = pos * f64_ref[...]
    a128 = pos * f128_ref[...]
    lane = lax.broadcasted_iota(I32, a64.shape, 1)
    c64_ref[...] = jnp.cos(a64)
    sn = jnp.sin(a64)
    s64_ref[...] = jnp.where((lane & 32) == 0, -sn, sn)
    c128_ref[...] = jnp.cos(a128)
    sn = jnp.sin(a128)
    s128_ref[...] = jnp.where((lane & 64) == 0, -sn, sn)


def _rope_tables(positions):
    t = positions.size
    pos = positions.reshape(t, 1).astype(F32)
    fr64 = ROPE_THETA ** (-jnp.arange(0, 64, 2, dtype=F32) / 64)
    fr128 = ROPE_THETA ** (-jnp.arange(0, 128, 2, dtype=F32) / 128)
    f64 = jnp.tile(fr64, 4).reshape(1, LANES)
    f128 = jnp.tile(fr128, 2).reshape(1, LANES)
    tm = min(t, 1024)
    tab = jax.ShapeDtypeStruct((t, LANES), F32)
    row = pl.BlockSpec((tm, LANES), lambda i: (i, 0))
    return pl.pallas_call(
        _rope_tables_kernel,
        out_shape=(tab, tab, tab, tab),
        grid=(t // tm,),
        in_specs=[pl.BlockSpec((tm, 1), lambda i: (i, 0)),
                  pl.BlockSpec((1, LANES), lambda i: (0, 0)),
                  pl.BlockSpec((1, LANES), lambda i: (0, 0))],
        out_specs=(row, row, row, row),
        compiler_params=_cparams(("parallel",)),
        name="rope_tables",
    )(pos, f64, f128)


def _rms(x, g):
    ms = jnp.mean(x * x, axis=-1, keepdims=True)
    return x * lax.rsqrt(ms + EPS) * g


def _swap_halves(a, half):
    lane = lax.broadcasted_iota(I32, a.shape, 1)
    return jnp.where((lane & half) == 0,
                     pltpu.roll(a, LANES - half, 1), pltpu.roll(a, half, 1))


def _inproj_kernel(x_ref, g_ref, w_ref, c64_ref, s64_ref, c128_ref, s128_ref, z_ref, xn_ref):
    n = pl.program_id(1)

    @pl.when(n == 0)
    def _():
        xn_ref[...] = _rms(x_ref[...], g_ref[...]).astype(BF16)

    acc = jnp.dot(xn_ref[...], w_ref[...], preferred_element_type=F32)
    groups = acc.shape[1] // LANES

    def rope(a, half):
        if half == 32:
            return a * c64_ref[...] + _swap_halves(a, 32) * s64_ref[...]
        return a * c128_ref[...] + _swap_halves(a, 64) * s128_ref[...]

    def grp(j):
        return acc[:, j * LANES:(j + 1) * LANES]

    def put(j, v):
        z_ref[:, j * LANES:(j + 1) * LANES] = v.astype(z_ref.dtype)

    @pl.when(n < 3)
    def _():
        for j in range(groups):
            put(j, rope(grp(j), 32))

    @pl.when(n == 3)
    def _():
        for j in range(groups):
            put(j, rope(grp(j), 64))

    @pl.when(n == 5)
    def _():
        put(0, rope(grp(0), 64))
        put(1, grp(1))
        a = grp(2)
        lane = lax.broadcasted_iota(I32, a.shape, 1)
        put(2, jnp.where(lane < IDX_DIM, rope(a, 32), a))
        put(3, grp(3))

    @pl.when((n == 4) | (n > 5))
    def _():
        z_ref[...] = acc.astype(z_ref.dtype)


def _inproj(h2d, g, w, tabs):
    t, d = h2d.shape
    tm = min(t, 512)
    tn = IN_TN
    c64, s64, c128, s128 = tabs
    tab = pl.BlockSpec((tm, LANES), lambda m, n: (m, 0))
    return pl.pallas_call(
        _inproj_kernel,
        out_shape=jax.ShapeDtypeStruct((t, Z_WIDTH), BF16),
        grid=(t // tm, Z_WIDTH // tn),
        in_specs=[pl.BlockSpec((tm, d), lambda m, n: (m, 0)),
                  pl.BlockSpec((1, d), lambda m, n: (0, 0)),
                  pl.BlockSpec((d, tn), lambda m, n: (0, n)),
                  tab, tab, tab, tab],
        out_specs=pl.BlockSpec((tm, tn), lambda m, n: (m, n)),
        scratch_shapes=[pltpu.VMEM((tm, d), BF16)],
        compiler_params=_cparams(("parallel", "arbitrary")),
        name="norm_inproj_rope",
    )(h2d, g.reshape(1, d), w, c64, s64, c128, s128)


def _permute_w_in(w):
    d = w.shape[0]
    pad = jnp.zeros((d, 56 + LANES), w.dtype)
    return jnp.concatenate(
        [w[:, 0:512], w[:, 512:1024], w[:, 2304:2816], w[:, 1536:2048], w[:, 1024:1536],
         w[:, 2048:2176], w[:, 2176:2304], w[:, 2816:2888], pad, w[:, 2888:3912]],
        axis=1).astype(BF16)


def _flash_step(carry, s, vb):
    m, l, acc = carry
    m_new = jnp.maximum(m, jnp.max(s, axis=1, keepdims=True))
    alpha = jnp.exp(m - m_new)
    p = jnp.exp(s - m_new)
    l = alpha * l + jnp.sum(p, axis=1, keepdims=True)
    acc = alpha * acc + jnp.dot(p.astype(BF16), vb, preferred_element_type=F32)
    return m_new, l, acc


def _dot_nt(a, b):
    return lax.dot_general(a, b, (((1,), (1,)), ((), ())), preferred_element_type=F32)


def _diff_attn_kernel(q_ref, k_ref, v_ref, lq1_ref, lk1_ref, lq2_ref, lk2_ref, g_ref, o_ref,
                      *, tq, tk, lambda_init):
    i = pl.program_id(2)
    q = q_ref[0].astype(F32) * (DIFF_QK ** -0.5)
    lane = lax.broadcasted_iota(I32, q.shape, 1)
    q2 = jnp.concatenate([jnp.where(lane < DIFF_QK, q, 0.0),
                          jnp.where(lane >= DIFF_QK, q, 0.0)], axis=0).astype(BF16)

    def block(j, carry, diag):
        kb = k_ref[0, pl.ds(j * tk, tk), :]
        vb = v_ref[0, pl.ds(j * tk, tk), :]
        s = _dot_nt(q2, kb)
        if diag:
            row = lax.broadcasted_iota(I32, s.shape, 0)
            row = i * tq + jnp.where(row >= tq, row - tq, row)
            col = j * tk + lax.broadcasted_iota(I32, s.shape, 1)
            s = jnp.where(col <= row, s, NEG_BIG)
        return _flash_step(carry, s, vb)

    init = (jnp.full((2 * tq, 1), NEG_BIG, F32), jnp.zeros((2 * tq, 1), F32),
            jnp.zeros((2 * tq, DIFF_V), F32))
    nfull = (i * tq) // tk
    carry = lax.fori_loop(0, nfull, lambda j, c: block(j, c, False), init)
    _, l, acc = block(nfull, carry, True)
    o = acc / l
    lam = (jnp.exp(jnp.sum(lq1_ref[...] * lk1_ref[...], keepdims=True))
           - jnp.exp(jnp.sum(lq2_ref[...] * lk2_ref[...], keepdims=True)) + lambda_init)
    out = o[:tq] - lam * o[tq:]
    out = _rms(out, g_ref[...]) * (1.0 - lambda_init)
    o_ref[0] = out.astype(o_ref.dtype)


def _diff_attn(z3, lq1, lk1, lq2, lk2, subln_g, lambda_init):
    b, l, _ = z3.shape
    tq = min(l, 256)
    tk = min(l, 512)
    vec = pl.BlockSpec((1, DIFF_QK), lambda bi, h, i: (0, 0))
    return pl.pallas_call(
        functools.partial(_diff_attn_kernel, tq=tq, tk=tk, lambda_init=lambda_init),
        out_shape=jax.ShapeDtypeStruct((b, l, DIFF_HEADS * DIFF_V), BF16),
        grid=(b, DIFF_HEADS, l // tq),
        in_specs=[pl.BlockSpec((1, tq, LANES), lambda bi, h, i: (bi, i, COL_AQ + h)),
                  pl.BlockSpec((1, l, LANES), lambda bi, h, i: (bi, 0, COL_AK + h)),
                  pl.BlockSpec((1, l, LANES), lambda bi, h, i: (bi, 0, COL_AV + h)),
                  vec, vec, vec, vec,
                  pl.BlockSpec((1, DIFF_V), lambda bi, h, i: (0, 0))],
        out_specs=pl.BlockSpec((1, tq, DIFF_V), lambda bi, h, i: (bi, i, h)),
        compiler_params=_cparams(("parallel", "parallel", "arbitrary")),
        name="diff_attention",
    )(z3, z3, z3, lq1.reshape(1, -1), lk1.reshape(1, -1), lq2.reshape(1, -1),
      lk2.reshape(1, -1), subln_g.reshape(1, -1))


def _sortable_key(s):
    bits = lax.bitcast_convert_type(s, I32)
    return bits ^ ((bits >> 31) & jnp.int32(0x7FFFFFFF))


def _radix_select16(count_ge, target, rows):
    c0 = count_ge(jnp.zeros((rows, 1), I32))
    thr = jnp.where(c0 >= target, 0, -(2 ** 15)).astype(I32)

    def bit_step(it, thr):
        cand = thr | (jnp.int32(1) << (14 - it))
        return jnp.where(count_ge(cand) >= target, cand, thr)

    return lax.fori_loop(0, 15, bit_step, thr)


def _dsa_kernel(bq_ref, iq_ref, iwq_ref, bk_ref, bv_ref, ik_ref, o_ref, hi_ref, lo_ref,
                *, tq, tk, topk, seq_bits):
    i = pl.program_id(1)
    nkb = ((i + 1) * tq + tk - 1) // tk
    groups = tk // LANES
    low16 = -(2 ** 15)

    iq = iq_ref[0]
    iw = (iwq_ref[0][:, IDX_DIM:IDX_DIM + IDX_HEADS].astype(F32) * (IDX_HEADS ** -0.5)
          * (IDX_DIM ** -0.5))
    row_g = i * tq + lax.broadcasted_iota(I32, (tq, tk), 0)
    col_l = lax.broadcasted_iota(I32, (tq, tk), 1)

    def score_block(j, _):
        ikb = ik_ref[0, pl.ds(j * tk, tk), :][:, :IDX_DIM]
        sc = jnp.zeros((tq, tk), F32)
        for h in range(IDX_HEADS):
            r = _dot_nt(iq[:, h * IDX_DIM:(h + 1) * IDX_DIM], ikb)
            sc = sc + jnp.maximum(r, 0.0) * iw[:, h:h + 1]
        sc = jnp.where(j * tk + col_l <= row_g, sc + 0.0, -jnp.inf)
        key = _sortable_key(sc)
        hi_ref[j] = (key >> 16).astype(jnp.int16)
        lo_ref[j] = ((key & 0xFFFF) + low16).astype(jnp.int16)
        return 0

    lax.fori_loop(0, nkb, score_block, 0)

    one16, zero16 = jnp.int16(1), jnp.int16(0)

    def count(pred_fn):
        def blk(j, cnt):
            for gi in range(groups):
                sl = slice(gi * LANES, (gi + 1) * LANES)
                cnt = cnt + jnp.where(pred_fn(j, sl), one16, zero16)
            return cnt
        cnt = lax.fori_loop(0, nkb, blk, jnp.zeros((tq, LANES), jnp.int16))
        return jnp.sum(cnt.astype(I32), axis=1, keepdims=True)

    def as16(v):
        return v.astype(jnp.int16)

    thr_hi = _radix_select16(
        lambda c: count(lambda j, sl: hi_ref[j, :, sl] >= as16(c)), topk, tq)
    t_hi = as16(thr_hi)
    c_gt_hi = count(lambda j, sl: hi_ref[j, :, sl] > t_hi)
    need_lo = topk - c_gt_hi

    def mask_lo(j, _):
        lo_ref[j] = jnp.where(hi_ref[j] == t_hi, lo_ref[j], jnp.int16(low16))
        return 0

    lax.fori_loop(0, nkb, mask_lo, 0)
    thr_lo = _radix_select16(
        lambda c: count(lambda j, sl: lo_ref[j, :, sl] >= as16(c)), need_lo, tq)
    t_lo = as16(thr_lo)

    def in_tie(j, sl):
        return (hi_ref[j, :, sl] == t_hi) & (lo_ref[j, :, sl] == t_lo)

    c_gt_lo = count(lambda j, sl: lo_ref[j, :, sl] > t_lo)
    c_eq = count(in_tie)
    need = need_lo - c_gt_lo
    has_excess = jnp.max(jnp.where(c_eq > need, 1, 0)) > 0
    lane16 = lax.broadcasted_iota(I32, (tq, LANES), 1)

    def tie_limit():
        def step(it, q):
            cand = q + (jnp.int32(1) << (seq_bits - 1 - it))

            def pred(j, sl):
                colg = as16(lane16 + (j * tk + sl.start))
                return in_tie(j, sl) & (colg < as16(cand))
            return jnp.where(count(pred) < need, cand, q)
        return lax.fori_loop(0, seq_bits, step, jnp.zeros((tq, 1), I32))

    jlim = lax.cond(has_excess, tie_limit,
                    lambda: jnp.full((tq, 1), 2 ** seq_bits, I32))
    jlim16 = as16(jlim)

    q = bq_ref[0]
    scale = DSA_DIM ** -0.5
    zero_b, neg_b = jnp.zeros((), BF16), jnp.full((), NEG_BIG, BF16)
    row16 = as16(row_g)

    def attn_block(j, carry):
        kb = bk_ref[0, pl.ds(j * tk, tk), :]
        vb = bv_ref[0, pl.ds(j * tk, tk), :]
        hi, lo = hi_ref[j], lo_ref[j]
        col16 = as16(j * tk + col_l)
        sel = (hi > t_hi) | ((hi == t_hi) & ((lo > t_lo) | ((lo == t_lo) & (col16 <= jlim16))))
        sel = sel & (col16 <= row16)
        bias = jnp.where(sel, zero_b, neg_b).astype(F32)
        out = []
        for h in range(DSA_HEADS):
            s = _dot_nt(q[:, h * DSA_DIM:(h + 1) * DSA_DIM], kb) * scale + bias
            out.append(_flash_step(carry[h], s, vb))
        return tuple(out)

    init = tuple((jnp.full((tq, 1), NEG_BIG, F32), jnp.zeros((tq, 1), F32),
                  jnp.zeros((tq, DSA_DIM), F32)) for _ in range(DSA_HEADS))
    res = lax.fori_loop(0, nkb, attn_block, init)
    for h in range(DSA_HEADS):
        _, l, acc = res[h]
        o_ref[0, :, h * DSA_DIM:(h + 1) * DSA_DIM] = (acc / l).astype(o_ref.dtype)


def _dsa_attn(z3):
    b, l, _ = z3.shape
    tq = min(l, DSA_TQ)
    tk = min(l, 512)
    topk = min(TOPK_MAX, l // 4)
    seq_bits = max(1, (l - 1).bit_length())
    wide = 4
    return pl.pallas_call(
        functools.partial(_dsa_kernel, tq=tq, tk=tk, topk=topk, seq_bits=seq_bits),
        out_shape=jax.ShapeDtypeStruct((b, l, DSA_HEADS * DSA_DIM), BF16),
        grid=(b, l // tq),
        in_specs=[pl.BlockSpec((1, tq, 4 * LANES), lambda bi, i: (bi, i, COL_BQ // wide)),
                  pl.BlockSpec((1, tq, 4 * LANES), lambda bi, i: (bi, i, COL_IQ // wide)),
                  pl.BlockSpec((1, tq, LANES), lambda bi, i: (bi, i, COL_IKW)),
                  pl.BlockSpec((1, l, LANES), lambda bi, i: (bi, 0, COL_BK)),
                  pl.BlockSpec((1, l, LANES), lambda bi, i: (bi, 0, COL_BV)),
                  pl.BlockSpec((1, l, LANES), lambda bi, i: (bi, 0, COL_IKW))],
        out_specs=pl.BlockSpec((1, tq, DSA_HEADS * DSA_DIM), lambda bi, i: (bi, i, 0)),
        scratch_shapes=[pltpu.VMEM((l // tk, tq, tk), jnp.int16),
                        pltpu.VMEM((l // tk, tq, tk), jnp.int16)],
        compiler_params=_cparams(("parallel", "arbitrary")),
        name="dsa_attention",
    )(z3, z3, z3, z3, z3, z3)


def _cmul(ar, ai, br, bi):
    return ar * br - ai * bi, ar * bi + ai * br


def _ssm_prep_kernel(lre_ref, lim_ref, lstep_ref, btr_ref, bti_ref, cr_ref, ci_ref,
                     w1_ref, mt_ref, tzt_ref, at_ref, tz_ref, *, tc, gb):
    c = SSM_C
    p = SSM_P
    for gi in range(gb):
        lr = lre_ref[gi].reshape(1, 1, p)
        li = lim_ref[gi].reshape(1, 1, p)
        step = jnp.exp(lstep_ref[gi]).reshape(1, 1, 1)
        mag = jnp.exp(lr * step)
        a_re, a_im = mag * jnp.cos(li * step), mag * jnp.sin(li * step)
        den = lr * lr + li * li
        nr, ni = a_re - 1.0, a_im
        f_re, f_im = (nr * lr + ni * li) / den, (ni * lr - nr * li) / den
        bt_r, bt_i = btr_ref[gi][None], bti_ref[gi][None]
        bb_re = f_re * bt_r - f_im * bt_i
        bb_im = f_re * bt_i + f_im * bt_r
        fr, fi = jnp.ones_like(a_re), jnp.zeros_like(a_im)
        rr, ri = fr, fi
        pr, pi = a_re, a_im
        n = 1
        while n < tc:
            xr, xi = _cmul(fr, fi, pr, pi)
            fr, fi = jnp.concatenate([fr, xr], 0), jnp.concatenate([fi, xi], 0)
            xr, xi = _cmul(rr, ri, pr, pi)
            rr, ri = jnp.concatenate([xr, rr], 0), jnp.concatenate([xi, ri], 0)
            pr, pi = _cmul(pr, pi, pr, pi)
            n *= 2
        at_ref[gi] = jnp.concatenate([pr[0], pi[0]], axis=1)
        wr, wi = _cmul(rr, ri, bb_re, bb_im)
        w1 = jnp.concatenate([wr, wi], axis=2).reshape(tc * c, 2 * p)
        w1_ref[gi] = w1.astype(w1_ref.dtype)
        f1r, f1i = _cmul(fr, fi, a_re, a_im)
        c_re, c_im = cr_ref[gi][None], ci_ref[gi][None]
        mr, mi = _cmul(f1r, f1i, c_re, c_im)
        mt_ref[gi] = jnp.concatenate([mr, -mi], axis=2).reshape(tc * c, 2 * p).astype(mt_ref.dtype)
        er, ei = _cmul(fr, fi, c_re, c_im)
        e2 = jnp.concatenate([er, -ei], axis=2).reshape(tc * c, 2 * p)
        bcat = jnp.concatenate([bb_re[0], bb_im[0]], axis=1)
        kflat = lax.dot_general(e2, bcat, (((1,), (1,)), ((), ())),
                                precision=lax.Precision.HIGHEST,
                                preferred_element_type=F32)
        tz_ref[...] = jnp.zeros(tz_ref.shape, tz_ref.dtype)
        for s in range(tc):
            tz_ref[s * c:, s * c:(s + 1) * c] = kflat[:(tc - s) * c, :]
        tzt_ref[gi] = tz_ref[...].astype(tzt_ref.dtype)


def _ssm_prep(lam_re, lam_im, log_step, b_re, b_im, c_re, c_im, tc, gb):
    g, p = lam_re.shape
    c = SSM_C
    n = tc * c
    vecp = pl.BlockSpec((gb, 1, p), lambda i: (i, 0, 0))
    mat = pl.BlockSpec((gb, c, p), lambda i: (i, 0, 0))
    op = pl.BlockSpec((gb, n, 2 * p), lambda i: (i, 0, 0))
    return pl.pallas_call(
        functools.partial(_ssm_prep_kernel, tc=tc, gb=gb),
        out_shape=(jax.ShapeDtypeStruct((g, n, 2 * p), BF16),
                   jax.ShapeDtypeStruct((g, n, 2 * p), BF16),
                   jax.ShapeDtypeStruct((g, n, n), BF16),
                   jax.ShapeDtypeStruct((g, 1, 2 * p), F32)),
        grid=(g // gb,),
        in_specs=[vecp, vecp, pl.BlockSpec((gb, 1, 1), lambda i: (i, 0, 0)), mat, mat, mat, mat],
        out_specs=(op, op, pl.BlockSpec((gb, n, n), lambda i: (i, 0, 0)),
                   pl.BlockSpec((gb, 1, 2 * p), lambda i: (i, 0, 0))),
        scratch_shapes=[pltpu.VMEM((n, n), F32)],
        compiler_params=_cparams(("parallel",)),
        name="ssm_prep",
    )(lam_re.reshape(g, 1, p), lam_im.reshape(g, 1, p), log_step.reshape(g, 1, 1),
      jnp.swapaxes(b_re, 1, 2), jnp.swapaxes(b_im, 1, 2), c_re, c_im)


def _ssm_chunk_state_kernel(u_ref, w1_ref, s_ref, *, gb):
    for gi in range(gb):
        s_ref[gi] = jnp.dot(u_ref[gi], w1_ref[gi], preferred_element_type=F32)


def _ssm_scan_kernel(s_ref, at_ref, x_ref, st_ref, *, nchunks, bsz):
    a = jnp.broadcast_to(at_ref[...], st_ref.shape)
    lane = lax.broadcasted_iota(I32, a.shape, 2)
    a_sw = pltpu.roll(a, SSM_P, 2)
    a_re = jnp.where(lane < SSM_P, a, a_sw)
    a_im_s = jnp.where(lane < SSM_P, -a_sw, a)
    st_ref[...] = jnp.zeros(st_ref.shape, st_ref.dtype)

    def step(c, _):
        x = st_ref[...]
        x_ref[:, pl.ds(c * bsz, bsz), :] = x
        s = s_ref[:, pl.ds(c * bsz, bsz), :]
        st_ref[...] = x * a_re + pltpu.roll(x, SSM_P, 2) * a_im_s + s
        return 0

    lax.fori_loop(0, nchunks, step, 0)


def _ssm_out_kernel(u_ref, x_ref, tzt_ref, mt_ref, d_ref, y_ref, *, gb):
    for gi in range(gb):
        u = u_ref[gi]
        y = _dot_nt(u, tzt_ref[gi]) + _dot_nt(x_ref[gi].astype(BF16), mt_ref[gi])
        y = y + u.astype(F32) * d_ref[gi]
        y_ref[gi] = jax.nn.gelu(y).astype(y_ref.dtype)


def _s5(u3, ops, d_skip, tc, gb):
    w1, mt, tzt, at = ops
    b, l, width = u3.shape
    c = SSM_C
    g = width // c
    nch = l // tc
    rows = nch * b
    n = tc * c
    p2 = 2 * SSM_P
    ug = u3.reshape(b, nch, tc, g, c).transpose(3, 1, 0, 2, 4).reshape(g, rows, n)
    blk_u = pl.BlockSpec((gb, rows, n), lambda i: (i, 0, 0))
    blk_s = pl.BlockSpec((gb, rows, p2), lambda i: (i, 0, 0))
    blk_op = pl.BlockSpec((gb, n, p2), lambda i: (i, 0, 0))
    s = pl.pallas_call(
        functools.partial(_ssm_chunk_state_kernel, gb=gb),
        out_shape=jax.ShapeDtypeStruct((g, rows, p2), F32),
        grid=(g // gb,),
        in_specs=[blk_u, blk_op],
        out_specs=blk_s,
        compiler_params=_cparams(("parallel",)),
        name="ssm_chunk_state",
    )(ug, w1)
    xprev = pl.pallas_call(
        functools.partial(_ssm_scan_kernel, nchunks=nch, bsz=b),
        out_shape=jax.ShapeDtypeStruct((g, rows, p2), F32),
        grid=(g // gb,),
        in_specs=[blk_s, pl.BlockSpec((gb, 1, p2), lambda i: (i, 0, 0))],
        out_specs=blk_s,
        scratch_shapes=[pltpu.VMEM((gb, b, p2), F32)],
        compiler_params=_cparams(("parallel",)),
        name="ssm_chunk_scan",
    )(s, at)
    dt = jnp.tile(d_skip.reshape(g, 1, c), (1, 1, tc)).astype(F32)
    y = pl.pallas_call(
        functools.partial(_ssm_out_kernel, gb=gb),
        out_shape=jax.ShapeDtypeStruct((g, rows, n), BF16),
        grid=(g // gb,),
        in_specs=[blk_u, blk_s, pl.BlockSpec((gb, n, n), lambda i: (i, 0, 0)), blk_op,
                  pl.BlockSpec((gb, 1, n), lambda i: (i, 0, 0))],
        out_specs=blk_u,
        compiler_params=_cparams(("parallel",)),
        name="ssm_chunk_out",
    )(ug, xprev, tzt, mt, dt)
    return y.reshape(g, nch, b, tc, c).transpose(2, 1, 3, 0, 4).reshape(b, l, width)


def _glu_kernel(y_ref, yn_ref, w_ref, o_ref):
    gate = jnp.dot(y_ref[...], w_ref[...], preferred_element_type=F32)
    o_ref[...] = (yn_ref[...].astype(F32) * jax.nn.sigmoid(gate)).astype(o_ref.dtype)


def _glu(y2d, w):
    t, k = y2d.shape
    tm = min(t, 1024)
    tn = min(k, 512)
    return pl.pallas_call(
        _glu_kernel,
        out_shape=jax.ShapeDtypeStruct((t, k), BF16),
        grid=(t // tm, k // tn),
        in_specs=[pl.BlockSpec((tm, k), lambda m, n: (m, 0)),
                  pl.BlockSpec((tm, tn), lambda m, n: (m, n)),
                  pl.BlockSpec((k, tn), lambda m, n: (0, n))],
        out_specs=pl.BlockSpec((tm, tn), lambda m, n: (m, n)),
        compiler_params=_cparams(("parallel", "arbitrary")),
        name="ssm_glu",
    )(y2d, y2d, w)


def _outproj_kernel(h_ref, ya_ref, yb_ref, yc_ref, wa_ref, wb_ref, wc_ref, o_ref):
    acc = jnp.dot(ya_ref[...], wa_ref[...], preferred_element_type=F32)
    acc += jnp.dot(yb_ref[...], wb_ref[...], preferred_element_type=F32)
    acc += jnp.dot(yc_ref[...], wc_ref[...], preferred_element_type=F32)
    o_ref[...] = h_ref[...] + acc


def _outproj(h2d, ya, yb, yc, w):
    t, d = h2d.shape
    ka, kb, kc = ya.shape[1], yb.shape[1], yc.shape[1]
    tm = min(t, 512)
    tn = min(d, 512)
    return pl.pallas_call(
        _outproj_kernel,
        out_shape=jax.ShapeDtypeStruct((t, d), F32),
        grid=(t // tm, d // tn),
        in_specs=[pl.BlockSpec((tm, tn), lambda m, n: (m, n)),
                  pl.BlockSpec((tm, ka), lambda m, n: (m, 0)),
                  pl.BlockSpec((tm, kb), lambda m, n: (m, 0)),
                  pl.BlockSpec((tm, kc), lambda m, n: (m, 0)),
                  pl.BlockSpec((ka, tn), lambda m, n: (0, n)),
                  pl.BlockSpec((kb, tn), lambda m, n: (0, n)),
                  pl.BlockSpec((kc, tn), lambda m, n: (0, n))],
        out_specs=pl.BlockSpec((tm, tn), lambda m, n: (m, n)),
        compiler_params=_cparams(("parallel", "arbitrary")),
        name="out_proj",
    )(h2d, ya, yb, yc, w[:ka], w[ka:ka + kb], w[ka + kb:])


def _mlp_kernel(h_ref, g_ref, wu_ref, wd_ref, o_ref, xn_ref, acc_ref):
    f = pl.program_id(1)

    @pl.when(f == 0)
    def _():
        xn_ref[...] = _rms(h_ref[...], g_ref[...]).astype(BF16)
        acc_ref[...] = h_ref[...]

    hid = jnp.dot(xn_ref[...], wu_ref[...], preferred_element_type=F32)
    hid = jnp.square(jnp.maximum(hid, 0.0)).astype(BF16)
    acc_ref[...] += jnp.dot(hid, wd_ref[...], preferred_element_type=F32)

    @pl.when(f == pl.num_programs(1) - 1)
    def _():
        o_ref[...] = acc_ref[...]


def _mlp(h2d, g, w_up, w_down):
    t, d = h2d.shape
    ff = w_up.shape[1]
    tm = min(t, 512)
    tf = min(ff, 512)
    return pl.pallas_call(
        _mlp_kernel,
        out_shape=jax.ShapeDtypeStruct((t, d), F32),
        grid=(t // tm, ff // tf),
        in_specs=[pl.BlockSpec((tm, d), lambda m, f: (m, 0)),
                  pl.BlockSpec((1, d), lambda m, f: (0, 0)),
                  pl.BlockSpec((d, tf), lambda m, f: (0, f)),
                  pl.BlockSpec((tf, d), lambda m, f: (f, 0))],
        out_specs=pl.BlockSpec((tm, d), lambda m, f: (m, 0)),
        scratch_shapes=[pltpu.VMEM((tm, d), BF16), pltpu.VMEM((tm, d), F32)],
        compiler_params=_cparams(("parallel", "arbitrary")),
        name="mlp_relu2",
    )(h2d, g.reshape(1, d), w_up, w_down)


def _ple_kernel(h_ref, g_ref, p_ref, wg_ref, wp_ref, fg_ref, o_ref, *, tn, final):
    h = h_ref[...]
    xn = _rms(h, g_ref[...]).astype(BF16)
    pb = p_ref[...]
    d = h.shape[1]
    for j in range(d // tn):
        sl = slice(j * tn, (j + 1) * tn)
        gate = jax.nn.sigmoid(jnp.dot(xn, wg_ref[:, sl], preferred_element_type=F32))
        proj = jnp.dot(pb, wp_ref[:, sl], preferred_element_type=F32)
        o_ref[:, sl] = h[:, sl] + gate * proj
    if final:
        o_ref[...] = _rms(o_ref[...], fg_ref[...])


def _ple(h2d, g, p2d, wg, wp, final_g, final):
    t, d = h2d.shape
    pd = p2d.shape[1]
    tm = min(t, 256)
    return pl.pallas_call(
        functools.partial(_ple_kernel, tn=min(d, 512), final=final),
        out_shape=jax.ShapeDtypeStruct((t, d), F32),
        grid=(t // tm,),
        in_specs=[pl.BlockSpec((tm, d), lambda m: (m, 0)),
                  pl.BlockSpec((1, d), lambda m: (0, 0)),
                  pl.BlockSpec((tm, pd), lambda m: (m, 0)),
                  pl.BlockSpec((d, d), lambda m: (0, 0)),
                  pl.BlockSpec((pd, d), lambda m: (0, 0)),
                  pl.BlockSpec((1, d), lambda m: (0, 0))],
        out_specs=pl.BlockSpec((tm, d), lambda m: (m, 0)),
        compiler_params=_cparams(("parallel",)),
        name="ple_gate",
    )(h2d, g.reshape(1, d), p2d, wg, wp, final_g.reshape(1, d))


SSM_CHUNK = 32
SSM_GROUP_BLOCK = 8


def kernel(x, p, positions, norm_mix_g, w_in, w_out, diff_lq1, diff_lk1, diff_lq2, diff_lk2,
           diff_subln_g, ssm_lambda_re, ssm_lambda_im, ssm_log_step, ssm_B_re, ssm_B_im,
           ssm_C_re, ssm_C_im, ssm_D, ssm_w_glu, norm_mlp_g, w_up, w_down, norm_ple_g,
           w_ple_gate, w_ple_proj, final_g):
    b, l, d = x.shape
    depth = w_in.shape[0]
    t = b * l
    tc = min(SSM_CHUNK, l)
    tabs = _rope_tables(positions)
    h = x.reshape(t, d)
    for i in range(depth):
        lambda_init = 0.8 - 0.6 * math.exp(-0.3 * i)
        z = _inproj(h, norm_mix_g[i], _permute_w_in(w_in[i]), tabs)
        z3 = z.reshape(b, l, Z_WIDTH)
        y_a = _diff_attn(z3, diff_lq1[i], diff_lk1[i], diff_lq2[i], diff_lk2[i],
                         diff_subln_g[i], lambda_init)
        y_b = _dsa_attn(z3)
        ops = _ssm_prep(ssm_lambda_re[i], ssm_lambda_im[i], ssm_log_step[i], ssm_B_re[i],
                        ssm_B_im[i], ssm_C_re[i], ssm_C_im[i], tc, SSM_GROUP_BLOCK)
        y_s = _s5(z3[:, :, COL_CU * LANES:], ops, ssm_D[i], tc, SSM_GROUP_BLOCK)
        y_c = _glu(y_s.reshape(t, -1), ssm_w_glu[i].astype(BF16))
        h = _outproj(h, y_a.reshape(t, -1), y_b.reshape(t, -1), y_c, w_out[i].astype(BF16))
        h = _mlp(h, norm_mlp_g[i], w_up[i].astype(BF16), w_down[i].astype(BF16))
        h = _ple(h, norm_ple_g[i], p[i].reshape(t, -1).astype(BF16), w_ple_gate[i].astype(BF16),
                 w_ple_proj[i].astype(BF16), final_g, final=(i == depth - 1))
    return h.reshape(b, l, d)
```

```python
import functools
import math

import jax
import jax.numpy as jnp
from jax import lax
from jax.experimental import pallas as pl
from jax.experimental.pallas import tpu as pltpu

F32 = jnp.float32
BF16 = jnp.bfloat16
I32 = jnp.int32

LANES = 128
EPS = 1e-6
ROPE_THETA = 10000.0

DIFF_QK = 64
DIFF_V = 128
DIFF_HEADS = 4
DSA_DIM = 128
DSA_HEADS = 4
IDX_HEADS = 8
IDX_DIM = 64
TOPK_MAX = 256
SSM_C = 16
SSM_P = 64

COL_AQ, COL_AK, COL_IQ, COL_BQ, COL_AV = 0, 4, 8, 12, 16
COL_BK, COL_BV, COL_IKW, COL_CU = 20, 21, 22, 24
Z_WIDTH = 32 * LANES
IN_TN = 512

DSA_TQ = 128
NEG_BIG = -1e30
LOG2E = math.log2(math.e)
VMEM_LIMIT = 56 * 1024 * 1024


def _cparams(sem):
    return pltpu.CompilerParams(dimension_semantics=sem, vmem_limit_bytes=VMEM_LIMIT)


def _rope_tables_kernel(pos_ref, f64_ref, f128_ref, c64_ref, s64_ref, c128_ref, s128_ref):
    pos = pos_ref[...]
    a64 = pos * f64_ref[...]
    a128 = pos * f128_ref[...]
    lane = lax.broadcasted_iota(I32, a64.shape, 1)
    c64_ref[...] = jnp.cos(a64)
    sn = jnp.sin(a64)
    s64_ref[...] = jnp.where((lane & 32) == 0, -sn, sn)
    c128_ref[...] = jnp.cos(a128)
    sn = jnp.sin(a128)
    s128_ref[...] = jnp.where((lane & 64) == 0, -sn, sn)


def _rope_tables(positions):
    t = positions.size
    pos = positions.reshape(t, 1).astype(F32)
    fr64 = ROPE_THETA ** (-jnp.arange(0, 64, 2, dtype=F32) / 64)
    fr128 = ROPE_THETA ** (-jnp.arange(0, 128, 2, dtype=F32) / 128)
    f64 = jnp.tile(fr64, 4).reshape(1, LANES)
    f128 = jnp.tile(fr128, 2).reshape(1, LANES)
    tm = min(t, 1024)
    tab = jax.ShapeDtypeStruct((t, LANES), F32)
    row = pl.BlockSpec((tm, LANES), lambda i: (i, 0))
    return pl.pallas_call(
        _rope_tables_kernel,
        out_shape=(tab, tab, tab, tab),
        grid=(t // tm,),
        in_specs=[pl.BlockSpec((tm, 1), lambda i: (i, 0)),
                  pl.BlockSpec((1, LANES), lambda i: (0, 0)),
                  pl.BlockSpec((1, LANES), lambda i: (0, 0))],
        out_specs=(row, row, row, row),
        compiler_params=_cparams(("parallel",)),
        name="rope_tables",
    )(pos, f64, f128)


def _rms(x, g):
    ms = jnp.mean(x * x, axis=-1, keepdims=True)
    return x * lax.rsqrt(ms + EPS) * g


def _swap_halves(a, half):
    lane = lax.broadcasted_iota(I32, a.shape, 1)
    return jnp.where((lane & half) == 0,
                     pltpu.roll(a, LANES - half, 1), pltpu.roll(a, half, 1))


def _inproj_kernel(x_ref, g_ref, w_ref, c64_ref, s64_ref, c128_ref, s128_ref, z_ref, xn_ref):
    n = pl.program_id(1)

    @pl.when(n == 0)
    def _():
        xn_ref[...] = _rms(x_ref[...], g_ref[...]).astype(BF16)

    acc = jnp.dot(xn_ref[...], w_ref[...], preferred_element_type=F32)
    groups = acc.shape[1] // LANES

    def rope(a, half):
        if half == 32:
            return a * c64_ref[...] + _swap_halves(a, 32) * s64_ref[...]
        return a * c128_ref[...] + _swap_halves(a, 64) * s128_ref[...]

    def grp(j):
        return acc[:, j * LANES:(j + 1) * LANES]

    def put(j, v):
        z_ref[:, j * LANES:(j + 1) * LANES] = v.astype(z_ref.dtype)

    @pl.when(n < 3)
    def _():
        for j in range(groups):
            put(j, rope(grp(j), 32))

    @pl.when(n == 3)
    def _():
        for j in range(groups):
            put(j, rope(grp(j), 64))

    @pl.when(n == 5)
    def _():
        put(0, rope(grp(0), 64))
        put(1, grp(1))
        a = grp(2)
        lane = lax.broadcasted_iota(I32, a.shape, 1)
        put(2, jnp.where(lane < IDX_DIM, rope(a, 32), a))
        put(3, grp(3))

    @pl.when((n == 4) | (n > 5))
    def _():
        z_ref[...] = acc.astype(z_ref.dtype)


def _inproj(h2d, g, w, tabs):
    t, d = h2d.shape
    tm = min(t, 512)
    tn = IN_TN
    c64, s64, c128, s128 = tabs
    tab = pl.BlockSpec((tm, LANES), lambda m, n: (m, 0))
    return pl.pallas_call(
        _inproj_kernel,
        out_shape=jax.ShapeDtypeStruct((t, Z_WIDTH), BF16),
        grid=(t // tm, Z_WIDTH // tn),
        in_specs=[pl.BlockSpec((tm, d), lambda m, n: (m, 0)),
                  pl.BlockSpec((1, d), lambda m, n: (0, 0)),
                  pl.BlockSpec((d, tn), lambda m, n: (0, n)),
                  tab, tab, tab, tab],
        out_specs=pl.BlockSpec((tm, tn), lambda m, n: (m, n)),
        scratch_shapes=[pltpu.VMEM((tm, d), BF16)],
        compiler_params=_cparams(("parallel", "arbitrary")),
        name="norm_inproj_rope",
    )(h2d, g.reshape(1, d), w, c64, s64, c128, s128)


def _permute_w_in(w):
    d = w.shape[0]
    pad = jnp.zeros((d, 56 + LANES), w.dtype)
    return jnp.concatenate(
        [w[:, 0:512], w[:, 512:1024], w[:, 2304:2816], w[:, 1536:2048], w[:, 1024:1536],
         w[:, 2048:2176], w[:, 2176:2304], w[:, 2816:2888], pad, w[:, 2888:3912]],
        axis=1).astype(BF16)


def _flash_step(carry, s, vb1):
    m, acc = carry
    m_new = jnp.maximum(m, jnp.max(s, axis=1, keepdims=True))
    alpha = jnp.exp2(m - m_new)
    p = jnp.exp2(s - m_new)
    acc = alpha * acc + jnp.dot(p.astype(BF16), vb1, preferred_element_type=F32)
    return m_new, acc


def _with_ones(vb):
    return jnp.concatenate([vb, jnp.ones_like(vb)], axis=1)


def _flash_init(rows, dv):
    return jnp.full((rows, 1), NEG_BIG, F32), jnp.zeros((rows, 2 * dv), F32)


def _flash_finish(acc, dv):
    return acc[:, :dv] / acc[:, dv:]


def _dot_nt(a, b):
    return lax.dot_general(a, b, (((1,), (1,)), ((), ())), preferred_element_type=F32)


def _diff_attn_kernel(q_ref, k_ref, v_ref, lq1_ref, lk1_ref, lq2_ref, lk2_ref, g_ref, o_ref,
                      *, tq, tk, lambda_init):
    i = pl.program_id(2)
    q = q_ref[0].astype(F32) * (DIFF_QK ** -0.5)
    lane = lax.broadcasted_iota(I32, q.shape, 1)
    q2 = jnp.concatenate([jnp.where(lane < DIFF_QK, q, 0.0),
                          jnp.where(lane >= DIFF_QK, q, 0.0)], axis=0).astype(BF16)

    def block(j, carry, diag):
        kb = k_ref[0, pl.ds(j * tk, tk), :]
        vb = _with_ones(v_ref[0, pl.ds(j * tk, tk), :])
        s = _dot_nt(q2, kb) * LOG2E
        if diag:
            row = lax.broadcasted_iota(I32, s.shape, 0)
            row = i * tq + jnp.where(row >= tq, row - tq, row)
            col = j * tk + lax.broadcasted_iota(I32, s.shape, 1)
            s = jnp.where(col <= row, s, NEG_BIG)
        return _flash_step(carry, s, vb)

    init = _flash_init(2 * tq, DIFF_V)
    nfull = (i * tq) // tk
    carry = lax.fori_loop(0, nfull, lambda j, c: block(j, c, False), init)
    _, acc = block(nfull, carry, True)
    o = _flash_finish(acc, DIFF_V)
    lam = (jnp.exp(jnp.sum(lq1_ref[...] * lk1_ref[...], keepdims=True))
           - jnp.exp(jnp.sum(lq2_ref[...] * lk2_ref[...], keepdims=True)) + lambda_init)
    out = o[:tq] - lam * o[tq:]
    out = _rms(out, g_ref[...]) * (1.0 - lambda_init)
    o_ref[0] = out.astype(o_ref.dtype)


def _diff_attn(z3, lq1, lk1, lq2, lk2, subln_g, lambda_init):
    b, l, _ = z3.shape
    tq = min(l, 256)
    tk = min(l, 512)
    vec = pl.BlockSpec((1, DIFF_QK), lambda bi, h, i: (0, 0))
    return pl.pallas_call(
        functools.partial(_diff_attn_kernel, tq=tq, tk=tk, lambda_init=lambda_init),
        out_shape=jax.ShapeDtypeStruct((b, l, DIFF_HEADS * DIFF_V), BF16),
        grid=(b, DIFF_HEADS, l // tq),
        in_specs=[pl.BlockSpec((1, tq, LANES), lambda bi, h, i: (bi, i, COL_AQ + h)),
                  pl.BlockSpec((1, l, LANES), lambda bi, h, i: (bi, 0, COL_AK + h)),
                  pl.BlockSpec((1, l, LANES), lambda bi, h, i: (bi, 0, COL_AV + h)),
                  vec, vec, vec, vec,
                  pl.BlockSpec((1, DIFF_V), lambda bi, h, i: (0, 0))],
        out_specs=pl.BlockSpec((1, tq, DIFF_V), lambda bi, h, i: (bi, i, h)),
        compiler_params=_cparams(("parallel", "parallel", "arbitrary")),
        name="diff_attention",
    )(z3, z3, z3, lq1.reshape(1, -1), lk1.reshape(1, -1), lq2.reshape(1, -1),
      lk2.reshape(1, -1), subln_g.reshape(1, -1))


def _sortable_key(s):
    bits = lax.bitcast_convert_type(s, I32)
    return bits ^ ((bits >> 31) & jnp.int32(0x7FFFFFFF))


def _dsa_kernel(bq_ref, iq_ref, iwq_ref, bk_ref, bv_ref, ik_ref, o_ref, keys_ref,
                *, tq, tk, topk, seq_bits):
    i = pl.program_id(1)
    nkb = ((i + 1) * tq + tk - 1) // tk
    groups = tk // LANES

    iq = iq_ref[0]
    iw = (iwq_ref[0][:, IDX_DIM:IDX_DIM + IDX_HEADS].astype(F32) * (IDX_HEADS ** -0.5)
          * (IDX_DIM ** -0.5))
    row_g = i * tq + lax.broadcasted_iota(I32, (tq, tk), 0)
    col_l = lax.broadcasted_iota(I32, (tq, tk), 1)

    def score_block(j, _):
        ikb = ik_ref[0, pl.ds(j * tk, tk), :][:, :IDX_DIM]
        sc = jnp.zeros((tq, tk), F32)
        for h in range(IDX_HEADS):
            r = _dot_nt(iq[:, h * IDX_DIM:(h + 1) * IDX_DIM], ikb)
            sc = sc + jnp.maximum(r, 0.0) * iw[:, h:h + 1]
        sc = jnp.where(j * tk + col_l <= row_g, sc + 0.0, -jnp.inf)
        keys_ref[j] = _sortable_key(sc)
        return 0

    lax.fori_loop(0, nkb, score_block, 0)

    lane_l = lax.broadcasted_iota(I32, (tq, LANES), 1)

    def count(pred_fn):
        def blk(j, cnt):
            for gi in range(groups):
                kk = keys_ref[j, :, gi * LANES:(gi + 1) * LANES]
                colg = lane_l + (j * tk + gi * LANES)
                cnt = cnt + jnp.where(pred_fn(kk, colg), 1, 0)
            return cnt
        cnt = lax.fori_loop(0, nkb, blk, jnp.zeros((tq, LANES), I32))
        return jnp.sum(cnt, axis=1, keepdims=True)

    int_min = jnp.int32(-2 ** 31)
    c0 = count(lambda kk, c: kk >= 0)
    thr = jnp.where(c0 >= topk, jnp.zeros((tq, 1), I32), jnp.full((tq, 1), int_min, I32))

    def bit_step(it, thr):
        cand = thr | (jnp.int32(1) << (30 - it))
        c = count(lambda kk, _: kk >= cand)
        return jnp.where(c >= topk, cand, thr)

    thr = lax.fori_loop(0, 31, bit_step, thr)

    c_gt = count(lambda kk, _: kk > thr)
    c_eq = count(lambda kk, _: kk == thr)
    need = topk - c_gt
    has_excess = jnp.max(jnp.where(c_eq > need, 1, 0)) > 0

    def tie_limit():
        def step(it, q):
            cand = q + (jnp.int32(1) << (seq_bits - 1 - it))
            c = count(lambda kk, colg: (kk == thr) & (colg < cand))
            return jnp.where(c < need, cand, q)
        return lax.fori_loop(0, seq_bits, step, jnp.zeros((tq, 1), I32))

    jlim = lax.cond(has_excess, tie_limit,
                    lambda: jnp.full((tq, 1), 2 ** seq_bits, I32))

    q = bq_ref[0]
    q4 = jnp.concatenate([q[:, h * DSA_DIM:(h + 1) * DSA_DIM] for h in range(DSA_HEADS)], axis=0)
    scale = DSA_DIM ** -0.5 * LOG2E

    def attn_block(j, carry):
        kb = bk_ref[0, pl.ds(j * tk, tk), :]
        vb = _with_ones(bv_ref[0, pl.ds(j * tk, tk), :])
        kk = keys_ref[j]
        colg = j * tk + col_l
        sel = ((kk > thr) | ((kk == thr) & (colg <= jlim))) & (colg <= row_g)
        bias = jnp.where(sel, 0.0, NEG_BIG)
        s = _dot_nt(q4, kb) * scale
        s = s + jnp.concatenate([bias] * DSA_HEADS, axis=0)
        return _flash_step(carry, s, vb)

    rows = DSA_HEADS * tq
    _, acc = lax.fori_loop(0, nkb, attn_block, _flash_init(rows, DSA_DIM))
    o = _flash_finish(acc, DSA_DIM)
    for h in range(DSA_HEADS):
        o_ref[0, :, h * DSA_DIM:(h + 1) * DSA_DIM] = o[h * tq:(h + 1) * tq].astype(o_ref.dtype)


def _dsa_attn(z3):
    b, l, _ = z3.shape
    tq = min(l, DSA_TQ)
    tk = min(l, 512)
    topk = min(TOPK_MAX, l // 4)
    seq_bits = max(1, (l - 1).bit_length())
    wide = 4
    return pl.pallas_call(
        functools.partial(_dsa_kernel, tq=tq, tk=tk, topk=topk, seq_bits=seq_bits),
        out_shape=jax.ShapeDtypeStruct((b, l, DSA_HEADS * DSA_DIM), BF16),
        grid=(b, l // tq),
        in_specs=[pl.BlockSpec((1, tq, 4 * LANES), lambda bi, i: (bi, i, COL_BQ // wide)),
                  pl.BlockSpec((1, tq, 4 * LANES), lambda bi, i: (bi, i, COL_IQ // wide)),
                  pl.BlockSpec((1, tq, LANES), lambda bi, i: (bi, i, COL_IKW)),
                  pl.BlockSpec((1, l, LANES), lambda bi, i: (bi, 0, COL_BK)),
                  pl.BlockSpec((1, l, LANES), lambda bi, i: (bi, 0, COL_BV)),
                  pl.BlockSpec((1, l, LANES), lambda bi, i: (bi, 0, COL_IKW))],
        out_specs=pl.BlockSpec((1, tq, DSA_HEADS * DSA_DIM), lambda bi, i: (bi, i, 0)),
        scratch_shapes=[pltpu.VMEM((l // tk, tq, tk), I32)],
        compiler_params=_cparams(("parallel", "arbitrary")),
        name="dsa_attention",
    )(z3, z3, z3, z3, z3, z3)


def _cmul(ar, ai, br, bi):
    return ar * br - ai * bi, ar * bi + ai * br


def _ssm_prep_kernel(lre_ref, lim_ref, lstep_ref, btr_ref, bti_ref, cr_ref, ci_ref,
                     w1_ref, mt_ref, tzt_ref, at_ref, tz_ref, *, tc, gb):
    c = SSM_C
    p = SSM_P
    for gi in range(gb):
        lr = lre_ref[gi].reshape(1, 1, p)
        li = lim_ref[gi].reshape(1, 1, p)
        step = jnp.exp(lstep_ref[gi]).reshape(1, 1, 1)
        mag = jnp.exp(lr * step)
        a_re, a_im = mag * jnp.cos(li * step), mag * jnp.sin(li * step)
        den = lr * lr + li * li
        nr, ni = a_re - 1.0, a_im
        f_re, f_im = (nr * lr + ni * li) / den, (ni * lr - nr * li) / den
        bt_r, bt_i = btr_ref[gi][None], bti_ref[gi][None]
        bb_re = f_re * bt_r - f_im * bt_i
        bb_im = f_re * bt_i + f_im * bt_r
        fr, fi = jnp.ones_like(a_re), jnp.zeros_like(a_im)
        rr, ri = fr, fi
        pr, pi = a_re, a_im
        n = 1
        while n < tc:
            xr, xi = _cmul(fr, fi, pr, pi)
            fr, fi = jnp.concatenate([fr, xr], 0), jnp.concatenate([fi, xi], 0)
            xr, xi = _cmul(rr, ri, pr, pi)
            rr, ri = jnp.concatenate([xr, rr], 0), jnp.concatenate([xi, ri], 0)
            pr, pi = _cmul(pr, pi, pr, pi)
            n *= 2
        at_ref[gi] = jnp.concatenate([pr[0], pi[0]], axis=1)
        wr, wi = _cmul(rr, ri, bb_re, bb_im)
        w1 = jnp.concatenate([wr, wi], axis=2).reshape(tc * c, 2 * p)
        w1_ref[gi] = w1.astype(w1_ref.dtype)
        f1r, f1i = _cmul(fr, fi, a_re, a_im)
        c_re, c_im = cr_ref[gi][None], ci_ref[gi][None]
        mr, mi = _cmul(f1r, f1i, c_re, c_im)
        mt_ref[gi] = jnp.concatenate([mr, -mi], axis=2).reshape(tc * c, 2 * p).astype(mt_ref.dtype)
        er, ei = _cmul(fr, fi, c_re, c_im)
        e2 = jnp.concatenate([er, -ei], axis=2).reshape(tc * c, 2 * p)
        bcat = jnp.concatenate([bb_re[0], bb_im[0]], axis=1)
        kflat = lax.dot_general(e2, bcat, (((1,), (1,)), ((), ())),
                                precision=lax.Precision.HIGHEST,
                                preferred_element_type=F32)
        tz_ref[...] = jnp.zeros(tz_ref.shape, tz_ref.dtype)
        for s in range(tc):
            tz_ref[s * c:, s * c:(s + 1) * c] = kflat[:(tc - s) * c, :]
        tzt_ref[gi] = tz_ref[...].astype(tzt_ref.dtype)


def _ssm_prep(lam_re, lam_im, log_step, b_re, b_im, c_re, c_im, tc, gb):
    g, p = lam_re.shape
    c = SSM_C
    n = tc * c
    vecp = pl.BlockSpec((gb, 1, p), lambda i: (i, 0, 0))
    mat = pl.BlockSpec((gb, c, p), lambda i: (i, 0, 0))
    op = pl.BlockSpec((gb, n, 2 * p), lambda i: (i, 0, 0))
    return pl.pallas_call(
        functools.partial(_ssm_prep_kernel, tc=tc, gb=gb),
        out_shape=(jax.ShapeDtypeStruct((g, n, 2 * p), BF16),
                   jax.ShapeDtypeStruct((g, n, 2 * p), BF16),
                   jax.ShapeDtypeStruct((g, n, n), BF16),
                   jax.ShapeDtypeStruct((g, 1, 2 * p), F32)),
        grid=(g // gb,),
        in_specs=[vecp, vecp, pl.BlockSpec((gb, 1, 1), lambda i: (i, 0, 0)), mat, mat, mat, mat],
        out_specs=(op, op, pl.BlockSpec((gb, n, n), lambda i: (i, 0, 0)),
                   pl.BlockSpec((gb, 1, 2 * p), lambda i: (i, 0, 0))),
        scratch_shapes=[pltpu.VMEM((n, n), F32)],
        compiler_params=_cparams(("parallel",)),
        name="ssm_prep",
    )(lam_re.reshape(g, 1, p), lam_im.reshape(g, 1, p), log_step.reshape(g, 1, 1),
      jnp.swapaxes(b_re, 1, 2), jnp.swapaxes(b_im, 1, 2), c_re, c_im)


def _ssm_chunk_state_kernel(u_ref, w1_ref, s_ref, *, gb):
    for gi in range(gb):
        s_ref[gi] = jnp.dot(u_ref[gi], w1_ref[gi], preferred_element_type=F32)


def _ssm_scan_kernel(s_ref, at_ref, x_ref, st_ref, *, nchunks, bsz):
    a = jnp.broadcast_to(at_ref[...], st_ref.shape)
    lane = lax.broadcasted_iota(I32, a.shape, 2)
    a_sw = pltpu.roll(a, SSM_P, 2)
    a_re = jnp.where(lane < SSM_P, a, a_sw)
    a_im_s = jnp.where(lane < SSM_P, -a_sw, a)
    st_ref[...] = jnp.zeros(st_ref.shape, st_ref.dtype)

    def step(c, _):
        x = st_ref[...]
        x_ref[:, pl.ds(c * bsz, bsz), :] = x
        s = s_ref[:, pl.ds(c * bsz, bsz), :]
        st_ref[...] = x * a_re + pltpu.roll(x, SSM_P, 2) * a_im_s + s
        return 0

    lax.fori_loop(0, nchunks, step, 0)


def _ssm_out_kernel(u_ref, x_ref, tzt_ref, mt_ref, d_ref, y_ref, *, gb):
    for gi in range(gb):
        u = u_ref[gi]
        y = _dot_nt(u, tzt_ref[gi]) + _dot_nt(x_ref[gi].astype(BF16), mt_ref[gi])
        y = y + u.astype(F32) * d_ref[gi]
        y_ref[gi] = jax.nn.gelu(y).astype(y_ref.dtype)


def _s5(u3, ops, d_skip, tc, gb):
    w1, mt, tzt, at = ops
    b, l, width = u3.shape
    c = SSM_C
    g = width // c
    nch = l // tc
    rows = nch * b
    n = tc * c
    p2 = 2 * SSM_P
    ug = u3.reshape(b, nch, tc, g, c).transpose(3, 1, 0, 2, 4).reshape(g, rows, n)
    blk_u = pl.BlockSpec((gb, rows, n), lambda i: (i, 0, 0))
    blk_s = pl.BlockSpec((gb, rows, p2), lambda i: (i, 0, 0))
    blk_op = pl.BlockSpec((gb, n, p2), lambda i: (i, 0, 0))
    s = pl.pallas_call(
        functools.partial(_ssm_chunk_state_kernel, gb=gb),
        out_shape=jax.ShapeDtypeStruct((g, rows, p2), F32),
        grid=(g // gb,),
        in_specs=[blk_u, blk_op],
        out_specs=blk_s,
        compiler_params=_cparams(("parallel",)),
        name="ssm_chunk_state",
    )(ug, w1)
    xprev = pl.pallas_call(
        functools.partial(_ssm_scan_kernel, nchunks=nch, bsz=b),
        out_shape=jax.ShapeDtypeStruct((g, rows, p2), F32),
        grid=(g // gb,),
        in_specs=[blk_s, pl.BlockSpec((gb, 1, p2), lambda i: (i, 0, 0))],
        out_specs=blk_s,
        scratch_shapes=[pltpu.VMEM((gb, b, p2), F32)],
        compiler_params=_cparams(("parallel",)),
        name="ssm_chunk_scan",
    )(s, at)
    dt = jnp.tile(d_skip.reshape(g, 1, c), (1, 1, tc)).astype(F32)
    y = pl.pallas_call(
        functools.partial(_ssm_out_kernel, gb=gb),
        out_shape=jax.ShapeDtypeStruct((g, rows, n), BF16),
        grid=(g // gb,),
        in_specs=[blk_u, blk_s, pl.BlockSpec((gb, n, n), lambda i: (i, 0, 0)), blk_op,
                  pl.BlockSpec((gb, 1, n), lambda i: (i, 0, 0))],
        out_specs=blk_u,
        compiler_params=_cparams(("parallel",)),
        name="ssm_chunk_out",
    )(ug, xprev, tzt, mt, dt)
    return y.reshape(g, nch, b, tc, c).transpose(2, 1, 3, 0, 4).reshape(b, l, width)


def _glu_kernel(y_ref, yn_ref, w_ref, o_ref):
    gate = jnp.dot(y_ref[...], w_ref[...], preferred_element_type=F32)
    o_ref[...] = (yn_ref[...].astype(F32) * jax.nn.sigmoid(gate)).astype(o_ref.dtype)


def _glu(y2d, w):
    t, k = y2d.shape
    tm = min(t, 1024)
    tn = min(k, 512)
    return pl.pallas_call(
        _glu_kernel,
        out_shape=jax.ShapeDtypeStruct((t, k), BF16),
        grid=(t // tm, k // tn),
        in_specs=[pl.BlockSpec((tm, k), lambda m, n: (m, 0)),
                  pl.BlockSpec((tm, tn), lambda m, n: (m, n)),
                  pl.BlockSpec((k, tn), lambda m, n: (0, n))],
        out_specs=pl.BlockSpec((tm, tn), lambda m, n: (m, n)),
        compiler_params=_cparams(("parallel", "arbitrary")),
        name="ssm_glu",
    )(y2d, y2d, w)


def _outproj_kernel(h_ref, ya_ref, yb_ref, yc_ref, wa_ref, wb_ref, wc_ref, o_ref):
    acc = jnp.dot(ya_ref[...], wa_ref[...], preferred_element_type=F32)
    acc += jnp.dot(yb_ref[...], wb_ref[...], preferred_element_type=F32)
    acc += jnp.dot(yc_ref[...], wc_ref[...], preferred_element_type=F32)
    o_ref[...] = h_ref[...] + acc


def _outproj(h2d, ya, yb, yc, w):
    t, d = h2d.shape
    ka, kb, kc = ya.shape[1], yb.shape[1], yc.shape[1]
    tm = min(t, 512)
    tn = min(d, 512)
    return pl.pallas_call(
        _outproj_kernel,
        out_shape=jax.ShapeDtypeStruct((t, d), F32),
        grid=(t // tm, d // tn),
        in_specs=[pl.BlockSpec((tm, tn), lambda m, n: (m, n)),
                  pl.BlockSpec((tm, ka), lambda m, n: (m, 0)),
                  pl.BlockSpec((tm, kb), lambda m, n: (m, 0)),
                  pl.BlockSpec((tm, kc), lambda m, n: (m, 0)),
                  pl.BlockSpec((ka, tn), lambda m, n: (0, n)),
                  pl.BlockSpec((kb, tn), lambda m, n: (0, n)),
                  pl.BlockSpec((kc, tn), lambda m, n: (0, n))],
        out_specs=pl.BlockSpec((tm, tn), lambda m, n: (m, n)),
        compiler_params=_cparams(("parallel", "arbitrary")),
        name="out_proj",
    )(h2d, ya, yb, yc, w[:ka], w[ka:ka + kb], w[ka + kb:])


def _mlp_kernel(h_ref, g_ref, wu_ref, wd_ref, o_ref, xn_ref, acc_ref):
    f = pl.program_id(1)

    @pl.when(f == 0)
    def _():
        xn_ref[...] = _rms(h_ref[...], g_ref[...]).astype(BF16)
        acc_ref[...] = h_ref[...]

    hid = jnp.dot(xn_ref[...], wu_ref[...], preferred_element_type=F32)
    hid = jnp.square(jnp.maximum(hid, 0.0)).astype(BF16)
    acc_ref[...] += jnp.dot(hid, wd_ref[...], preferred_element_type=F32)

    @pl.when(f == pl.num_programs(1) - 1)
    def _():
        o_ref[...] = acc_ref[...]


def _mlp(h2d, g, w_up, w_down):
    t, d = h2d.shape
    ff = w_up.shape[1]
    tm = min(t, 512)
    tf = min(ff, 512)
    return pl.pallas_call(
        _mlp_kernel,
        out_shape=jax.ShapeDtypeStruct((t, d), F32),
        grid=(t // tm, ff // tf),
        in_specs=[pl.BlockSpec((tm, d), lambda m, f: (m, 0)),
                  pl.BlockSpec((1, d), lambda m, f: (0, 0)),
                  pl.BlockSpec((d, tf), lambda m, f: (0, f)),
                  pl.BlockSpec((tf, d), lambda m, f: (f, 0))],
        out_specs=pl.BlockSpec((tm, d), lambda m, f: (m, 0)),
        scratch_shapes=[pltpu.VMEM((tm, d), BF16), pltpu.VMEM((tm, d), F32)],
        compiler_params=_cparams(("parallel", "arbitrary")),
        name="mlp_relu2",
    )(h2d, g.reshape(1, d), w_up, w_down)


def _ple_kernel(h_ref, g_ref, p_ref, wg_ref, wp_ref, fg_ref, o_ref, *, tn, final):
    h = h_ref[...]
    xn = _rms(h, g_ref[...]).astype(BF16)
    pb = p_ref[...]
    d = h.shape[1]
    for j in range(d // tn):
        sl = slice(j * tn, (j + 1) * tn)
        gate = jax.nn.sigmoid(jnp.dot(xn, wg_ref[:, sl], preferred_element_type=F32))
        proj = jnp.dot(pb, wp_ref[:, sl], preferred_element_type=F32)
        o_ref[:, sl] = h[:, sl] + gate * proj
    if final:
        o_ref[...] = _rms(o_ref[...], fg_ref[...])


def _ple(h2d, g, p2d, wg, wp, final_g, final):
    t, d = h2d.shape
    pd = p2d.shape[1]
    tm = min(t, 256)
    return pl.pallas_call(
        functools.partial(_ple_kernel, tn=min(d, 512), final=final),
        out_shape=jax.ShapeDtypeStruct((t, d), F32),
        grid=(t // tm,),
        in_specs=[pl.BlockSpec((tm, d), lambda m: (m, 0)),
                  pl.BlockSpec((1, d), lambda m: (0, 0)),
                  pl.BlockSpec((tm, pd), lambda m: (m, 0)),
                  pl.BlockSpec((d, d), lambda m: (0, 0)),
                  pl.BlockSpec((pd, d), lambda m: (0, 0)),
                  pl.BlockSpec((1, d), lambda m: (0, 0))],
        out_specs=pl.BlockSpec((tm, d), lambda m: (m, 0)),
        compiler_params=_cparams(("parallel",)),
        name="ple_gate",
    )(h2d, g.reshape(1, d), p2d, wg, wp, final_g.reshape(1, d))


SSM_CHUNK = 32
SSM_GROUP_BLOCK = 8


def kernel(x, p, positions, norm_mix_g, w_in, w_out, diff_lq1, diff_lk1, diff_lq2, diff_lk2,
           diff_subln_g, ssm_lambda_re, ssm_lambda_im, ssm_log_step, ssm_B_re, ssm_B_im,
           ssm_C_re, ssm_C_im, ssm_D, ssm_w_glu, norm_mlp_g, w_up, w_down, norm_ple_g,
           w_ple_gate, w_ple_proj, final_g):
    b, l, d = x.shape
    depth = w_in.shape[0]
    t = b * l
    tc = min(SSM_CHUNK, l)
    tabs = _rope_tables(positions)
    h = x.reshape(t, d)
    for i in range(depth):
        lambda_init = 0.8 - 0.6 * math.exp(-0.3 * i)
        z = _inproj(h, norm_mix_g[i], _permute_w_in(w_in[i]), tabs)
        z3 = z.reshape(b, l, Z_WIDTH)
        y_a = _diff_attn(z3, diff_lq1[i], diff_lk1[i], diff_lq2[i], diff_lk2[i],
                         diff_subln_g[i], lambda_init)
        y_b = _dsa_attn(z3)
        ops = _ssm_prep(ssm_lambda_re[i], ssm_lambda_im[i], ssm_log_step[i], ssm_B_re[i],
                        ssm_B_im[i], ssm_C_re[i], ssm_C_im[i], tc, SSM_GROUP_BLOCK)
        y_s = _s5(z3[:, :, COL_CU * LANES:], ops, ssm_D[i], tc, SSM_GROUP_BLOCK)
        y_c = _glu(y_s.reshape(t, -1), ssm_w_glu[i].astype(BF16))
        h = _outproj(h, y_a.reshape(t, -1), y_b.reshape(t, -1), y_c, w_out[i].astype(BF16))
        h = _mlp(h, norm_mlp_g[i], w_up[i].astype(BF16), w_down[i].astype(BF16))
        h = _ple(h, norm_ple_g[i], p[i].reshape(t, -1).astype(BF16), w_ple_gate[i].astype(BF16),
                 w_ple_proj[i].astype(BF16), final_g, final=(i == depth - 1))
    return h.reshape(b, l, d)
```

```python
import functools
import math

import jax
import jax.numpy as jnp
from jax import lax
from jax.experimental import pallas as pl
from jax.experimental.pallas import tpu as pltpu

F32 = jnp.float32
BF16 = jnp.bfloat16
I32 = jnp.int32

LANES = 128
EPS = 1e-6
ROPE_THETA = 10000.0

DIFF_QK = 64
DIFF_V = 128
DIFF_HEADS = 4
DSA_DIM = 128
DSA_HEADS = 4
IDX_HEADS = 8
IDX_DIM = 64
TOPK_MAX = 256
SSM_C = 16
SSM_P = 64

COL_AQ, COL_AK, COL_IQ, COL_BQ, COL_AV = 0, 4, 8, 12, 16
COL_BK, COL_BV, COL_IKW, COL_CU = 20, 21, 22, 24
Z_WIDTH = 32 * LANES
IN_TN = 512

DSA_TQ = 128
NEG_BIG = -1e30
LOG2E = math.log2(math.e)
VMEM_LIMIT = 56 * 1024 * 1024


def _cparams(sem):
    return pltpu.CompilerParams(dimension_semantics=sem, vmem_limit_bytes=VMEM_LIMIT)


def _rope_tables_kernel(pos_ref, f64_ref, f128_ref, c64_ref, s64_ref, c128_ref, s128_ref):
    pos = pos_ref[...]
    a64 = pos * f64_ref[...]
    a128 = pos * f128_ref[...]
    lane = lax.broadcasted_iota(I32, a64.shape, 1)
    c64_ref[...] = jnp.cos(a64)
    sn = jnp.sin(a64)
    s64_ref[...] = jnp.where((lane & 32) == 0, -sn, sn)
    c128_ref[...] = jnp.cos(a128)
    sn = jnp.sin(a128)
    s128_ref[...] = jnp.where((lane & 64) == 0, -sn, sn)


def _rope_tables(positions):
    t = positions.size
    pos = positions.reshape(t, 1).astype(F32)
    fr64 = ROPE_THETA ** (-jnp.arange(0, 64, 2, dtype=F32) / 64)
    fr128 = ROPE_THETA ** (-jnp.arange(0, 128, 2, dtype=F32) / 128)
    f64 = jnp.tile(fr64, 4).reshape(1, LANES)
    f128 = jnp.tile(fr128, 2).reshape(1, LANES)
    tm = min(t, 1024)
    tab = jax.ShapeDtypeStruct((t, LANES), F32)
    row = pl.BlockSpec((tm, LANES), lambda i: (i, 0))
    return pl.pallas_call(
        _rope_tables_kernel,
        out_shape=(tab, tab, tab, tab),
        grid=(t // tm,),
        in_specs=[pl.BlockSpec((tm, 1), lambda i: (i, 0)),
                  pl.BlockSpec((1, LANES), lambda i: (0, 0)),
                  pl.BlockSpec((1, LANES), lambda i: (0, 0))],
        out_specs=(row, row, row, row),
        compiler_params=_cparams(("parallel",)),
        name="rope_tables",
    )(pos, f64, f128)


def _rms(x, g):
    ms = jnp.mean(x * x, axis=-1, keepdims=True)
    return x * lax.rsqrt(ms + EPS) * g


def _swap_halves(a, half):
    lane = lax.broadcasted_iota(I32, a.shape, 1)
    return jnp.where((lane & half) == 0,
                     pltpu.roll(a, LANES - half, 1), pltpu.roll(a, half, 1))


def _inproj_kernel(x_ref, g_ref, w_ref, c64_ref, s64_ref, c128_ref, s128_ref, z_ref, xn_ref):
    n = pl.program_id(1)

    @pl.when(n == 0)
    def _():
        xn_ref[...] = _rms(x_ref[...], g_ref[...]).astype(BF16)

    acc = jnp.dot(xn_ref[...], w_ref[...], preferred_element_type=F32)
    groups = acc.shape[1] // LANES

    def rope(a, half):
        if half == 32:
            return a * c64_ref[...] + _swap_halves(a, 32) * s64_ref[...]
        return a * c128_ref[...] + _swap_halves(a, 64) * s128_ref[...]

    def grp(j):
        return acc[:, j * LANES:(j + 1) * LANES]

    def put(j, v):
        z_ref[:, j * LANES:(j + 1) * LANES] = v.astype(z_ref.dtype)

    @pl.when(n < 3)
    def _():
        for j in range(groups):
            put(j, rope(grp(j), 32))

    @pl.when(n == 3)
    def _():
        for j in range(groups):
            put(j, rope(grp(j), 64))

    @pl.when(n == 5)
    def _():
        put(0, rope(grp(0), 64))
        put(1, grp(1))
        a = grp(2)
        lane = lax.broadcasted_iota(I32, a.shape, 1)
        put(2, jnp.where(lane < IDX_DIM, rope(a, 32), a))
        put(3, grp(3))

    @pl.when((n == 4) | (n > 5))
    def _():
        z_ref[...] = acc.astype(z_ref.dtype)


def _inproj(h2d, g, w, tabs):
    t, d = h2d.shape
    tm = min(t, 512)
    tn = IN_TN
    c64, s64, c128, s128 = tabs
    tab = pl.BlockSpec((tm, LANES), lambda m, n: (m, 0))
    return pl.pallas_call(
        _inproj_kernel,
        out_shape=jax.ShapeDtypeStruct((t, Z_WIDTH), BF16),
        grid=(t // tm, Z_WIDTH // tn),
        in_specs=[pl.BlockSpec((tm, d), lambda m, n: (m, 0)),
                  pl.BlockSpec((1, d), lambda m, n: (0, 0)),
                  pl.BlockSpec((d, tn), lambda m, n: (0, n)),
                  tab, tab, tab, tab],
        out_specs=pl.BlockSpec((tm, tn), lambda m, n: (m, n)),
        scratch_shapes=[pltpu.VMEM((tm, d), BF16)],
        compiler_params=_cparams(("parallel", "arbitrary")),
        name="norm_inproj_rope",
    )(h2d, g.reshape(1, d), w, c64, s64, c128, s128)


def _permute_w_in(w):
    d = w.shape[0]
    pad = jnp.zeros((d, 56 + LANES), w.dtype)
    return jnp.concatenate(
        [w[:, 0:512], w[:, 512:1024], w[:, 2304:2816], w[:, 1536:2048], w[:, 1024:1536],
         w[:, 2048:2176], w[:, 2176:2304], w[:, 2816:2888], pad, w[:, 2888:3912]],
        axis=1).astype(BF16)


def _flash_step(carry, s, vb):
    m, l, acc = carry
    m_new = jnp.maximum(m, jnp.max(s, axis=1, keepdims=True))
    alpha = jnp.exp2(m - m_new)
    p = jnp.exp2(s - m_new)
    l = alpha * l + jnp.sum(p, axis=1, keepdims=True)
    acc = alpha * acc + jnp.dot(p.astype(BF16), vb, preferred_element_type=F32)
    return m_new, l, acc


def _flash_init(rows, dv):
    return (jnp.full((rows, 1), NEG_BIG, F32), jnp.zeros((rows, 1), F32),
            jnp.zeros((rows, dv), F32))


def _dot_nt(a, b):
    return lax.dot_general(a, b, (((1,), (1,)), ((), ())), preferred_element_type=F32)


def _diff_attn_kernel(q_ref, k_ref, v_ref, lq1_ref, lk1_ref, lq2_ref, lk2_ref, g_ref, o_ref,
                      *, tq, tk, lambda_init):
    i = pl.program_id(2)
    q = q_ref[0].astype(F32) * (DIFF_QK ** -0.5)
    lane = lax.broadcasted_iota(I32, q.shape, 1)
    q2 = jnp.concatenate([jnp.where(lane < DIFF_QK, q, 0.0),
                          jnp.where(lane >= DIFF_QK, q, 0.0)], axis=0).astype(BF16)

    def block(j, carry, diag):
        kb = k_ref[0, pl.ds(j * tk, tk), :]
        vb = v_ref[0, pl.ds(j * tk, tk), :]
        s = _dot_nt(q2, kb) * LOG2E
        if diag:
            row = lax.broadcasted_iota(I32, s.shape, 0)
            row = i * tq + jnp.where(row >= tq, row - tq, row)
            col = j * tk + lax.broadcasted_iota(I32, s.shape, 1)
            s = jnp.where(col <= row, s, NEG_BIG)
        return _flash_step(carry, s, vb)

    init = _flash_init(2 * tq, DIFF_V)
    nfull = (i * tq) // tk
    carry = lax.fori_loop(0, nfull, lambda j, c: block(j, c, False), init)
    _, l, acc = block(nfull, carry, True)
    o = acc / l
    lam = (jnp.exp(jnp.sum(lq1_ref[...] * lk1_ref[...], keepdims=True))
           - jnp.exp(jnp.sum(lq2_ref[...] * lk2_ref[...], keepdims=True)) + lambda_init)
    out = o[:tq] - lam * o[tq:]
    out = _rms(out, g_ref[...]) * (1.0 - lambda_init)
    o_ref[0] = out.astype(o_ref.dtype)


def _diff_attn(z3, lq1, lk1, lq2, lk2, subln_g, lambda_init):
    b, l, _ = z3.shape
    tq = min(l, 256)
    tk = min(l, 512)
    vec = pl.BlockSpec((1, DIFF_QK), lambda bi, h, i: (0, 0))
    return pl.pallas_call(
        functools.partial(_diff_attn_kernel, tq=tq, tk=tk, lambda_init=lambda_init),
        out_shape=jax.ShapeDtypeStruct((b, l, DIFF_HEADS * DIFF_V), BF16),
        grid=(b, DIFF_HEADS, l // tq),
        in_specs=[pl.BlockSpec((1, tq, LANES), lambda bi, h, i: (bi, i, COL_AQ + h)),
                  pl.BlockSpec((1, l, LANES), lambda bi, h, i: (bi, 0, COL_AK + h)),
                  pl.BlockSpec((1, l, LANES), lambda bi, h, i: (bi, 0, COL_AV + h)),
                  vec, vec, vec, vec,
                  pl.BlockSpec((1, DIFF_V), lambda bi, h, i: (0, 0))],
        out_specs=pl.BlockSpec((1, tq, DIFF_V), lambda bi, h, i: (bi, i, h)),
        compiler_params=_cparams(("parallel", "parallel", "arbitrary")),
        name="diff_attention",
    )(z3, z3, z3, lq1.reshape(1, -1), lk1.reshape(1, -1), lq2.reshape(1, -1),
      lk2.reshape(1, -1), subln_g.reshape(1, -1))


def _sortable_key(s):
    bits = lax.bitcast_convert_type(s, I32)
    return bits ^ ((bits >> 31) & jnp.int32(0x7FFFFFFF))


def _dsa_kernel(bq_ref, iq_ref, iwq_ref, bk_ref, bv_ref, ik_ref, o_ref, keys_ref,
                *, tq, tk, topk, seq_bits):
    i = pl.program_id(1)
    nkb = ((i + 1) * tq + tk - 1) // tk
    groups = tk // LANES

    iq = iq_ref[0]
    iw = (iwq_ref[0][:, IDX_DIM:IDX_DIM + IDX_HEADS].astype(F32) * (IDX_HEADS ** -0.5)
          * (IDX_DIM ** -0.5))
    row_g = i * tq + lax.broadcasted_iota(I32, (tq, tk), 0)
    col_l = lax.broadcasted_iota(I32, (tq, tk), 1)

    def score_block(j, _):
        ikb = ik_ref[0, pl.ds(j * tk, tk), :][:, :IDX_DIM]
        sc = jnp.zeros((tq, tk), F32)
        for h in range(IDX_HEADS):
            r = _dot_nt(iq[:, h * IDX_DIM:(h + 1) * IDX_DIM], ikb)
            sc = sc + jnp.maximum(r, 0.0) * iw[:, h:h + 1]
        sc = jnp.where(j * tk + col_l <= row_g, sc + 0.0, -jnp.inf)
        keys_ref[j] = _sortable_key(sc)
        return 0

    lax.fori_loop(0, nkb, score_block, 0)

    lane_l = lax.broadcasted_iota(I32, (tq, LANES), 1)

    def count(pred_fn):
        def blk(j, cnt):
            for gi in range(groups):
                kk = keys_ref[j, :, gi * LANES:(gi + 1) * LANES]
                colg = lane_l + (j * tk + gi * LANES)
                cnt = cnt + jnp.where(pred_fn(kk, colg), 1, 0)
            return cnt
        cnt = lax.fori_loop(0, nkb, blk, jnp.zeros((tq, LANES), I32))
        return jnp.sum(cnt, axis=1, keepdims=True)

    int_min = jnp.int32(-2 ** 31)
    c0 = count(lambda kk, c: kk >= 0)
    thr = jnp.where(c0 >= topk, jnp.zeros((tq, 1), I32), jnp.full((tq, 1), int_min, I32))

    def bit_step(it, thr):
        cand = thr | (jnp.int32(1) << (30 - it))
        c = count(lambda kk, _: kk >= cand)
        return jnp.where(c >= topk, cand, thr)

    thr = lax.fori_loop(0, 31, bit_step, thr)

    c_gt = count(lambda kk, _: kk > thr)
    c_eq = count(lambda kk, _: kk == thr)
    need = topk - c_gt
    has_excess = jnp.max(jnp.where(c_eq > need, 1, 0)) > 0

    def tie_limit():
        def step(it, q):
            cand = q + (jnp.int32(1) << (seq_bits - 1 - it))
            c = count(lambda kk, colg: (kk == thr) & (colg < cand))
            return jnp.where(c < need, cand, q)
        return lax.fori_loop(0, seq_bits, step, jnp.zeros((tq, 1), I32))

    jlim = lax.cond(has_excess, tie_limit,
                    lambda: jnp.full((tq, 1), 2 ** seq_bits, I32))

    q = bq_ref[0]
    q4 = jnp.concatenate([q[:, h * DSA_DIM:(h + 1) * DSA_DIM] for h in range(DSA_HEADS)], axis=0)
    scale = DSA_DIM ** -0.5 * LOG2E

    def attn_block(j, carry):
        kb = bk_ref[0, pl.ds(j * tk, tk), :]
        vb = bv_ref[0, pl.ds(j * tk, tk), :]
        kk = keys_ref[j]
        colg = j * tk + col_l
        sel = ((kk > thr) | ((kk == thr) & (colg <= jlim))) & (colg <= row_g)
        bias = jnp.where(sel, 0.0, NEG_BIG)
        s = _dot_nt(q4, kb) * scale
        s = s + jnp.concatenate([bias] * DSA_HEADS, axis=0)
        return _flash_step(carry, s, vb)

    rows = DSA_HEADS * tq
    _, l, acc = lax.fori_loop(0, nkb, attn_block, _flash_init(rows, DSA_DIM))
    o = acc / l
    for h in range(DSA_HEADS):
        o_ref[0, :, h * DSA_DIM:(h + 1) * DSA_DIM] = o[h * tq:(h + 1) * tq].astype(o_ref.dtype)


def _dsa_attn(z3):
    b, l, _ = z3.shape
    tq = min(l, DSA_TQ)
    tk = min(l, 512)
    topk = min(TOPK_MAX, l // 4)
    seq_bits = max(1, (l - 1).bit_length())
    wide = 4
    return pl.pallas_call(
        functools.partial(_dsa_kernel, tq=tq, tk=tk, topk=topk, seq_bits=seq_bits),
        out_shape=jax.ShapeDtypeStruct((b, l, DSA_HEADS * DSA_DIM), BF16),
        grid=(b, l // tq),
        in_specs=[pl.BlockSpec((1, tq, 4 * LANES), lambda bi, i: (bi, i, COL_BQ // wide)),
                  pl.BlockSpec((1, tq, 4 * LANES), lambda bi, i: (bi, i, COL_IQ // wide)),
                  pl.BlockSpec((1, tq, LANES), lambda bi, i: (bi, i, COL_IKW)),
                  pl.BlockSpec((1, l, LANES), lambda bi, i: (bi, 0, COL_BK)),
                  pl.BlockSpec((1, l, LANES), lambda bi, i: (bi, 0, COL_BV)),
                  pl.BlockSpec((1, l, LANES), lambda bi, i: (bi, 0, COL_IKW))],
        out_specs=pl.BlockSpec((1, tq, DSA_HEADS * DSA_DIM), lambda bi, i: (bi, i, 0)),
        scratch_shapes=[pltpu.VMEM((l // tk, tq, tk), I32)],
        compiler_params=_cparams(("parallel", "arbitrary")),
        name="dsa_attention",
    )(z3, z3, z3, z3, z3, z3)


def _cmul(ar, ai, br, bi):
    return ar * br - ai * bi, ar * bi + ai * br


def _ssm_prep_kernel(lre_ref, lim_ref, lstep_ref, btr_ref, bti_ref, cr_ref, ci_ref,
                     w1_ref, mt_ref, tzt_ref, at_ref, tz_ref, *, tc, gb):
    c = SSM_C
    p = SSM_P
    for gi in range(gb):
        lr = lre_ref[gi].reshape(1, 1, p)
        li = lim_ref[gi].reshape(1, 1, p)
        step = jnp.exp(lstep_ref[gi]).reshape(1, 1, 1)
        mag = jnp.exp(lr * step)
        a_re, a_im = mag * jnp.cos(li * step), mag * jnp.sin(li * step)
        den = lr * lr + li * li
        nr, ni = a_re - 1.0, a_im
        f_re, f_im = (nr * lr + ni * li) / den, (ni * lr - nr * li) / den
        bt_r, bt_i = btr_ref[gi][None], bti_ref[gi][None]
        bb_re = f_re * bt_r - f_im * bt_i
        bb_im = f_re * bt_i + f_im * bt_r
        fr, fi = jnp.ones_like(a_re), jnp.zeros_like(a_im)
        rr, ri = fr, fi
        pr, pi = a_re, a_im
        n = 1
        while n < tc:
            xr, xi = _cmul(fr, fi, pr, pi)
            fr, fi = jnp.concatenate([fr, xr], 0), jnp.concatenate([fi, xi], 0)
            xr, xi = _cmul(rr, ri, pr, pi)
            rr, ri = jnp.concatenate([xr, rr], 0), jnp.concatenate([xi, ri], 0)
            pr, pi = _cmul(pr, pi, pr, pi)
            n *= 2
        at_ref[gi] = jnp.concatenate([pr[0], pi[0]], axis=1)
        wr, wi = _cmul(rr, ri, bb_re, bb_im)
        w1 = jnp.concatenate([wr, wi], axis=2).reshape(tc * c, 2 * p)
        w1_ref[gi] = w1.astype(w1_ref.dtype)
        f1r, f1i = _cmul(fr, fi, a_re, a_im)
        c_re, c_im = cr_ref[gi][None], ci_ref[gi][None]
        mr, mi = _cmul(f1r, f1i, c_re, c_im)
        mt_ref[gi] = jnp.concatenate([mr, -mi], axis=2).reshape(tc * c, 2 * p).astype(mt_ref.dtype)
        er, ei = _cmul(fr, fi, c_re, c_im)
        e2 = jnp.concatenate([er, -ei], axis=2).reshape(tc * c, 2 * p)
        bcat = jnp.concatenate([bb_re[0], bb_im[0]], axis=1)
        kflat = lax.dot_general(e2, bcat, (((1,), (1,)), ((), ())),
                                precision=lax.Precision.HIGHEST,
                                preferred_element_type=F32)
        tz_ref[...] = jnp.zeros(tz_ref.shape, tz_ref.dtype)
        for s in range(tc):
            tz_ref[s * c:, s * c:(s + 1) * c] = kflat[:(tc - s) * c, :]
        tzt_ref[gi] = tz_ref[...].astype(tzt_ref.dtype)


def _ssm_prep(lam_re, lam_im, log_step, b_re, b_im, c_re, c_im, tc, gb):
    g, p = lam_re.shape
    c = SSM_C
    n = tc * c
    vecp = pl.BlockSpec((gb, 1, p), lambda i: (i, 0, 0))
    mat = pl.BlockSpec((gb, c, p), lambda i: (i, 0, 0))
    op = pl.BlockSpec((gb, n, 2 * p), lambda i: (i, 0, 0))
    return pl.pallas_call(
        functools.partial(_ssm_prep_kernel, tc=tc, gb=gb),
        out_shape=(jax.ShapeDtypeStruct((g, n, 2 * p), BF16),
                   jax.ShapeDtypeStruct((g, n, 2 * p), BF16),
                   jax.ShapeDtypeStruct((g, n, n), BF16),
                   jax.ShapeDtypeStruct((g, 1, 2 * p), F32)),
        grid=(g // gb,),
        in_specs=[vecp, vecp, pl.BlockSpec((gb, 1, 1), lambda i: (i, 0, 0)), mat, mat, mat, mat],
        out_specs=(op, op, pl.BlockSpec((gb, n, n), lambda i: (i, 0, 0)),
                   pl.BlockSpec((gb, 1, 2 * p), lambda i: (i, 0, 0))),
        scratch_shapes=[pltpu.VMEM((n, n), F32)],
        compiler_params=_cparams(("parallel",)),
        name="ssm_prep",
    )(lam_re.reshape(g, 1, p), lam_im.reshape(g, 1, p), log_step.reshape(g, 1, 1),
      jnp.swapaxes(b_re, 1, 2), jnp.swapaxes(b_im, 1, 2), c_re, c_im)


def _ssm_fused_kernel(u_ref, w1_ref, mt_ref, tzt_ref, at_ref, d_ref, y_ref,
                      uf_ref, ufl_ref, s_ref, x_ref, yfl_ref, yt_ref, *, tc, gb):
    c = SSM_C
    l = uf_ref.shape[0]
    nch = l // tc
    uf_ref[...] = u_ref[0].astype(F32)
    for s in range(tc):
        tile = uf_ref[pl.ds(s, nch, stride=tc), :]
        for g in range(gb):
            ufl_ref[g, :, s * c:(s + 1) * c] = tile[:, g * c:(g + 1) * c]
    for g in range(gb):
        sg = jnp.dot(ufl_ref[g].astype(BF16), w1_ref[g], preferred_element_type=F32)
        s_ref[pl.ds(g, nch, stride=gb), :] = sg
    a = at_ref[...].reshape(gb, 2 * SSM_P)
    lane = lax.broadcasted_iota(I32, a.shape, 1)
    a_sw = pltpu.roll(a, SSM_P, 1)
    a_re = jnp.where(lane < SSM_P, a, a_sw)
    a_im_s = jnp.where(lane < SSM_P, -a_sw, a)

    def step(ci, x):
        r0 = pl.multiple_of(ci * gb, gb)
        x_ref[pl.ds(r0, gb), :] = x
        return x * a_re + pltpu.roll(x, SSM_P, 1) * a_im_s + s_ref[pl.ds(r0, gb), :]

    lax.fori_loop(0, nch, step, jnp.zeros((gb, 2 * SSM_P), F32))
    for g in range(gb):
        u = ufl_ref[g]
        xg = x_ref[pl.ds(g, nch, stride=gb), :]
        y = _dot_nt(u.astype(BF16), tzt_ref[g]) + _dot_nt(xg.astype(BF16), mt_ref[g])
        yfl_ref[g] = jax.nn.gelu(y + u * d_ref[g])
    for t in range(tc):
        tile = jnp.concatenate([yfl_ref[g, :, t * c:(t + 1) * c] for g in range(gb)], axis=1)
        yt_ref[pl.ds(t, nch, stride=tc), :] = tile
    y_ref[0] = yt_ref[...].astype(y_ref.dtype)


def _s5(z3, ops, d_skip, tc, gb):
    w1, mt, tzt, at = ops
    b, l, _ = z3.shape
    c = SSM_C
    g = w1.shape[0]
    nch = l // tc
    n = tc * c
    p2 = 2 * SSM_P
    wl = gb * c
    dt = jnp.tile(d_skip.reshape(g, 1, c), (1, 1, tc)).astype(F32)
    col0 = COL_CU * LANES // wl
    return pl.pallas_call(
        functools.partial(_ssm_fused_kernel, tc=tc, gb=gb),
        out_shape=jax.ShapeDtypeStruct((b, l, g * c), BF16),
        grid=(g // gb, b),
        in_specs=[pl.BlockSpec((1, l, wl), lambda gi, bi: (bi, 0, col0 + gi)),
                  pl.BlockSpec((gb, n, p2), lambda gi, bi: (gi, 0, 0)),
                  pl.BlockSpec((gb, n, p2), lambda gi, bi: (gi, 0, 0)),
                  pl.BlockSpec((gb, n, n), lambda gi, bi: (gi, 0, 0)),
                  pl.BlockSpec((gb, 1, p2), lambda gi, bi: (gi, 0, 0)),
                  pl.BlockSpec((gb, 1, n), lambda gi, bi: (gi, 0, 0))],
        out_specs=pl.BlockSpec((1, l, wl), lambda gi, bi: (bi, 0, gi)),
        scratch_shapes=[pltpu.VMEM((l, wl), F32),
                        pltpu.VMEM((gb, nch, n), F32),
                        pltpu.VMEM((nch * gb, p2), F32),
                        pltpu.VMEM((nch * gb, p2), F32),
                        pltpu.VMEM((gb, nch, n), F32),
                        pltpu.VMEM((l, wl), F32)],
        compiler_params=_cparams(("parallel", "arbitrary")),
        name="ssm_fused",
    )(z3, w1, mt, tzt, at, dt)


def _glu_kernel(y_ref, yn_ref, w_ref, o_ref):
    gate = jnp.dot(y_ref[...], w_ref[...], preferred_element_type=F32)
    o_ref[...] = (yn_ref[...].astype(F32) * jax.nn.sigmoid(gate)).astype(o_ref.dtype)


def _glu(y2d, w):
    t, k = y2d.shape
    tm = min(t, 1024)
    tn = min(k, 512)
    return pl.pallas_call(
        _glu_kernel,
        out_shape=jax.ShapeDtypeStruct((t, k), BF16),
        grid=(t // tm, k // tn),
        in_specs=[pl.BlockSpec((tm, k), lambda m, n: (m, 0)),
                  pl.BlockSpec((tm, tn), lambda m, n: (m, n)),
                  pl.BlockSpec((k, tn), lambda m, n: (0, n))],
        out_specs=pl.BlockSpec((tm, tn), lambda m, n: (m, n)),
        compiler_params=_cparams(("parallel", "arbitrary")),
        name="ssm_glu",
    )(y2d, y2d, w)


def _outproj_kernel(h_ref, ya_ref, yb_ref, yc_ref, wa_ref, wb_ref, wc_ref, o_ref):
    acc = jnp.dot(ya_ref[...], wa_ref[...], preferred_element_type=F32)
    acc += jnp.dot(yb_ref[...], wb_ref[...], preferred_element_type=F32)
    acc += jnp.dot(yc_ref[...], wc_ref[...], preferred_element_type=F32)
    o_ref[...] = h_ref[...] + acc


def _outproj(h2d, ya, yb, yc, w):
    t, d = h2d.shape
    ka, kb, kc = ya.shape[1], yb.shape[1], yc.shape[1]
    tm = min(t, 512)
    tn = min(d, 512)
    return pl.pallas_call(
        _outproj_kernel,
        out_shape=jax.ShapeDtypeStruct((t, d), F32),
        grid=(t // tm, d // tn),
        in_specs=[pl.BlockSpec((tm, tn), lambda m, n: (m, n)),
                  pl.BlockSpec((tm, ka), lambda m, n: (m, 0)),
                  pl.BlockSpec((tm, kb), lambda m, n: (m, 0)),
                  pl.BlockSpec((tm, kc), lambda m, n: (m, 0)),
                  pl.BlockSpec((ka, tn), lambda m, n: (0, n)),
                  pl.BlockSpec((kb, tn), lambda m, n: (0, n)),
                  pl.BlockSpec((kc, tn), lambda m, n: (0, n))],
        out_specs=pl.BlockSpec((tm, tn), lambda m, n: (m, n)),
        compiler_params=_cparams(("parallel", "arbitrary")),
        name="out_proj",
    )(h2d, ya, yb, yc, w[:ka], w[ka:ka + kb], w[ka + kb:])


def _mlp_kernel(h_ref, g_ref, wu_ref, wd_ref, o_ref, xn_ref, acc_ref):
    f = pl.program_id(1)

    @pl.when(f == 0)
    def _():
        xn_ref[...] = _rms(h_ref[...], g_ref[...]).astype(BF16)
        acc_ref[...] = h_ref[...]

    hid = jnp.dot(xn_ref[...], wu_ref[...], preferred_element_type=F32)
    hid = jnp.square(jnp.maximum(hid, 0.0)).astype(BF16)
    acc_ref[...] += jnp.dot(hid, wd_ref[...], preferred_element_type=F32)

    @pl.when(f == pl.num_programs(1) - 1)
    def _():
        o_ref[...] = acc_ref[...]


def _mlp(h2d, g, w_up, w_down):
    t, d = h2d.shape
    ff = w_up.shape[1]
    tm = min(t, 512)
    tf = min(ff, 512)
    return pl.pallas_call(
        _mlp_kernel,
        out_shape=jax.ShapeDtypeStruct((t, d), F32),
        grid=(t // tm, ff // tf),
        in_specs=[pl.BlockSpec((tm, d), lambda m, f: (m, 0)),
                  pl.BlockSpec((1, d), lambda m, f: (0, 0)),
                  pl.BlockSpec((d, tf), lambda m, f: (0, f)),
                  pl.BlockSpec((tf, d), lambda m, f: (f, 0))],
        out_specs=pl.BlockSpec((tm, d), lambda m, f: (m, 0)),
        scratch_shapes=[pltpu.VMEM((tm, d), BF16), pltpu.VMEM((tm, d), F32)],
        compiler_params=_cparams(("parallel", "arbitrary")),
        name="mlp_relu2",
    )(h2d, g.reshape(1, d), w_up, w_down)


def _ple_kernel(h_ref, g_ref, p_ref, wg_ref, wp_ref, fg_ref, o_ref, *, tn, final):
    h = h_ref[...]
    xn = _rms(h, g_ref[...]).astype(BF16)
    pb = p_ref[...]
    d = h.shape[1]
    for j in range(d // tn):
        sl = slice(j * tn, (j + 1) * tn)
        gate = jax.nn.sigmoid(jnp.dot(xn, wg_ref[:, sl], preferred_element_type=F32))
        proj = jnp.dot(pb, wp_ref[:, sl], preferred_element_type=F32)
        o_ref[:, sl] = h[:, sl] + gate * proj
    if final:
        o_ref[...] = _rms(o_ref[...], fg_ref[...])


def _ple(h2d, g, p2d, wg, wp, final_g, final):
    t, d = h2d.shape
    pd = p2d.shape[1]
    tm = min(t, 256)
    return pl.pallas_call(
        functools.partial(_ple_kernel, tn=min(d, 512), final=final),
        out_shape=jax.ShapeDtypeStruct((t, d), F32),
        grid=(t // tm,),
        in_specs=[pl.BlockSpec((tm, d), lambda m: (m, 0)),
                  pl.BlockSpec((1, d), lambda m: (0, 0)),
                  pl.BlockSpec((tm, pd), lambda m: (m, 0)),
                  pl.BlockSpec((d, d), lambda m: (0, 0)),
                  pl.BlockSpec((pd, d), lambda m: (0, 0)),
                  pl.BlockSpec((1, d), lambda m: (0, 0))],
        out_specs=pl.BlockSpec((tm, d), lambda m: (m, 0)),
        compiler_params=_cparams(("parallel",)),
        name="ple_gate",
    )(h2d, g.reshape(1, d), p2d, wg, wp, final_g.reshape(1, d))


SSM_CHUNK = 32
SSM_GROUP_BLOCK = 8


def kernel(x, p, positions, norm_mix_g, w_in, w_out, diff_lq1, diff_lk1, diff_lq2, diff_lk2,
           diff_subln_g, ssm_lambda_re, ssm_lambda_im, ssm_log_step, ssm_B_re, ssm_B_im,
           ssm_C_re, ssm_C_im, ssm_D, ssm_w_glu, norm_mlp_g, w_up, w_down, norm_ple_g,
           w_ple_gate, w_ple_proj, final_g):
    b, l, d = x.shape
    depth = w_in.shape[0]
    t = b * l
    tc = min(SSM_CHUNK, l)
    tabs = _rope_tables(positions)
    h = x.reshape(t, d)
    for i in range(depth):
        lambda_init = 0.8 - 0.6 * math.exp(-0.3 * i)
        z = _inproj(h, norm_mix_g[i], _permute_w_in(w_in[i]), tabs)
        z3 = z.reshape(b, l, Z_WIDTH)
        y_a = _diff_attn(z3, diff_lq1[i], diff_lk1[i], diff_lq2[i], diff_lk2[i],
                         diff_subln_g[i], lambda_init)
        y_b = _dsa_attn(z3)
        ops = _ssm_prep(ssm_lambda_re[i], ssm_lambda_im[i], ssm_log_step[i], ssm_B_re[i],
                        ssm_B_im[i], ssm_C_re[i], ssm_C_im[i], tc, SSM_GROUP_BLOCK)
        y_s = _s5(z3, ops, ssm_D[i], tc, SSM_GROUP_BLOCK)
        y_c = _glu(y_s.reshape(t, -1), ssm_w_glu[i].astype(BF16))
        h = _outproj(h, y_a.reshape(t, -1), y_b.reshape(t, -1), y_c, w_out[i].astype(BF16))
        h = _mlp(h, norm_mlp_g[i], w_up[i].astype(BF16), w_down[i].astype(BF16))
        h = _ple(h, norm_ple_g[i], p[i].reshape(t, -1).astype(BF16), w_ple_gate[i].astype(BF16),
                 w_ple_proj[i].astype(BF16), final_g, final=(i == depth - 1))
    return h.reshape(b, l, d)
```

```python
import functools
import math

import jax
import jax.numpy as jnp
from jax import lax
from jax.experimental import pallas as pl
from jax.experimental.pallas import tpu as pltpu

F32 = jnp.float32
BF16 = jnp.bfloat16
I32 = jnp.int32

LANES = 128
EPS = 1e-6
ROPE_THETA = 10000.0

DIFF_QK = 64
DIFF_V = 128
DIFF_HEADS = 4
DSA_DIM = 128
DSA_HEADS = 4
IDX_HEADS = 8
IDX_DIM = 64
TOPK_MAX = 256
SSM_C = 16
SSM_P = 64

COL_AQ, COL_AK, COL_IQ, COL_BQ, COL_AV = 0, 4, 8, 12, 16
COL_BK, COL_BV, COL_IKW, COL_CU = 20, 21, 22, 24
Z_WIDTH = 32 * LANES
IN_TN = 512

DSA_TQ = 256
DSA_TK = 256
NEG_BIG = -1e30
LOG2E = math.log2(math.e)
VMEM_LIMIT = 56 * 1024 * 1024


def _cparams(sem):
    return pltpu.CompilerParams(dimension_semantics=sem, vmem_limit_bytes=VMEM_LIMIT)


def _rope_tables_kernel(pos_ref, f64_ref, f128_ref, c64_ref, s64_ref, c128_ref, s128_ref):
    pos = pos_ref[...]
    a64 = pos * f64_ref[...]
    a128 = pos * f128_ref[...]
    lane = lax.broadcasted_iota(I32, a64.shape, 1)
    c64_ref[...] = jnp.cos(a64)
    sn = jnp.sin(a64)
    s64_ref[...] = jnp.where((lane & 32) == 0, -sn, sn)
    c128_ref[...] = jnp.cos(a128)
    sn = jnp.sin(a128)
    s128_ref[...] = jnp.where((lane & 64) == 0, -sn, sn)


def _rope_tables(positions):
    t = positions.size
    pos = positions.reshape(t, 1).astype(F32)
    fr64 = ROPE_THETA ** (-jnp.arange(0, 64, 2, dtype=F32) / 64)
    fr128 = ROPE_THETA ** (-jnp.arange(0, 128, 2, dtype=F32) / 128)
    f64 = jnp.tile(fr64, 4).reshape(1, LANES)
    f128 = jnp.tile(fr128, 2).reshape(1, LANES)
    tm = min(t, 1024)
    tab = jax.ShapeDtypeStruct((t, LANES), F32)
    row = pl.BlockSpec((tm, LANES), lambda i: (i, 0))
    return pl.pallas_call(
        _rope_tables_kernel,
        out_shape=(tab, tab, tab, tab),
        grid=(t // tm,),
        in_specs=[pl.BlockSpec((tm, 1), lambda i: (i, 0)),
                  pl.BlockSpec((1, LANES), lambda i: (0, 0)),
                  pl.BlockSpec((1, LANES), lambda i: (0, 0))],
        out_specs=(row, row, row, row),
        compiler_params=_cparams(("parallel",)),
        name="rope_tables",
    )(pos, f64, f128)


def _rms(x, g):
    ms = jnp.mean(x * x, axis=-1, keepdims=True)
    return x * lax.rsqrt(ms + EPS) * g


def _swap_halves(a, half):
    lane = lax.broadcasted_iota(I32, a.shape, 1)
    return jnp.where((lane & half) == 0,
                     pltpu.roll(a, LANES - half, 1), pltpu.roll(a, half, 1))


def _inproj_kernel(x_ref, g_ref, w_ref, c64_ref, s64_ref, c128_ref, s128_ref, z_ref, xn_ref):
    n = pl.program_id(1)

    @pl.when(n == 0)
    def _():
        xn_ref[...] = _rms(x_ref[...], g_ref[...]).astype(BF16)

    acc = jnp.dot(xn_ref[...], w_ref[...], preferred_element_type=F32)
    groups = acc.shape[1] // LANES

    def rope(a, half):
        if half == 32:
            return a * c64_ref[...] + _swap_halves(a, 32) * s64_ref[...]
        return a * c128_ref[...] + _swap_halves(a, 64) * s128_ref[...]

    def grp(j):
        return acc[:, j * LANES:(j + 1) * LANES]

    def put(j, v):
        z_ref[:, j * LANES:(j + 1) * LANES] = v.astype(z_ref.dtype)

    @pl.when(n < 3)
    def _():
        for j in range(groups):
            put(j, rope(grp(j), 32))

    @pl.when(n == 3)
    def _():
        for j in range(groups):
            put(j, rope(grp(j), 64))

    @pl.when(n == 5)
    def _():
        put(0, rope(grp(0), 64))
        put(1, grp(1))
        a = grp(2)
        lane = lax.broadcasted_iota(I32, a.shape, 1)
        put(2, jnp.where(lane < IDX_DIM, rope(a, 32), a))
        put(3, grp(3))

    @pl.when((n == 4) | (n > 5))
    def _():
        z_ref[...] = acc.astype(z_ref.dtype)


def _inproj(h2d, g, w, tabs):
    t, d = h2d.shape
    tm = min(t, 512)
    tn = IN_TN
    c64, s64, c128, s128 = tabs
    tab = pl.BlockSpec((tm, LANES), lambda m, n: (m, 0))
    return pl.pallas_call(
        _inproj_kernel,
        out_shape=jax.ShapeDtypeStruct((t, Z_WIDTH), BF16),
        grid=(t // tm, Z_WIDTH // tn),
        in_specs=[pl.BlockSpec((tm, d), lambda m, n: (m, 0)),
                  pl.BlockSpec((1, d), lambda m, n: (0, 0)),
                  pl.BlockSpec((d, tn), lambda m, n: (0, n)),
                  tab, tab, tab, tab],
        out_specs=pl.BlockSpec((tm, tn), lambda m, n: (m, n)),
        scratch_shapes=[pltpu.VMEM((tm, d), BF16)],
        compiler_params=_cparams(("parallel", "arbitrary")),
        name="norm_inproj_rope",
    )(h2d, g.reshape(1, d), w, c64, s64, c128, s128)


def _permute_w_in(w):
    d = w.shape[0]
    pad = jnp.zeros((d, 56 + LANES), w.dtype)
    return jnp.concatenate(
        [w[:, 0:512], w[:, 512:1024], w[:, 2304:2816], w[:, 1536:2048], w[:, 1024:1536],
         w[:, 2048:2176], w[:, 2176:2304], w[:, 2816:2888], pad, w[:, 2888:3912]],
        axis=1).astype(BF16)


def _flash_step(carry, s, vb):
    m, l, acc = carry
    m_new = jnp.maximum(m, jnp.max(s, axis=1, keepdims=True))
    alpha = jnp.exp2(m - m_new)
    p = jnp.exp2(s - m_new)
    l = alpha * l + jnp.sum(p, axis=1, keepdims=True)
    acc = alpha * acc + jnp.dot(p.astype(BF16), vb, preferred_element_type=F32)
    return m_new, l, acc


def _flash_init(rows, dv):
    return (jnp.full((rows, 1), NEG_BIG, F32), jnp.zeros((rows, 1), F32),
            jnp.zeros((rows, dv), F32))


def _dot_nt(a, b):
    return lax.dot_general(a, b, (((1,), (1,)), ((), ())), preferred_element_type=F32)


def _diff_attn_kernel(q_ref, k_ref, v_ref, lq1_ref, lk1_ref, lq2_ref, lk2_ref, g_ref, o_ref,
                      *, tq, tk, lambda_init):
    i = pl.program_id(2)
    q = q_ref[0].astype(F32) * (DIFF_QK ** -0.5)
    lane = lax.broadcasted_iota(I32, q.shape, 1)
    q2 = jnp.concatenate([jnp.where(lane < DIFF_QK, q, 0.0),
                          jnp.where(lane >= DIFF_QK, q, 0.0)], axis=0).astype(BF16)

    def block(j, carry, diag):
        kb = k_ref[0, pl.ds(j * tk, tk), :]
        vb = v_ref[0, pl.ds(j * tk, tk), :]
        s = _dot_nt(q2, kb) * LOG2E
        if diag:
            row = lax.broadcasted_iota(I32, s.shape, 0)
            row = i * tq + jnp.where(row >= tq, row - tq, row)
            col = j * tk + lax.broadcasted_iota(I32, s.shape, 1)
            s = jnp.where(col <= row, s, NEG_BIG)
        return _flash_step(carry, s, vb)

    init = _flash_init(2 * tq, DIFF_V)
    nfull = (i * tq) // tk
    carry = lax.fori_loop(0, nfull, lambda j, c: block(j, c, False), init)
    _, l, acc = block(nfull, carry, True)
    o = acc / l
    lam = (jnp.exp(jnp.sum(lq1_ref[...] * lk1_ref[...], keepdims=True))
           - jnp.exp(jnp.sum(lq2_ref[...] * lk2_ref[...], keepdims=True)) + lambda_init)
    out = o[:tq] - lam * o[tq:]
    out = _rms(out, g_ref[...]) * (1.0 - lambda_init)
    o_ref[0] = out.astype(o_ref.dtype)


def _diff_attn(z3, lq1, lk1, lq2, lk2, subln_g, lambda_init):
    b, l, _ = z3.shape
    tq = min(l, 256)
    tk = min(l, 512)
    vec = pl.BlockSpec((1, DIFF_QK), lambda bi, h, i: (0, 0))
    return pl.pallas_call(
        functools.partial(_diff_attn_kernel, tq=tq, tk=tk, lambda_init=lambda_init),
        out_shape=jax.ShapeDtypeStruct((b, l, DIFF_HEADS * DIFF_V), BF16),
        grid=(b, DIFF_HEADS, l // tq),
        in_specs=[pl.BlockSpec((1, tq, LANES), lambda bi, h, i: (bi, i, COL_AQ + h)),
                  pl.BlockSpec((1, l, LANES), lambda bi, h, i: (bi, 0, COL_AK + h)),
                  pl.BlockSpec((1, l, LANES), lambda bi, h, i: (bi, 0, COL_AV + h)),
                  vec, vec, vec, vec,
                  pl.BlockSpec((1, DIFF_V), lambda bi, h, i: (0, 0))],
        out_specs=pl.BlockSpec((1, tq, DIFF_V), lambda bi, h, i: (bi, i, h)),
        compiler_params=_cparams(("parallel", "parallel", "arbitrary")),
        name="diff_attention",
    )(z3, z3, z3, lq1.reshape(1, -1), lk1.reshape(1, -1), lq2.reshape(1, -1),
      lk2.reshape(1, -1), subln_g.reshape(1, -1))


def _sortable_key(s):
    bits = lax.bitcast_convert_type(s, I32)
    return bits ^ ((bits >> 31) & jnp.int32(0x7FFFFFFF))


def _dsa_kernel(bq_ref, iq_ref, iwq_ref, bk_ref, bvt_ref, ik_ref, o_ref, hi_ref, lo_ref,
                *, tq, tk, topk, seq_bits):
    i = pl.program_id(1)
    nkb = ((i + 1) * tq + tk - 1) // tk
    low16 = -(2 ** 15)
    i16 = jnp.int16

    def as16(v):
        return v.astype(i16)

    krow = lax.broadcasted_iota(I32, (tk, tq), 0)
    qcol = i * tq + lax.broadcasted_iota(I32, (tk, tq), 1)

    iq = iq_ref[0]
    iq_h = [iq[:, h * IDX_DIM:(h + 1) * IDX_DIM] for h in range(IDX_HEADS)]
    iw_t = jnp.transpose(iwq_ref[0].astype(F32))[IDX_DIM:IDX_DIM + IDX_HEADS, :]
    iw_t = iw_t * (IDX_HEADS ** -0.5) * (IDX_DIM ** -0.5)

    def score_block(j, _):
        ikb = ik_ref[0, pl.ds(j * tk, tk), :][:, :IDX_DIM]
        sc = jnp.zeros((tk, tq), F32)
        for h in range(IDX_HEADS):
            sc = sc + jnp.maximum(_dot_nt(ikb, iq_h[h]), 0.0) * iw_t[h:h + 1, :]
        sc = jnp.where(j * tk + krow <= qcol, sc + 0.0, -jnp.inf)
        key = _sortable_key(sc)
        hi_ref[j] = as16(key >> 16)
        lo_ref[j] = as16((key & 0xFFFF) + low16)
        return 0

    lax.fori_loop(0, nkb, score_block, 0)

    one16, zero16 = jnp.ones((), i16), jnp.zeros((), i16)

    def count(pred_fn):
        def blk(j, cnt):
            m = jnp.where(pred_fn(j), one16, zero16)
            parts = [m[r * 16:(r + 1) * 16, :] for r in range(tk // 16)]
            while len(parts) > 1:
                parts = [a + b for a, b in zip(parts[::2], parts[1::2])]
            return cnt + parts[0]
        cnt = lax.fori_loop(0, nkb, blk, jnp.zeros((16, tq), i16))
        return jnp.sum(cnt.astype(I32), axis=0, keepdims=True)

    def radix_select(ref, target):
        c0 = count(lambda j: ref[j] >= zero16)
        thr = jnp.where(c0 >= target, 0, low16).astype(I32)

        def bit_step(it, thr):
            cand = thr | (jnp.int32(1) << (14 - it))
            c16 = as16(cand)
            return jnp.where(count(lambda j: ref[j] >= c16) >= target, cand, thr)

        return lax.fori_loop(0, 15, bit_step, thr)

    t_hi = as16(radix_select(hi_ref, topk))
    need_lo = topk - count(lambda j: hi_ref[j] > t_hi)

    def mask_lo(j, _):
        lo_ref[j] = jnp.where(hi_ref[j] == t_hi, lo_ref[j], jnp.int16(low16))
        return 0

    lax.fori_loop(0, nkb, mask_lo, 0)
    t_lo = as16(radix_select(lo_ref, need_lo))

    def in_tie(j):
        return (hi_ref[j] == t_hi) & (lo_ref[j] == t_lo)

    need = need_lo - count(lambda j: lo_ref[j] > t_lo)
    has_excess = jnp.max(jnp.where(count(in_tie) > need, 1, 0)) > 0
    krow16 = as16(krow)

    def tie_limit():
        def step(it, q):
            cand = q + (jnp.int32(1) << (seq_bits - 1 - it))
            c16 = as16(cand)
            c = count(lambda j: in_tie(j) & (krow16 + as16(j * tk) < c16))
            return jnp.where(c < need, cand, q)
        return lax.fori_loop(0, seq_bits, step, jnp.zeros((1, tq), I32))

    jlim16 = as16(lax.cond(has_excess, tie_limit, lambda: jnp.full((1, tq), 2 ** seq_bits, I32)))

    q = bq_ref[0]
    q_h = [q[:, h * DSA_DIM:(h + 1) * DSA_DIM] for h in range(DSA_HEADS)]
    scale = DSA_DIM ** -0.5 * LOG2E
    zero_b, neg_b = jnp.zeros((), BF16), jnp.full((), NEG_BIG, BF16)
    qcol16 = as16(qcol)

    def attn_block(j, carry):
        kb = bk_ref[0, pl.ds(j * tk, tk), :]
        vtb = bvt_ref[0, :, pl.ds(j * tk, tk)]
        hi, lo = hi_ref[j], lo_ref[j]
        kidx = krow16 + as16(j * tk)
        sel = (hi > t_hi) | ((hi == t_hi) & ((lo > t_lo) | ((lo == t_lo) & (kidx <= jlim16))))
        sel = sel & (kidx <= qcol16)
        bias = jnp.where(sel, zero_b, neg_b).astype(F32)
        out = []
        for h in range(DSA_HEADS):
            m, l, acc = carry[h]
            s = _dot_nt(kb, q_h[h]) * scale + bias
            m_new = jnp.maximum(m, jnp.max(s, axis=0, keepdims=True))
            alpha = jnp.exp2(m - m_new)
            p = jnp.exp2(s - m_new)
            l = alpha * l + jnp.sum(p, axis=0, keepdims=True)
            acc = alpha * acc + jnp.dot(vtb, p.astype(BF16), preferred_element_type=F32)
            out.append((m_new, l, acc))
        return tuple(out)

    init = tuple((jnp.full((1, tq), NEG_BIG, F32), jnp.zeros((1, tq), F32),
                  jnp.zeros((DSA_DIM, tq), F32)) for _ in range(DSA_HEADS))
    res = lax.fori_loop(0, nkb, attn_block, init)
    for h in range(DSA_HEADS):
        _, l, acc = res[h]
        o_ref[0, :, h * DSA_DIM:(h + 1) * DSA_DIM] = jnp.transpose(acc / l).astype(o_ref.dtype)


def _dsa_attn(z3):
    b, l, _ = z3.shape
    tq = min(l, DSA_TQ)
    tk = min(l, DSA_TK)
    topk = min(TOPK_MAX, l // 4)
    seq_bits = max(1, (l - 1).bit_length())
    wide = 4
    v_t = jnp.swapaxes(z3[:, :, COL_BV * LANES:(COL_BV + 1) * LANES], 1, 2)
    return pl.pallas_call(
        functools.partial(_dsa_kernel, tq=tq, tk=tk, topk=topk, seq_bits=seq_bits),
        out_shape=jax.ShapeDtypeStruct((b, l, DSA_HEADS * DSA_DIM), BF16),
        grid=(b, l // tq),
        in_specs=[pl.BlockSpec((1, tq, 4 * LANES), lambda bi, i: (bi, i, COL_BQ // wide)),
                  pl.BlockSpec((1, tq, 4 * LANES), lambda bi, i: (bi, i, COL_IQ // wide)),
                  pl.BlockSpec((1, tq, LANES), lambda bi, i: (bi, i, COL_IKW)),
                  pl.BlockSpec((1, l, LANES), lambda bi, i: (bi, 0, COL_BK)),
                  pl.BlockSpec((1, DSA_DIM, l), lambda bi, i: (bi, 0, 0)),
                  pl.BlockSpec((1, l, LANES), lambda bi, i: (bi, 0, COL_IKW))],
        out_specs=pl.BlockSpec((1, tq, DSA_HEADS * DSA_DIM), lambda bi, i: (bi, i, 0)),
        scratch_shapes=[pltpu.VMEM((l // tk, tk, tq), jnp.int16),
                        pltpu.VMEM((l // tk, tk, tq), jnp.int16)],
        compiler_params=_cparams(("parallel", "arbitrary")),
        name="dsa_attention",
    )(z3, z3, z3, z3, v_t, z3)


def _cmul(ar, ai, br, bi):
    return ar * br - ai * bi, ar * bi + ai * br


def _ssm_prep_kernel(lre_ref, lim_ref, lstep_ref, btr_ref, bti_ref, cr_ref, ci_ref,
                     w1_ref, mt_ref, tzt_ref, at_ref, tz_ref, *, tc, gb):
    c = SSM_C
    p = SSM_P
    for gi in range(gb):
        lr = lre_ref[gi].reshape(1, 1, p)
        li = lim_ref[gi].reshape(1, 1, p)
        step = jnp.exp(lstep_ref[gi]).reshape(1, 1, 1)
        mag = jnp.exp(lr * step)
        a_re, a_im = mag * jnp.cos(li * step), mag * jnp.sin(li * step)
        den = lr * lr + li * li
        nr, ni = a_re - 1.0, a_im
        f_re, f_im = (nr * lr + ni * li) / den, (ni * lr - nr * li) / den
        bt_r, bt_i = btr_ref[gi][None], bti_ref[gi][None]
        bb_re = f_re * bt_r - f_im * bt_i
        bb_im = f_re * bt_i + f_im * bt_r
        fr, fi = jnp.ones_like(a_re), jnp.zeros_like(a_im)
        rr, ri = fr, fi
        pr, pi = a_re, a_im
        n = 1
        while n < tc:
            xr, xi = _cmul(fr, fi, pr, pi)
            fr, fi = jnp.concatenate([fr, xr], 0), jnp.concatenate([fi, xi], 0)
            xr, xi = _cmul(rr, ri, pr, pi)
            rr, ri = jnp.concatenate([xr, rr], 0), jnp.concatenate([xi, ri], 0)
            pr, pi = _cmul(pr, pi, pr, pi)
            n *= 2
        at_ref[gi] = jnp.concatenate([pr[0], pi[0]], axis=1)
        wr, wi = _cmul(rr, ri, bb_re, bb_im)
        w1 = jnp.concatenate([wr, wi], axis=2).reshape(tc * c, 2 * p)
        w1_ref[gi] = w1.astype(w1_ref.dtype)
        f1r, f1i = _cmul(fr, fi, a_re, a_im)
        c_re, c_im = cr_ref[gi][None], ci_ref[gi][None]
        mr, mi = _cmul(f1r, f1i, c_re, c_im)
        mt_ref[gi] = jnp.concatenate([mr, -mi], axis=2).reshape(tc * c, 2 * p).astype(mt_ref.dtype)
        er, ei = _cmul(fr, fi, c_re, c_im)
        e2 = jnp.concatenate([er, -ei], axis=2).reshape(tc * c, 2 * p)
        bcat = jnp.concatenate([bb_re[0], bb_im[0]], axis=1)
        kflat = lax.dot_general(e2, bcat, (((1,), (1,)), ((), ())),
                                precision=lax.Precision.HIGHEST,
                                preferred_element_type=F32)
        tz_ref[...] = jnp.zeros(tz_ref.shape, tz_ref.dtype)
        for s in range(tc):
            tz_ref[s * c:, s * c:(s + 1) * c] = kflat[:(tc - s) * c, :]
        tzt_ref[gi] = tz_ref[...].astype(tzt_ref.dtype)


def _ssm_prep(lam_re, lam_im, log_step, b_re, b_im, c_re, c_im, tc, gb):
    g, p = lam_re.shape
    c = SSM_C
    n = tc * c
    vecp = pl.BlockSpec((gb, 1, p), lambda i: (i, 0, 0))
    mat = pl.BlockSpec((gb, c, p), lambda i: (i, 0, 0))
    op = pl.BlockSpec((gb, n, 2 * p), lambda i: (i, 0, 0))
    return pl.pallas_call(
        functools.partial(_ssm_prep_kernel, tc=tc, gb=gb),
        out_shape=(jax.ShapeDtypeStruct((g, n, 2 * p), BF16),
                   jax.ShapeDtypeStruct((g, n, 2 * p), BF16),
                   jax.ShapeDtypeStruct((g, n, n), BF16),
                   jax.ShapeDtypeStruct((g, 1, 2 * p), F32)),
        grid=(g // gb,),
        in_specs=[vecp, vecp, pl.BlockSpec((gb, 1, 1), lambda i: (i, 0, 0)), mat, mat, mat, mat],
        out_specs=(op, op, pl.BlockSpec((gb, n, n), lambda i: (i, 0, 0)),
                   pl.BlockSpec((gb, 1, 2 * p), lambda i: (i, 0, 0))),
        scratch_shapes=[pltpu.VMEM((n, n), F32)],
        compiler_params=_cparams(("parallel",)),
        name="ssm_prep",
    )(lam_re.reshape(g, 1, p), lam_im.reshape(g, 1, p), log_step.reshape(g, 1, 1),
      jnp.swapaxes(b_re, 1, 2), jnp.swapaxes(b_im, 1, 2), c_re, c_im)


def _ssm_fused_kernel(u_ref, w1_ref, mt_ref, tzt_ref, at_ref, d_ref, y_ref,
                      uf_ref, ufl_ref, s_ref, s2_ref, x_ref, yfl_ref, yt_ref, *, tc, gb):
    c = SSM_C
    l = uf_ref.shape[0]
    nch = l // tc
    uf_ref[...] = u_ref[0].astype(F32)
    for s in range(tc):
        tile = uf_ref[pl.ds(s, nch, stride=tc), :]
        for g in range(gb):
            ufl_ref[g, :, s * c:(s + 1) * c] = tile[:, g * c:(g + 1) * c]
    for g in range(gb):
        sg = jnp.dot(ufl_ref[g].astype(BF16), w1_ref[g], preferred_element_type=F32)
        s_ref[pl.ds(g, nch, stride=gb), :] = sg
    a = at_ref[...].reshape(gb, 2 * SSM_P)
    lane = lax.broadcasted_iota(I32, a.shape, 1)
    a_sw = pltpu.roll(a, SSM_P, 1)
    a_re = jnp.where(lane < SSM_P, a, a_sw)
    a_im_s = jnp.where(lane < SSM_P, -a_sw, a)

    s2_ref[...] = pltpu.roll(s_ref[...], SSM_P, 1)

    def step(ci, carry):
        x1, x2 = carry
        r0 = pl.multiple_of(ci * gb, gb)
        x_ref[pl.ds(r0, gb), :] = x1
        n1 = x1 * a_re + x2 * a_im_s + s_ref[pl.ds(r0, gb), :]
        n2 = x2 * a_re - x1 * a_im_s + s2_ref[pl.ds(r0, gb), :]
        return n1, n2

    zero = jnp.zeros((gb, 2 * SSM_P), F32)
    lax.fori_loop(0, nch, step, (zero, zero))
    for g in range(gb):
        u = ufl_ref[g]
        xg = x_ref[pl.ds(g, nch, stride=gb), :]
        y = _dot_nt(u.astype(BF16), tzt_ref[g]) + _dot_nt(xg.astype(BF16), mt_ref[g])
        yfl_ref[g] = jax.nn.gelu(y + u * d_ref[g])
    for t in range(tc):
        tile = jnp.concatenate([yfl_ref[g, :, t * c:(t + 1) * c] for g in range(gb)], axis=1)
        yt_ref[pl.ds(t, nch, stride=tc), :] = tile
    y_ref[0] = yt_ref[...].astype(y_ref.dtype)


def _s5(z3, ops, d_skip, tc, gb):
    w1, mt, tzt, at = ops
    b, l, _ = z3.shape
    c = SSM_C
    g = w1.shape[0]
    nch = l // tc
    n = tc * c
    p2 = 2 * SSM_P
    wl = gb * c
    dt = jnp.tile(d_skip.reshape(g, 1, c), (1, 1, tc)).astype(F32)
    col0 = COL_CU * LANES // wl
    return pl.pallas_call(
        functools.partial(_ssm_fused_kernel, tc=tc, gb=gb),
        out_shape=jax.ShapeDtypeStruct((b, l, g * c), BF16),
        grid=(g // gb, b),
        in_specs=[pl.BlockSpec((1, l, wl), lambda gi, bi: (bi, 0, col0 + gi)),
                  pl.BlockSpec((gb, n, p2), lambda gi, bi: (gi, 0, 0)),
                  pl.BlockSpec((gb, n, p2), lambda gi, bi: (gi, 0, 0)),
                  pl.BlockSpec((gb, n, n), lambda gi, bi: (gi, 0, 0)),
                  pl.BlockSpec((gb, 1, p2), lambda gi, bi: (gi, 0, 0)),
                  pl.BlockSpec((gb, 1, n), lambda gi, bi: (gi, 0, 0))],
        out_specs=pl.BlockSpec((1, l, wl), lambda gi, bi: (bi, 0, gi)),
        scratch_shapes=[pltpu.VMEM((l, wl), F32),
                        pltpu.VMEM((gb, nch, n), F32),
                        pltpu.VMEM((nch * gb, p2), F32),
                        pltpu.VMEM((nch * gb, p2), F32),
                        pltpu.VMEM((nch * gb, p2), F32),
                        pltpu.VMEM((gb, nch, n), F32),
                        pltpu.VMEM((l, wl), F32)],
        compiler_params=_cparams(("parallel", "arbitrary")),
        name="ssm_fused",
    )(z3, w1, mt, tzt, at, dt)


def _glu_kernel(y_ref, yn_ref, w_ref, o_ref):
    gate = jnp.dot(y_ref[...], w_ref[...], preferred_element_type=F32)
    o_ref[...] = (yn_ref[...].astype(F32) * jax.nn.sigmoid(gate)).astype(o_ref.dtype)


def _glu(y2d, w):
    t, k = y2d.shape
    tm = min(t, 1024)
    tn = min(k, 512)
    return pl.pallas_call(
        _glu_kernel,
        out_shape=jax.ShapeDtypeStruct((t, k), BF16),
        grid=(t // tm, k // tn),
        in_specs=[pl.BlockSpec((tm, k), lambda m, n: (m, 0)),
                  pl.BlockSpec((tm, tn), lambda m, n: (m, n)),
                  pl.BlockSpec((k, tn), lambda m, n: (0, n))],
        out_specs=pl.BlockSpec((tm, tn), lambda m, n: (m, n)),
        compiler_params=_cparams(("parallel", "arbitrary")),
        name="ssm_glu",
    )(y2d, y2d, w)


def _outproj_kernel(h_ref, ya_ref, yb_ref, yc_ref, wa_ref, wb_ref, wc_ref, o_ref):
    acc = jnp.dot(ya_ref[...], wa_ref[...], preferred_element_type=F32)
    acc += jnp.dot(yb_ref[...], wb_ref[...], preferred_element_type=F32)
    acc += jnp.dot(yc_ref[...], wc_ref[...], preferred_element_type=F32)
    o_ref[...] = h_ref[...] + acc


def _outproj(h2d, ya, yb, yc, w):
    t, d = h2d.shape
    ka, kb, kc = ya.shape[1], yb.shape[1], yc.shape[1]
    tm = min(t, 512)
    tn = min(d, 512)
    return pl.pallas_call(
        _outproj_kernel,
        out_shape=jax.ShapeDtypeStruct((t, d), F32),
        grid=(t // tm, d // tn),
        in_specs=[pl.BlockSpec((tm, tn), lambda m, n: (m, n)),
                  pl.BlockSpec((tm, ka), lambda m, n: (m, 0)),
                  pl.BlockSpec((tm, kb), lambda m, n: (m, 0)),
                  pl.BlockSpec((tm, kc), lambda m, n: (m, 0)),
                  pl.BlockSpec((ka, tn), lambda m, n: (0, n)),
                  pl.BlockSpec((kb, tn), lambda m, n: (0, n)),
                  pl.BlockSpec((kc, tn), lambda m, n: (0, n))],
        out_specs=pl.BlockSpec((tm, tn), lambda m, n: (m, n)),
        compiler_params=_cparams(("parallel", "arbitrary")),
        name="out_proj",
    )(h2d, ya, yb, yc, w[:ka], w[ka:ka + kb], w[ka + kb:])


def _mlp_kernel(h_ref, g_ref, wu_ref, wd_ref, o_ref, xn_ref, acc_ref):
    f = pl.program_id(1)

    @pl.when(f == 0)
    def _():
        xn_ref[...] = _rms(h_ref[...], g_ref[...]).astype(BF16)
        acc_ref[...] = h_ref[...]

    hid = jnp.dot(xn_ref[...], wu_ref[...], preferred_element_type=F32)
    hid = jnp.square(jnp.maximum(hid, 0.0)).astype(BF16)
    acc_ref[...] += jnp.dot(hid, wd_ref[...], preferred_element_type=F32)

    @pl.when(f == pl.num_programs(1) - 1)
    def _():
        o_ref[...] = acc_ref[...]


def _mlp(h2d, g, w_up, w_down):
    t, d = h2d.shape
    ff = w_up.shape[1]
    tm = min(t, 512)
    tf = min(ff, 512)
    return pl.pallas_call(
        _mlp_kernel,
        out_shape=jax.ShapeDtypeStruct((t, d), F32),
        grid=(t // tm, ff // tf),
        in_specs=[pl.BlockSpec((tm, d), lambda m, f: (m, 0)),
                  pl.BlockSpec((1, d), lambda m, f: (0, 0)),
                  pl.BlockSpec((d, tf), lambda m, f: (0, f)),
                  pl.BlockSpec((tf, d), lambda m, f: (f, 0))],
        out_specs=pl.BlockSpec((tm, d), lambda m, f: (m, 0)),
        scratch_shapes=[pltpu.VMEM((tm, d), BF16), pltpu.VMEM((tm, d), F32)],
        compiler_params=_cparams(("parallel", "arbitrary")),
        name="mlp_relu2",
    )(h2d, g.reshape(1, d), w_up, w_down)


def _ple_kernel(h_ref, g_ref, p_ref, wg_ref, wp_ref, fg_ref, o_ref, *, tn, final):
    h = h_ref[...]
    xn = _rms(h, g_ref[...]).astype(BF16)
    pb = p_ref[...]
    d = h.shape[1]
    for j in range(d // tn):
        sl = slice(j * tn, (j + 1) * tn)
        gate = jax.nn.sigmoid(jnp.dot(xn, wg_ref[:, sl], preferred_element_type=F32))
        proj = jnp.dot(pb, wp_ref[:, sl], preferred_element_type=F32)
        o_ref[:, sl] = h[:, sl] + gate * proj
    if final:
        o_ref[...] = _rms(o_ref[...], fg_ref[...])


def _ple(h2d, g, p2d, wg, wp, final_g, final):
    t, d = h2d.shape
    pd = p2d.shape[1]
    tm = min(t, 256)
    return pl.pallas_call(
        functools.partial(_ple_kernel, tn=min(d, 512), final=final),
        out_shape=jax.ShapeDtypeStruct((t, d), F32),
        grid=(t // tm,),
        in_specs=[pl.BlockSpec((tm, d), lambda m: (m, 0)),
                  pl.BlockSpec((1, d), lambda m: (0, 0)),
                  pl.BlockSpec((tm, pd), lambda m: (m, 0)),
                  pl.BlockSpec((d, d), lambda m: (0, 0)),
                  pl.BlockSpec((pd, d), lambda m: (0, 0)),
                  pl.BlockSpec((1, d), lambda m: (0, 0))],
        out_specs=pl.BlockSpec((tm, d), lambda m: (m, 0)),
        compiler_params=_cparams(("parallel",)),
        name="ple_gate",
    )(h2d, g.reshape(1, d), p2d, wg, wp, final_g.reshape(1, d))


SSM_CHUNK = 32
SSM_GROUP_BLOCK = 8


def kernel(x, p, positions, norm_mix_g, w_in, w_out, diff_lq1, diff_lk1, diff_lq2, diff_lk2,
           diff_subln_g, ssm_lambda_re, ssm_lambda_im, ssm_log_step, ssm_B_re, ssm_B_im,
           ssm_C_re, ssm_C_im, ssm_D, ssm_w_glu, norm_mlp_g, w_up, w_down, norm_ple_g,
           w_ple_gate, w_ple_proj, final_g):
    b, l, d = x.shape
    depth = w_in.shape[0]
    t = b * l
    tc = min(SSM_CHUNK, l)
    tabs = _rope_tables(positions)
    h = x.reshape(t, d)
    for i in range(depth):
        lambda_init = 0.8 - 0.6 * math.exp(-0.3 * i)
        z = _inproj(h, norm_mix_g[i], _permute_w_in(w_in[i]), tabs)
        z3 = z.reshape(b, l, Z_WIDTH)
        y_a = _diff_attn(z3, diff_lq1[i], diff_lk1[i], diff_lq2[i], diff_lk2[i],
                         diff_subln_g[i], lambda_init)
        y_b = _dsa_attn(z3)
        ops = _ssm_prep(ssm_lambda_re[i], ssm_lambda_im[i], ssm_log_step[i], ssm_B_re[i],
                        ssm_B_im[i], ssm_C_re[i], ssm_C_im[i], tc, SSM_GROUP_BLOCK)
        y_s = _s5(z3, ops, ssm_D[i], tc, SSM_GROUP_BLOCK)
        y_c = _glu(y_s.reshape(t, -1), ssm_w_glu[i].astype(BF16))
        h = _outproj(h, y_a.reshape(t, -1), y_b.reshape(t, -1), y_c, w_out[i].astype(BF16))
        h = _mlp(h, norm_mlp_g[i], w_up[i].astype(BF16), w_down[i].astype(BF16))
        h = _ple(h, norm_ple_g[i], p[i].reshape(t, -1).astype(BF16), w_ple_gate[i].astype(BF16),
                 w_ple_proj[i].astype(BF16), final_g, final=(i == depth - 1))
    return h.reshape(b, l, d)
```

```python
import functools
import math

import jax
import jax.numpy as jnp
from jax import lax
from jax.experimental import pallas as pl
from jax.experimental.pallas import tpu as pltpu

F32 = jnp.float32
BF16 = jnp.bfloat16
I32 = jnp.int32

LANES = 128
EPS = 1e-6
ROPE_THETA = 10000.0

DIFF_QK = 64
DIFF_V = 128
DIFF_HEADS = 4
DSA_DIM = 128
DSA_HEADS = 4
IDX_HEADS = 8
IDX_DIM = 64
TOPK_MAX = 256
SSM_C = 16
SSM_P = 64

COL_AQ, COL_AK, COL_IQ, COL_BQ, COL_AV = 0, 4, 8, 12, 16
COL_BK, COL_BV, COL_IKW, COL_CU = 20, 21, 22, 24
Z_WIDTH = 32 * LANES
IN_TN = 512

DSA_TQ = 256
DSA_TK = 256
NEG_BIG = -1e30
LOG2E = math.log2(math.e)
VMEM_LIMIT = 56 * 1024 * 1024


def _cparams(sem):
    return pltpu.CompilerParams(dimension_semantics=sem, vmem_limit_bytes=VMEM_LIMIT)


def _rope_tables_kernel(pos_ref, f64_ref, f128_ref, c64_ref, s64_ref, c128_ref, s128_ref):
    pos = pos_ref[...]
    a64 = pos * f64_ref[...]
    a128 = pos * f128_ref[...]
    lane = lax.broadcasted_iota(I32, a64.shape, 1)
    c64_ref[...] = jnp.cos(a64)
    sn = jnp.sin(a64)
    s64_ref[...] = jnp.where((lane & 32) == 0, -sn, sn)
    c128_ref[...] = jnp.cos(a128)
    sn = jnp.sin(a128)
    s128_ref[...] = jnp.where((lane & 64) == 0, -sn, sn)


def _rope_tables(positions):
    t = positions.size
    pos = positions.reshape(t, 1).astype(F32)
    fr64 = ROPE_THETA ** (-jnp.arange(0, 64, 2, dtype=F32) / 64)
    fr128 = ROPE_THETA ** (-jnp.arange(0, 128, 2, dtype=F32) / 128)
    f64 = jnp.tile(fr64, 4).reshape(1, LANES)
    f128 = jnp.tile(fr128, 2).reshape(1, LANES)
    tm = min(t, 1024)
    tab = jax.ShapeDtypeStruct((t, LANES), F32)
    row = pl.BlockSpec((tm, LANES), lambda i: (i, 0))
    return pl.pallas_call(
        _rope_tables_kernel,
        out_shape=(tab, tab, tab, tab),
        grid=(t // tm,),
        in_specs=[pl.BlockSpec((tm, 1), lambda i: (i, 0)),
                  pl.BlockSpec((1, LANES), lambda i: (0, 0)),
                  pl.BlockSpec((1, LANES), lambda i: (0, 0))],
        out_specs=(row, row, row, row),
        compiler_params=_cparams(("parallel",)),
        name="rope_tables",
    )(pos, f64, f128)


def _rms(x, g):
    ms = jnp.mean(x * x, axis=-1, keepdims=True)
    return x * lax.rsqrt(ms + EPS) * g


def _swap_halves(a, half):
    lane = lax.broadcasted_iota(I32, a.shape, 1)
    return jnp.where((lane & half) == 0,
                     pltpu.roll(a, LANES - half, 1), pltpu.roll(a, half, 1))


def _inproj_kernel(x_ref, g_ref, w_ref, c64_ref, s64_ref, c128_ref, s128_ref, z_ref, xn_ref):
    n = pl.program_id(1)

    @pl.when(n == 0)
    def _():
        xn_ref[...] = _rms(x_ref[...], g_ref[...]).astype(BF16)

    acc = jnp.dot(xn_ref[...], w_ref[...], preferred_element_type=F32)
    groups = acc.shape[1] // LANES

    def rope(a, half):
        if half == 32:
            return a * c64_ref[...] + _swap_halves(a, 32) * s64_ref[...]
        return a * c128_ref[...] + _swap_halves(a, 64) * s128_ref[...]

    def grp(j):
        return acc[:, j * LANES:(j + 1) * LANES]

    def put(j, v):
        z_ref[:, j * LANES:(j + 1) * LANES] = v.astype(z_ref.dtype)

    @pl.when(n < 3)
    def _():
        for j in range(groups):
            put(j, rope(grp(j), 32))

    @pl.when(n == 3)
    def _():
        for j in range(groups):
            put(j, rope(grp(j), 64))

    @pl.when(n == 5)
    def _():
        put(0, rope(grp(0), 64))
        put(1, grp(1))
        a = grp(2)
        lane = lax.broadcasted_iota(I32, a.shape, 1)
        put(2, jnp.where(lane < IDX_DIM, rope(a, 32), a))
        put(3, grp(3))

    @pl.when((n == 4) | (n > 5))
    def _():
        z_ref[...] = acc.astype(z_ref.dtype)


def _inproj(h2d, g, w, tabs):
    t, d = h2d.shape
    tm = min(t, 512)
    tn = IN_TN
    c64, s64, c128, s128 = tabs
    tab = pl.BlockSpec((tm, LANES), lambda m, n: (m, 0))
    return pl.pallas_call(
        _inproj_kernel,
        out_shape=jax.ShapeDtypeStruct((t, Z_WIDTH), BF16),
        grid=(t // tm, Z_WIDTH // tn),
        in_specs=[pl.BlockSpec((tm, d), lambda m, n: (m, 0)),
                  pl.BlockSpec((1, d), lambda m, n: (0, 0)),
                  pl.BlockSpec((d, tn), lambda m, n: (0, n)),
                  tab, tab, tab, tab],
        out_specs=pl.BlockSpec((tm, tn), lambda m, n: (m, n)),
        scratch_shapes=[pltpu.VMEM((tm, d), BF16)],
        compiler_params=_cparams(("parallel", "arbitrary")),
        name="norm_inproj_rope",
    )(h2d, g.reshape(1, d), w, c64, s64, c128, s128)


def _permute_w_in(w):
    d = w.shape[0]
    pad = jnp.zeros((d, 56 + LANES), w.dtype)
    return jnp.concatenate(
        [w[:, 0:512], w[:, 512:1024], w[:, 2304:2816], w[:, 1536:2048], w[:, 1024:1536],
         w[:, 2048:2176], w[:, 2176:2304], w[:, 2816:2888], pad, w[:, 2888:3912]],
        axis=1).astype(BF16)


def _flash_step(carry, s, vb):
    m, l, acc = carry
    m_new = jnp.maximum(m, jnp.max(s, axis=1, keepdims=True))
    alpha = jnp.exp2(m - m_new)
    p = jnp.exp2(s - m_new)
    l = alpha * l + jnp.sum(p, axis=1, keepdims=True)
    acc = alpha * acc + jnp.dot(p.astype(BF16), vb, preferred_element_type=F32)
    return m_new, l, acc


def _flash_init(rows, dv):
    return (jnp.full((rows, 1), NEG_BIG, F32), jnp.zeros((rows, 1), F32),
            jnp.zeros((rows, dv), F32))


def _dot_nt(a, b):
    return lax.dot_general(a, b, (((1,), (1,)), ((), ())), preferred_element_type=F32)


def _diff_attn_kernel(q_ref, k_ref, v_ref, lq1_ref, lk1_ref, lq2_ref, lk2_ref, g_ref, o_ref,
                      *, tq, tk, lambda_init):
    i = pl.program_id(2)
    q = q_ref[0].astype(F32) * (DIFF_QK ** -0.5)
    lane = lax.broadcasted_iota(I32, q.shape, 1)
    q2 = jnp.concatenate([jnp.where(lane < DIFF_QK, q, 0.0),
                          jnp.where(lane >= DIFF_QK, q, 0.0)], axis=0).astype(BF16)

    def block(j, carry, diag):
        kb = k_ref[0, pl.ds(j * tk, tk), :]
        vb = v_ref[0, pl.ds(j * tk, tk), :]
        s = _dot_nt(q2, kb) * LOG2E
        if diag:
            row = lax.broadcasted_iota(I32, s.shape, 0)
            row = i * tq + jnp.where(row >= tq, row - tq, row)
            col = j * tk + lax.broadcasted_iota(I32, s.shape, 1)
            s = jnp.where(col <= row, s, NEG_BIG)
        return _flash_step(carry, s, vb)

    init = _flash_init(2 * tq, DIFF_V)
    nfull = (i * tq) // tk
    carry = lax.fori_loop(0, nfull, lambda j, c: block(j, c, False), init)
    _, l, acc = block(nfull, carry, True)
    o = acc / l
    lam = (jnp.exp(jnp.sum(lq1_ref[...] * lk1_ref[...], keepdims=True))
           - jnp.exp(jnp.sum(lq2_ref[...] * lk2_ref[...], keepdims=True)) + lambda_init)
    out = o[:tq] - lam * o[tq:]
    out = _rms(out, g_ref[...]) * (1.0 - lambda_init)
    o_ref[0] = out.astype(o_ref.dtype)


def _diff_attn(z3, lq1, lk1, lq2, lk2, subln_g, lambda_init):
    b, l, _ = z3.shape
    tq = min(l, 256)
    tk = min(l, 512)
    vec = pl.BlockSpec((1, DIFF_QK), lambda bi, h, i: (0, 0))
    return pl.pallas_call(
        functools.partial(_diff_attn_kernel, tq=tq, tk=tk, lambda_init=lambda_init),
        out_shape=jax.ShapeDtypeStruct((b, l, DIFF_HEADS * DIFF_V), BF16),
        grid=(b, DIFF_HEADS, l // tq),
        in_specs=[pl.BlockSpec((1, tq, LANES), lambda bi, h, i: (bi, i, COL_AQ + h)),
                  pl.BlockSpec((1, l, LANES), lambda bi, h, i: (bi, 0, COL_AK + h)),
                  pl.BlockSpec((1, l, LANES), lambda bi, h, i: (bi, 0, COL_AV + h)),
                  vec, vec, vec, vec,
                  pl.BlockSpec((1, DIFF_V), lambda bi, h, i: (0, 0))],
        out_specs=pl.BlockSpec((1, tq, DIFF_V), lambda bi, h, i: (bi, i, h)),
        compiler_params=_cparams(("parallel", "parallel", "arbitrary")),
        name="diff_attention",
    )(z3, z3, z3, lq1.reshape(1, -1), lk1.reshape(1, -1), lq2.reshape(1, -1),
      lk2.reshape(1, -1), subln_g.reshape(1, -1))


def _sortable_key(s):
    bits = lax.bitcast_convert_type(s, I32)
    return bits ^ ((bits >> 31) & jnp.int32(0x7FFFFFFF))


def _dsa_kernel(bq_ref, iq_ref, iwq_ref, bk_ref, bvt_ref, ik_ref, o_ref, hi_ref, lo_ref,
                *, tq, tk, topk, seq_bits):
    i = pl.program_id(1)
    nkb = ((i + 1) * tq + tk - 1) // tk
    low16 = -(2 ** 15)
    i16 = jnp.int16

    def as16(v):
        return v.astype(i16)

    krow = lax.broadcasted_iota(I32, (tk, tq), 0)
    qcol = i * tq + lax.broadcasted_iota(I32, (tk, tq), 1)

    iq_t = jnp.transpose(iq_ref[0].astype(F32))
    iq_all = jnp.concatenate([iq_t[h * IDX_DIM:(h + 1) * IDX_DIM, :] for h in range(IDX_HEADS)],
                             axis=1).astype(BF16)
    iw_t = jnp.transpose(iwq_ref[0].astype(F32))[IDX_DIM:IDX_DIM + IDX_HEADS, :]
    iw_t = iw_t * (IDX_HEADS ** -0.5) * (IDX_DIM ** -0.5)

    def score_block(j, _):
        ikb = ik_ref[0, pl.ds(j * tk, tk), :][:, :IDX_DIM]
        r_all = jnp.dot(ikb, iq_all, preferred_element_type=F32)
        sc = jnp.zeros((tk, tq), F32)
        for h in range(IDX_HEADS):
            sc = sc + jnp.maximum(r_all[:, h * tq:(h + 1) * tq], 0.0) * iw_t[h:h + 1, :]
        sc = jnp.where(j * tk + krow <= qcol, sc + 0.0, -jnp.inf)
        key = _sortable_key(sc)
        hi_ref[j] = as16(key >> 16)
        lo_ref[j] = as16((key & 0xFFFF) + low16)
        return 0

    lax.fori_loop(0, nkb, score_block, 0)

    one16, zero16 = jnp.ones((), i16), jnp.zeros((), i16)

    def count(pred_fn):
        def blk(j, cnt):
            m = jnp.where(pred_fn(j), one16, zero16)
            parts = [m[r * 16:(r + 1) * 16, :] for r in range(tk // 16)]
            while len(parts) > 1:
                parts = [a + b for a, b in zip(parts[::2], parts[1::2])]
            return cnt + parts[0]
        cnt = lax.fori_loop(0, nkb, blk, jnp.zeros((16, tq), i16))
        return jnp.sum(cnt.astype(I32), axis=0, keepdims=True)

    def radix_select(ref, target):
        c0 = count(lambda j: ref[j] >= zero16)
        thr = jnp.where(c0 >= target, 0, low16).astype(I32)

        def bit_step(it, thr):
            cand = thr | (jnp.int32(1) << (14 - it))
            c16 = as16(cand)
            return jnp.where(count(lambda j: ref[j] >= c16) >= target, cand, thr)

        return lax.fori_loop(0, 15, bit_step, thr)

    t_hi = as16(radix_select(hi_ref, topk))
    need_lo = topk - count(lambda j: hi_ref[j] > t_hi)

    def mask_lo(j, _):
        lo_ref[j] = jnp.where(hi_ref[j] == t_hi, lo_ref[j], jnp.int16(low16))
        return 0

    lax.fori_loop(0, nkb, mask_lo, 0)
    t_lo = as16(radix_select(lo_ref, need_lo))

    def in_tie(j):
        return (hi_ref[j] == t_hi) & (lo_ref[j] == t_lo)

    need = need_lo - count(lambda j: lo_ref[j] > t_lo)
    has_excess = jnp.max(jnp.where(count(in_tie) > need, 1, 0)) > 0
    krow16 = as16(krow)

    def tie_limit():
        def step(it, q):
            cand = q + (jnp.int32(1) << (seq_bits - 1 - it))
            c16 = as16(cand)
            c = count(lambda j: in_tie(j) & (krow16 + as16(j * tk) < c16))
            return jnp.where(c < need, cand, q)
        return lax.fori_loop(0, seq_bits, step, jnp.zeros((1, tq), I32))

    jlim16 = as16(lax.cond(has_excess, tie_limit, lambda: jnp.full((1, tq), 2 ** seq_bits, I32)))

    q_t = jnp.transpose(bq_ref[0].astype(F32))
    q_all = jnp.concatenate([q_t[h * DSA_DIM:(h + 1) * DSA_DIM, :] for h in range(DSA_HEADS)],
                            axis=1).astype(BF16)
    scale = DSA_DIM ** -0.5 * LOG2E
    zero_b, neg_b = jnp.zeros((), BF16), jnp.full((), NEG_BIG, BF16)
    qcol16 = as16(qcol)

    def attn_block(j, carry):
        kb = bk_ref[0, pl.ds(j * tk, tk), :]
        vtb = bvt_ref[0, :, pl.ds(j * tk, tk)]
        hi, lo = hi_ref[j], lo_ref[j]
        kidx = krow16 + as16(j * tk)
        sel = (hi > t_hi) | ((hi == t_hi) & ((lo > t_lo) | ((lo == t_lo) & (kidx <= jlim16))))
        sel = sel & (kidx <= qcol16)
        bias = jnp.where(sel, zero_b, neg_b).astype(F32)
        m, l, acc = carry
        s = (jnp.dot(kb, q_all, preferred_element_type=F32) * scale
             + jnp.concatenate([bias] * DSA_HEADS, axis=1))
        m_new = jnp.maximum(m, jnp.max(s, axis=0, keepdims=True))
        alpha = jnp.exp2(m - m_new)
        p = jnp.exp2(s - m_new)
        l = alpha * l + jnp.sum(p, axis=0, keepdims=True)
        pb = p.astype(BF16)
        pv = jnp.concatenate(
            [jnp.dot(vtb, pb[:, h * tq:(h + 1) * tq], preferred_element_type=F32)
             for h in range(DSA_HEADS)], axis=1)
        return m_new, l, alpha * acc + pv

    wide_q = DSA_HEADS * tq
    init = (jnp.full((1, wide_q), NEG_BIG, F32), jnp.zeros((1, wide_q), F32),
            jnp.zeros((DSA_DIM, wide_q), F32))
    _, l, acc = lax.fori_loop(0, nkb, attn_block, init)
    o = acc / l
    for h in range(DSA_HEADS):
        o_ref[0, :, h * DSA_DIM:(h + 1) * DSA_DIM] = jnp.transpose(
            o[:, h * tq:(h + 1) * tq]).astype(o_ref.dtype)


def _dsa_attn(z3):
    b, l, _ = z3.shape
    tq = min(l, DSA_TQ)
    tk = min(l, DSA_TK)
    topk = min(TOPK_MAX, l // 4)
    seq_bits = max(1, (l - 1).bit_length())
    wide = 4
    v_t = jnp.swapaxes(z3[:, :, COL_BV * LANES:(COL_BV + 1) * LANES], 1, 2)
    return pl.pallas_call(
        functools.partial(_dsa_kernel, tq=tq, tk=tk, topk=topk, seq_bits=seq_bits),
        out_shape=jax.ShapeDtypeStruct((b, l, DSA_HEADS * DSA_DIM), BF16),
        grid=(b, l // tq),
        in_specs=[pl.BlockSpec((1, tq, 4 * LANES), lambda bi, i: (bi, i, COL_BQ // wide)),
                  pl.BlockSpec((1, tq, 4 * LANES), lambda bi, i: (bi, i, COL_IQ // wide)),
                  pl.BlockSpec((1, tq, LANES), lambda bi, i: (bi, i, COL_IKW)),
                  pl.BlockSpec((1, l, LANES), lambda bi, i: (bi, 0, COL_BK)),
                  pl.BlockSpec((1, DSA_DIM, l), lambda bi, i: (bi, 0, 0)),
                  pl.BlockSpec((1, l, LANES), lambda bi, i: (bi, 0, COL_IKW))],
        out_specs=pl.BlockSpec((1, tq, DSA_HEADS * DSA_DIM), lambda bi, i: (bi, i, 0)),
        scratch_shapes=[pltpu.VMEM((l // tk, tk, tq), jnp.int16),
                        pltpu.VMEM((l // tk, tk, tq), jnp.int16)],
        compiler_params=_cparams(("parallel", "arbitrary")),
        name="dsa_attention",
    )(z3, z3, z3, z3, v_t, z3)


def _cmul(ar, ai, br, bi):
    return ar * br - ai * bi, ar * bi + ai * br


def _ssm_prep_kernel(lre_ref, lim_ref, lstep_ref, btr_ref, bti_ref, cr_ref, ci_ref,
                     w1_ref, mt_ref, tzt_ref, at_ref, tz_ref, *, tc, gb):
    c = SSM_C
    p = SSM_P
    for gi in range(gb):
        lr = lre_ref[gi].reshape(1, 1, p)
        li = lim_ref[gi].reshape(1, 1, p)
        step = jnp.exp(lstep_ref[gi]).reshape(1, 1, 1)
        mag = jnp.exp(lr * step)
        a_re, a_im = mag * jnp.cos(li * step), mag * jnp.sin(li * step)
        den = lr * lr + li * li
        nr, ni = a_re - 1.0, a_im
        f_re, f_im = (nr * lr + ni * li) / den, (ni * lr - nr * li) / den
        bt_r, bt_i = btr_ref[gi][None], bti_ref[gi][None]
        bb_re = f_re * bt_r - f_im * bt_i
        bb_im = f_re * bt_i + f_im * bt_r
        fr, fi = jnp.ones_like(a_re), jnp.zeros_like(a_im)
        rr, ri = fr, fi
        pr, pi = a_re, a_im
        n = 1
        while n < tc:
            xr, xi = _cmul(fr, fi, pr, pi)
            fr, fi = jnp.concatenate([fr, xr], 0), jnp.concatenate([fi, xi], 0)
            xr, xi = _cmul(rr, ri, pr, pi)
            rr, ri = jnp.concatenate([xr, rr], 0), jnp.concatenate([xi, ri], 0)
            pr, pi = _cmul(pr, pi, pr, pi)
            n *= 2
        at_ref[gi] = jnp.concatenate([pr[0], pi[0]], axis=1)
        wr, wi = _cmul(rr, ri, bb_re, bb_im)
        w1 = jnp.concatenate([wr, wi], axis=2).reshape(tc * c, 2 * p)
        w1_ref[gi] = w1.astype(w1_ref.dtype)
        f1r, f1i = _cmul(fr, fi, a_re, a_im)
        c_re, c_im = cr_ref[gi][None], ci_ref[gi][None]
        mr, mi = _cmul(f1r, f1i, c_re, c_im)
        mt_ref[gi] = jnp.concatenate([mr, -mi], axis=2).reshape(tc * c, 2 * p).astype(mt_ref.dtype)
        er, ei = _cmul(fr, fi, c_re, c_im)
        e2 = jnp.concatenate([er, -ei], axis=2).reshape(tc * c, 2 * p)
        bcat = jnp.concatenate([bb_re[0], bb_im[0]], axis=1)
        kflat = lax.dot_general(e2, bcat, (((1,), (1,)), ((), ())),
                                precision=lax.Precision.HIGHEST,
                                preferred_element_type=F32)
        tz_ref[...] = jnp.zeros(tz_ref.shape, tz_ref.dtype)
        for s in range(tc):
            tz_ref[s * c:, s * c:(s + 1) * c] = kflat[:(tc - s) * c, :]
        tzt_ref[gi] = tz_ref[...].astype(tzt_ref.dtype)


def _ssm_prep(lam_re, lam_im, log_step, b_re, b_im, c_re, c_im, tc, gb):
    g, p = lam_re.shape
    c = SSM_C
    n = tc * c
    vecp = pl.BlockSpec((gb, 1, p), lambda i: (i, 0, 0))
    mat = pl.BlockSpec((gb, c, p), lambda i: (i, 0, 0))
    op = pl.BlockSpec((gb, n, 2 * p), lambda i: (i, 0, 0))
    return pl.pallas_call(
        functools.partial(_ssm_prep_kernel, tc=tc, gb=gb),
        out_shape=(jax.ShapeDtypeStruct((g, n, 2 * p), BF16),
                   jax.ShapeDtypeStruct((g, n, 2 * p), BF16),
                   jax.ShapeDtypeStruct((g, n, n), BF16),
                   jax.ShapeDtypeStruct((g, 1, 2 * p), F32)),
        grid=(g // gb,),
        in_specs=[vecp, vecp, pl.BlockSpec((gb, 1, 1), lambda i: (i, 0, 0)), mat, mat, mat, mat],
        out_specs=(op, op, pl.BlockSpec((gb, n, n), lambda i: (i, 0, 0)),
                   pl.BlockSpec((gb, 1, 2 * p), lambda i: (i, 0, 0))),
        scratch_shapes=[pltpu.VMEM((n, n), F32)],
        compiler_params=_cparams(("parallel",)),
        name="ssm_prep",
    )(lam_re.reshape(g, 1, p), lam_im.reshape(g, 1, p), log_step.reshape(g, 1, 1),
      jnp.swapaxes(b_re, 1, 2), jnp.swapaxes(b_im, 1, 2), c_re, c_im)


def _ssm_fused_kernel(u_ref, w1_ref, mt_ref, tzt_ref, at_ref, d_ref, y_ref,
                      uf_ref, ufl_ref, s_ref, s2_ref, x_ref, yfl_ref, yt_ref, *, tc, gb):
    c = SSM_C
    l = uf_ref.shape[0]
    nch = l // tc
    uf_ref[...] = u_ref[0].astype(F32)
    for s in range(tc):
        tile = uf_ref[pl.ds(s, nch, stride=tc), :]
        for g in range(gb):
            ufl_ref[g, :, s * c:(s + 1) * c] = tile[:, g * c:(g + 1) * c]
    for g in range(gb):
        sg = jnp.dot(ufl_ref[g].astype(BF16), w1_ref[g], preferred_element_type=F32)
        s_ref[pl.ds(g, nch, stride=gb), :] = sg
    a = at_ref[...].reshape(gb, 2 * SSM_P)
    lane = lax.broadcasted_iota(I32, a.shape, 1)
    a_sw = pltpu.roll(a, SSM_P, 1)
    a_re = jnp.where(lane < SSM_P, a, a_sw)
    a_im_s = jnp.where(lane < SSM_P, -a_sw, a)

    s2_ref[...] = pltpu.roll(s_ref[...], SSM_P, 1)

    def step(ci, carry):
        x1, x2 = carry
        r0 = pl.multiple_of(ci * gb, gb)
        x_ref[pl.ds(r0, gb), :] = x1
        n1 = x1 * a_re + x2 * a_im_s + s_ref[pl.ds(r0, gb), :]
        n2 = x2 * a_re - x1 * a_im_s + s2_ref[pl.ds(r0, gb), :]
        return n1, n2

    zero = jnp.zeros((gb, 2 * SSM_P), F32)
    lax.fori_loop(0, nch, step, (zero, zero))
    for g in range(gb):
        u = ufl_ref[g]
        xg = x_ref[pl.ds(g, nch, stride=gb), :]
        y = _dot_nt(u.astype(BF16), tzt_ref[g]) + _dot_nt(xg.astype(BF16), mt_ref[g])
        yfl_ref[g] = jax.nn.gelu(y + u * d_ref[g])
    for t in range(tc):
        tile = jnp.concatenate([yfl_ref[g, :, t * c:(t + 1) * c] for g in range(gb)], axis=1)
        yt_ref[pl.ds(t, nch, stride=tc), :] = tile
    y_ref[0] = yt_ref[...].astype(y_ref.dtype)


def _s5(z3, ops, d_skip, tc, gb):
    w1, mt, tzt, at = ops
    b, l, _ = z3.shape
    c = SSM_C
    g = w1.shape[0]
    nch = l // tc
    n = tc * c
    p2 = 2 * SSM_P
    wl = gb * c
    dt = jnp.tile(d_skip.reshape(g, 1, c), (1, 1, tc)).astype(F32)
    col0 = COL_CU * LANES // wl
    return pl.pallas_call(
        functools.partial(_ssm_fused_kernel, tc=tc, gb=gb),
        out_shape=jax.ShapeDtypeStruct((b, l, g * c), BF16),
        grid=(g // gb, b),
        in_specs=[pl.BlockSpec((1, l, wl), lambda gi, bi: (bi, 0, col0 + gi)),
                  pl.BlockSpec((gb, n, p2), lambda gi, bi: (gi, 0, 0)),
                  pl.BlockSpec((gb, n, p2), lambda gi, bi: (gi, 0, 0)),
                  pl.BlockSpec((gb, n, n), lambda gi, bi: (gi, 0, 0)),
                  pl.BlockSpec((gb, 1, p2), lambda gi, bi: (gi, 0, 0)),
                  pl.BlockSpec((gb, 1, n), lambda gi, bi: (gi, 0, 0))],
        out_specs=pl.BlockSpec((1, l, wl), lambda gi, bi: (bi, 0, gi)),
        scratch_shapes=[pltpu.VMEM((l, wl), F32),
                        pltpu.VMEM((gb, nch, n), F32),
                        pltpu.VMEM((nch * gb, p2), F32),
                        pltpu.VMEM((nch * gb, p2), F32),
                        pltpu.VMEM((nch * gb, p2), F32),
                        pltpu.VMEM((gb, nch, n), F32),
                        pltpu.VMEM((l, wl), F32)],
        compiler_params=_cparams(("parallel", "arbitrary")),
        name="ssm_fused",
    )(z3, w1, mt, tzt, at, dt)


def _glu_kernel(y_ref, yn_ref, w_ref, o_ref):
    gate = jnp.dot(y_ref[...], w_ref[...], preferred_element_type=F32)
    o_ref[...] = (yn_ref[...].astype(F32) * jax.nn.sigmoid(gate)).astype(o_ref.dtype)


def _glu(y2d, w):
    t, k = y2d.shape
    tm = min(t, 1024)
    tn = min(k, 512)
    return pl.pallas_call(
        _glu_kernel,
        out_shape=jax.ShapeDtypeStruct((t, k), BF16),
        grid=(t // tm, k // tn),
        in_specs=[pl.BlockSpec((tm, k), lambda m, n: (m, 0)),
                  pl.BlockSpec((tm, tn), lambda m, n: (m, n)),
                  pl.BlockSpec((k, tn), lambda m, n: (0, n))],
        out_specs=pl.BlockSpec((tm, tn), lambda m, n: (m, n)),
        compiler_params=_cparams(("parallel", "arbitrary")),
        name="ssm_glu",
    )(y2d, y2d, w)


def _outproj_kernel(h_ref, ya_ref, yb_ref, yc_ref, wa_ref, wb_ref, wc_ref, o_ref):
    acc = jnp.dot(ya_ref[...], wa_ref[...], preferred_element_type=F32)
    acc += jnp.dot(yb_ref[...], wb_ref[...], preferred_element_type=F32)
    acc += jnp.dot(yc_ref[...], wc_ref[...], preferred_element_type=F32)
    o_ref[...] = h_ref[...] + acc


def _outproj(h2d, ya, yb, yc, w):
    t, d = h2d.shape
    ka, kb, kc = ya.shape[1], yb.shape[1], yc.shape[1]
    tm = min(t, 512)
    tn = min(d, 512)
    return pl.pallas_call(
        _outproj_kernel,
        out_shape=jax.ShapeDtypeStruct((t, d), F32),
        grid=(t // tm, d // tn),
        in_specs=[pl.BlockSpec((tm, tn), lambda m, n: (m, n)),
                  pl.BlockSpec((tm, ka), lambda m, n: (m, 0)),
                  pl.BlockSpec((tm, kb), lambda m, n: (m, 0)),
                  pl.BlockSpec((tm, kc), lambda m, n: (m, 0)),
                  pl.BlockSpec((ka, tn), lambda m, n: (0, n)),
                  pl.BlockSpec((kb, tn), lambda m, n: (0, n)),
                  pl.BlockSpec((kc, tn), lambda m, n: (0, n))],
        out_specs=pl.BlockSpec((tm, tn), lambda m, n: (m, n)),
        compiler_params=_cparams(("parallel", "arbitrary")),
        name="out_proj",
    )(h2d, ya, yb, yc, w[:ka], w[ka:ka + kb], w[ka + kb:])


def _mlp_kernel(h_ref, g_ref, wu_ref, wd_ref, o_ref, xn_ref, acc_ref):
    f = pl.program_id(1)

    @pl.when(f == 0)
    def _():
        xn_ref[...] = _rms(h_ref[...], g_ref[...]).astype(BF16)
        acc_ref[...] = h_ref[...]

    hid = jnp.dot(xn_ref[...], wu_ref[...], preferred_element_type=F32)
    hid = jnp.square(jnp.maximum(hid, 0.0)).astype(BF16)
    acc_ref[...] += jnp.dot(hid, wd_ref[...], preferred_element_type=F32)

    @pl.when(f == pl.num_programs(1) - 1)
    def _():
        o_ref[...] = acc_ref[...]


def _mlp(h2d, g, w_up, w_down):
    t, d = h2d.shape
    ff = w_up.shape[1]
    tm = min(t, 512)
    tf = min(ff, 512)
    return pl.pallas_call(
        _mlp_kernel,
        out_shape=jax.ShapeDtypeStruct((t, d), F32),
        grid=(t // tm, ff // tf),
        in_specs=[pl.BlockSpec((tm, d), lambda m, f: (m, 0)),
                  pl.BlockSpec((1, d), lambda m, f: (0, 0)),
                  pl.BlockSpec((d, tf), lambda m, f: (0, f)),
                  pl.BlockSpec((tf, d), lambda m, f: (f, 0))],
        out_specs=pl.BlockSpec((tm, d), lambda m, f: (m, 0)),
        scratch_shapes=[pltpu.VMEM((tm, d), BF16), pltpu.VMEM((tm, d), F32)],
        compiler_params=_cparams(("parallel", "arbitrary")),
        name="mlp_relu2",
    )(h2d, g.reshape(1, d), w_up, w_down)


def _ple_kernel(h_ref, g_ref, p_ref, wg_ref, wp_ref, fg_ref, o_ref, *, tn, final):
    h = h_ref[...]
    xn = _rms(h, g_ref[...]).astype(BF16)
    pb = p_ref[...]
    d = h.shape[1]
    for j in range(d // tn):
        sl = slice(j * tn, (j + 1) * tn)
        gate = jax.nn.sigmoid(jnp.dot(xn, wg_ref[:, sl], preferred_element_type=F32))
        proj = jnp.dot(pb, wp_ref[:, sl], preferred_element_type=F32)
        o_ref[:, sl] = h[:, sl] + gate * proj
    if final:
        o_ref[...] = _rms(o_ref[...], fg_ref[...])


def _ple(h2d, g, p2d, wg, wp, final_g, final):
    t, d = h2d.shape
    pd = p2d.shape[1]
    tm = min(t, 256)
    return pl.pallas_call(
        functools.partial(_ple_kernel, tn=min(d, 512), final=final),
        out_shape=jax.ShapeDtypeStruct((t, d), F32),
        grid=(t // tm,),
        in_specs=[pl.BlockSpec((tm, d), lambda m: (m, 0)),
                  pl.BlockSpec((1, d), lambda m: (0, 0)),
                  pl.BlockSpec((tm, pd), lambda m: (m, 0)),
                  pl.BlockSpec((d, d), lambda m: (0, 0)),
                  pl.BlockSpec((pd, d), lambda m: (0, 0)),
                  pl.BlockSpec((1, d), lambda m: (0, 0))],
        out_specs=pl.BlockSpec((tm, d), lambda m: (m, 0)),
        compiler_params=_cparams(("parallel",)),
        name="ple_gate",
    )(h2d, g.reshape(1, d), p2d, wg, wp, final_g.reshape(1, d))


SSM_CHUNK = 32
SSM_GROUP_BLOCK = 8


def kernel(x, p, positions, norm_mix_g, w_in, w_out, diff_lq1, diff_lk1, diff_lq2, diff_lk2,
           diff_subln_g, ssm_lambda_re, ssm_lambda_im, ssm_log_step, ssm_B_re, ssm_B_im,
           ssm_C_re, ssm_C_im, ssm_D, ssm_w_glu, norm_mlp_g, w_up, w_down, norm_ple_g,
           w_ple_gate, w_ple_proj, final_g):
    b, l, d = x.shape
    depth = w_in.shape[0]
    t = b * l
    tc = min(SSM_CHUNK, l)
    tabs = _rope_tables(positions)
    h = x.reshape(t, d)
    for i in range(depth):
        lambda_init = 0.8 - 0.6 * math.exp(-0.3 * i)
        z = _inproj(h, norm_mix_g[i], _permute_w_in(w_in[i]), tabs)
        z3 = z.reshape(b, l, Z_WIDTH)
        y_a = _diff_attn(z3, diff_lq1[i], diff_lk1[i], diff_lq2[i], diff_lk2[i],
                         diff_subln_g[i], lambda_init)
        y_b = _dsa_attn(z3)
        ops = _ssm_prep(ssm_lambda_re[i], ssm_lambda_im[i], ssm_log_step[i], ssm_B_re[i],
                        ssm_B_im[i], ssm_C_re[i], ssm_C_im[i], tc, SSM_GROUP_BLOCK)
        y_s = _s5(z3, ops, ssm_D[i], tc, SSM_GROUP_BLOCK)
        y_c = _glu(y_s.reshape(t, -1), ssm_w_glu[i].astype(BF16))
        h = _outproj(h, y_a.reshape(t, -1), y_b.reshape(t, -1), y_c, w_out[i].astype(BF16))
        h = _mlp(h, norm_mlp_g[i], w_up[i].astype(BF16), w_down[i].astype(BF16))
        h = _ple(h, norm_ple_g[i], p[i].reshape(t, -1).astype(BF16), w_ple_gate[i].astype(BF16),
                 w_ple_proj[i].astype(BF16), final_g, final=(i == depth - 1))
    return h.reshape(b, l, d)
```

```python
import functools
import math

import jax
import jax.numpy as jnp
from jax import lax
from jax.experimental import pallas as pl
from jax.experimental.pallas import tpu as pltpu

F32 = jnp.float32
BF16 = jnp.bfloat16
I32 = jnp.int32

LANES = 128
EPS = 1e-6
ROPE_THETA = 10000.0

DIFF_QK = 64
DIFF_V = 128
DIFF_HEADS = 4
DSA_DIM = 128
DSA_HEADS = 4
IDX_HEADS = 8
IDX_DIM = 64
TOPK_MAX = 256
SSM_C = 16
SSM_P = 64

COL_AQ, COL_AK, COL_IQ, COL_BQ, COL_AV = 0, 4, 8, 12, 16
COL_BK, COL_BV, COL_IKW, COL_CU = 20, 21, 22, 24
Z_WIDTH = 32 * LANES
IN_TN = 512

DIFF_TQ = 512
DIFF_TK = 512
DSA_TQ = 256
DSA_TK = 256
NEG_BIG = -1e30
LOG2E = math.log2(math.e)
VMEM_LIMIT = 56 * 1024 * 1024


def _cparams(sem):
    return pltpu.CompilerParams(dimension_semantics=sem, vmem_limit_bytes=VMEM_LIMIT)


def _rope_tables_kernel(pos_ref, f64_ref, f128_ref, c64_ref, s64_ref, c128_ref, s128_ref):
    pos = pos_ref[...]
    a64 = pos * f64_ref[...]
    a128 = pos * f128_ref[...]
    lane = lax.broadcasted_iota(I32, a64.shape, 1)
    c64_ref[...] = jnp.cos(a64)
    sn = jnp.sin(a64)
    s64_ref[...] = jnp.where((lane & 32) == 0, -sn, sn)
    c128_ref[...] = jnp.cos(a128)
    sn = jnp.sin(a128)
    s128_ref[...] = jnp.where((lane & 64) == 0, -sn, sn)


def _rope_tables(positions):
    t = positions.size
    pos = positions.reshape(t, 1).astype(F32)
    fr64 = ROPE_THETA ** (-jnp.arange(0, 64, 2, dtype=F32) / 64)
    fr128 = ROPE_THETA ** (-jnp.arange(0, 128, 2, dtype=F32) / 128)
    f64 = jnp.tile(fr64, 4).reshape(1, LANES)
    f128 = jnp.tile(fr128, 2).reshape(1, LANES)
    tm = min(t, 1024)
    tab = jax.ShapeDtypeStruct((t, LANES), F32)
    row = pl.BlockSpec((tm, LANES), lambda i: (i, 0))
    return pl.pallas_call(
        _rope_tables_kernel,
        out_shape=(tab, tab, tab, tab),
        grid=(t // tm,),
        in_specs=[pl.BlockSpec((tm, 1), lambda i: (i, 0)),
                  pl.BlockSpec((1, LANES), lambda i: (0, 0)),
                  pl.BlockSpec((1, LANES), lambda i: (0, 0))],
        out_specs=(row, row, row, row),
        compiler_params=_cparams(("parallel",)),
        name="rope_tables",
    )(pos, f64, f128)


def _rms(x, g):
    ms = jnp.mean(x * x, axis=-1, keepdims=True)
    return x * lax.rsqrt(ms + EPS) * g


def _swap_halves(a, half):
    lane = lax.broadcasted_iota(I32, a.shape, 1)
    return jnp.where((lane & half) == 0,
                     pltpu.roll(a, LANES - half, 1), pltpu.roll(a, half, 1))


def _inproj_kernel(x_ref, g_ref, w_ref, c64_ref, s64_ref, c128_ref, s128_ref, z_ref, xn_ref):
    n = pl.program_id(1)

    @pl.when(n == 0)
    def _():
        xn_ref[...] = _rms(x_ref[...], g_ref[...]).astype(BF16)

    acc = jnp.dot(xn_ref[...], w_ref[...], preferred_element_type=F32)
    groups = acc.shape[1] // LANES

    def rope(a, half):
        if half == 32:
            return a * c64_ref[...] + _swap_halves(a, 32) * s64_ref[...]
        return a * c128_ref[...] + _swap_halves(a, 64) * s128_ref[...]

    def grp(j):
        return acc[:, j * LANES:(j + 1) * LANES]

    def put(j, v):
        z_ref[:, j * LANES:(j + 1) * LANES] = v.astype(z_ref.dtype)

    @pl.when(n < 3)
    def _():
        for j in range(groups):
            put(j, rope(grp(j), 32))

    @pl.when(n == 3)
    def _():
        for j in range(groups):
            put(j, rope(grp(j), 64))

    @pl.when(n == 5)
    def _():
        put(0, rope(grp(0), 64))
        put(1, grp(1))
        a = grp(2)
        lane = lax.broadcasted_iota(I32, a.shape, 1)
        put(2, jnp.where(lane < IDX_DIM, rope(a, 32), a))
        put(3, grp(3))

    @pl.when((n == 4) | (n > 5))
    def _():
        z_ref[...] = acc.astype(z_ref.dtype)


def _inproj(h2d, g, w, tabs):
    t, d = h2d.shape
    tm = min(t, 512)
    tn = IN_TN
    c64, s64, c128, s128 = tabs
    tab = pl.BlockSpec((tm, LANES), lambda m, n: (m, 0))
    return pl.pallas_call(
        _inproj_kernel,
        out_shape=jax.ShapeDtypeStruct((t, Z_WIDTH), BF16),
        grid=(t // tm, Z_WIDTH // tn),
        in_specs=[pl.BlockSpec((tm, d), lambda m, n: (m, 0)),
                  pl.BlockSpec((1, d), lambda m, n: (0, 0)),
                  pl.BlockSpec((d, tn), lambda m, n: (0, n)),
                  tab, tab, tab, tab],
        out_specs=pl.BlockSpec((tm, tn), lambda m, n: (m, n)),
        scratch_shapes=[pltpu.VMEM((tm, d), BF16)],
        compiler_params=_cparams(("parallel", "arbitrary")),
        name="norm_inproj_rope",
    )(h2d, g.reshape(1, d), w, c64, s64, c128, s128)


def _permute_w_in(w):
    d = w.shape[0]
    pad = jnp.zeros((d, 56 + LANES), w.dtype)
    return jnp.concatenate(
        [w[:, 0:512], w[:, 512:1024], w[:, 2304:2816], w[:, 1536:2048], w[:, 1024:1536],
         w[:, 2048:2176], w[:, 2176:2304], w[:, 2816:2888], pad, w[:, 2888:3912]],
        axis=1).astype(BF16)


def _flash_step(carry, s, vb):
    m, l, acc = carry
    m_new = jnp.maximum(m, jnp.max(s, axis=1, keepdims=True))
    alpha = jnp.exp2(m - m_new)
    p = jnp.exp2(s - m_new)
    l = alpha * l + jnp.sum(p, axis=1, keepdims=True)
    acc = alpha * acc + jnp.dot(p.astype(BF16), vb, preferred_element_type=F32)
    return m_new, l, acc


def _flash_init(rows, dv):
    return (jnp.full((rows, 1), NEG_BIG, F32), jnp.zeros((rows, 1), F32),
            jnp.zeros((rows, dv), F32))


def _dot_nt(a, b):
    return lax.dot_general(a, b, (((1,), (1,)), ((), ())), preferred_element_type=F32)


def _diff_attn_kernel(q_ref, k_ref, vt_ref, lq1_ref, lk1_ref, lq2_ref, lk2_ref, g_ref, o_ref,
                      *, tq, tk, lambda_init):
    i = pl.program_id(2)
    q_t = jnp.transpose(q_ref[0].astype(F32)) * (DIFF_QK ** -0.5)
    dim = lax.broadcasted_iota(I32, q_t.shape, 0)
    q2 = jnp.concatenate([jnp.where(dim < DIFF_QK, q_t, 0.0),
                          jnp.where(dim >= DIFF_QK, q_t, 0.0)], axis=1).astype(BF16)

    def block(j, carry, diag):
        m, l, acc = carry
        kb = k_ref[0, pl.ds(j * tk, tk), :]
        vtb = vt_ref[0, 0, :, pl.ds(j * tk, tk)]
        s = jnp.dot(kb, q2, preferred_element_type=F32) * LOG2E
        if diag:
            key = j * tk + lax.broadcasted_iota(I32, s.shape, 0)
            qi = lax.broadcasted_iota(I32, s.shape, 1)
            qi = i * tq + jnp.where(qi >= tq, qi - tq, qi)
            s = jnp.where(key <= qi, s, NEG_BIG)
        m_new = jnp.maximum(m, jnp.max(s, axis=0, keepdims=True))
        alpha = jnp.exp2(m - m_new)
        p = jnp.exp2(s - m_new)
        l = alpha * l + jnp.sum(p, axis=0, keepdims=True)
        acc = alpha * acc + jnp.dot(vtb, p.astype(BF16), preferred_element_type=F32)
        return m_new, l, acc

    init = (jnp.full((1, 2 * tq), NEG_BIG, F32), jnp.zeros((1, 2 * tq), F32),
            jnp.zeros((DIFF_V, 2 * tq), F32))
    nfull = (i * tq) // tk
    carry = lax.fori_loop(0, nfull, lambda j, c: block(j, c, False), init)
    _, l, acc = block(nfull, carry, True)
    o = acc / l
    lam = (jnp.exp(jnp.sum(lq1_ref[...] * lk1_ref[...], keepdims=True))
           - jnp.exp(jnp.sum(lq2_ref[...] * lk2_ref[...], keepdims=True)) + lambda_init)
    out = jnp.transpose(o[:, :tq] - lam * o[:, tq:])
    out = _rms(out, g_ref[...]) * (1.0 - lambda_init)
    o_ref[0] = out.astype(o_ref.dtype)


def _diff_attn(z3, lq1, lk1, lq2, lk2, subln_g, lambda_init):
    b, l, _ = z3.shape
    tq = min(l, DIFF_TQ)
    tk = min(l, DIFF_TK)
    vec = pl.BlockSpec((1, DIFF_QK), lambda bi, h, i: (0, 0))
    v = z3[:, :, COL_AV * LANES:(COL_AV + DIFF_HEADS) * LANES]
    v_t = v.reshape(b, l, DIFF_HEADS, DIFF_V).transpose(0, 2, 3, 1)
    return pl.pallas_call(
        functools.partial(_diff_attn_kernel, tq=tq, tk=tk, lambda_init=lambda_init),
        out_shape=jax.ShapeDtypeStruct((b, l, DIFF_HEADS * DIFF_V), BF16),
        grid=(b, DIFF_HEADS, l // tq),
        in_specs=[pl.BlockSpec((1, tq, LANES), lambda bi, h, i: (bi, i, COL_AQ + h)),
                  pl.BlockSpec((1, l, LANES), lambda bi, h, i: (bi, 0, COL_AK + h)),
                  pl.BlockSpec((1, 1, DIFF_V, l), lambda bi, h, i: (bi, h, 0, 0)),
                  vec, vec, vec, vec,
                  pl.BlockSpec((1, DIFF_V), lambda bi, h, i: (0, 0))],
        out_specs=pl.BlockSpec((1, tq, DIFF_V), lambda bi, h, i: (bi, i, h)),
        compiler_params=_cparams(("parallel", "parallel", "arbitrary")),
        name="diff_attention",
    )(z3, z3, v_t, lq1.reshape(1, -1), lk1.reshape(1, -1), lq2.reshape(1, -1),
      lk2.reshape(1, -1), subln_g.reshape(1, -1))


def _sortable_key(s):
    bits = lax.bitcast_convert_type(s, I32)
    return bits ^ ((bits >> 31) & jnp.int32(0x7FFFFFFF))


def _dsa_kernel(bq_ref, iq_ref, iwq_ref, bk_ref, bvt_ref, ik_ref, o_ref, hi_ref, lo_ref,
                *, tq, tk, topk, seq_bits):
    i = pl.program_id(1)
    nkb = ((i + 1) * tq + tk - 1) // tk
    low16 = -(2 ** 15)
    i16 = jnp.int16

    def as16(v):
        return v.astype(i16)

    krow = lax.broadcasted_iota(I32, (tk, tq), 0)
    qcol = i * tq + lax.broadcasted_iota(I32, (tk, tq), 1)

    iq_t = jnp.transpose(iq_ref[0].astype(F32))
    iq_all = jnp.concatenate([iq_t[h * IDX_DIM:(h + 1) * IDX_DIM, :] for h in range(IDX_HEADS)],
                             axis=1).astype(BF16)
    iw_t = jnp.transpose(iwq_ref[0].astype(F32))[IDX_DIM:IDX_DIM + IDX_HEADS, :]
    iw_t = iw_t * (IDX_HEADS ** -0.5) * (IDX_DIM ** -0.5)

    def score_block(j, _):
        ikb = ik_ref[0, pl.ds(j * tk, tk), :][:, :IDX_DIM]
        r_all = jnp.dot(ikb, iq_all, preferred_element_type=F32)
        sc = jnp.zeros((tk, tq), F32)
        for h in range(IDX_HEADS):
            sc = sc + jnp.maximum(r_all[:, h * tq:(h + 1) * tq], 0.0) * iw_t[h:h + 1, :]
        sc = jnp.where(j * tk + krow <= qcol, sc + 0.0, -jnp.inf)
        key = _sortable_key(sc)
        hi_ref[j] = as16(key >> 16)
        lo_ref[j] = as16((key & 0xFFFF) + low16)
        return 0

    lax.fori_loop(0, nkb, score_block, 0)

    one16, zero16 = jnp.ones((), i16), jnp.zeros((), i16)

    def count(pred_fn):
        def blk(j, cnt):
            m = jnp.where(pred_fn(j), one16, zero16)
            parts = [m[r * 16:(r + 1) * 16, :] for r in range(tk // 16)]
            while len(parts) > 1:
                parts = [a + b for a, b in zip(parts[::2], parts[1::2])]
            return cnt + parts[0]
        cnt = lax.fori_loop(0, nkb, blk, jnp.zeros((16, tq), i16))
        return jnp.sum(cnt.astype(I32), axis=0, keepdims=True)

    def radix_select(ref, target):
        c0 = count(lambda j: ref[j] >= zero16)
        thr = jnp.where(c0 >= target, 0, low16).astype(I32)

        def bit_step(it, thr):
            cand = thr | (jnp.int32(1) << (14 - it))
            c16 = as16(cand)
            return jnp.where(count(lambda j: ref[j] >= c16) >= target, cand, thr)

        return lax.fori_loop(0, 15, bit_step, thr)

    t_hi = as16(radix_select(hi_ref, topk))
    need_lo = topk - count(lambda j: hi_ref[j] > t_hi)

    def mask_lo(j, _):
        lo_ref[j] = jnp.where(hi_ref[j] == t_hi, lo_ref[j], jnp.int16(low16))
        return 0

    lax.fori_loop(0, nkb, mask_lo, 0)
    t_lo = as16(radix_select(lo_ref, need_lo))

    def in_tie(j):
        return (hi_ref[j] == t_hi) & (lo_ref[j] == t_lo)

    need = need_lo - count(lambda j: lo_ref[j] > t_lo)
    has_excess = jnp.max(jnp.where(count(in_tie) > need, 1, 0)) > 0
    krow16 = as16(krow)

    def tie_limit():
        def step(it, q):
            cand = q + (jnp.int32(1) << (seq_bits - 1 - it))
            c16 = as16(cand)
            c = count(lambda j: in_tie(j) & (krow16 + as16(j * tk) < c16))
            return jnp.where(c < need, cand, q)
        return lax.fori_loop(0, seq_bits, step, jnp.zeros((1, tq), I32))

    jlim16 = as16(lax.cond(has_excess, tie_limit, lambda: jnp.full((1, tq), 2 ** seq_bits, I32)))

    q_t = jnp.transpose(bq_ref[0].astype(F32))
    q_all = jnp.concatenate([q_t[h * DSA_DIM:(h + 1) * DSA_DIM, :] for h in range(DSA_HEADS)],
                            axis=1).astype(BF16)
    scale = DSA_DIM ** -0.5 * LOG2E
    zero_b, neg_b = jnp.zeros((), BF16), jnp.full((), NEG_BIG, BF16)
    qcol16 = as16(qcol)

    def attn_block(j, carry):
        kb = bk_ref[0, pl.ds(j * tk, tk), :]
        vtb = bvt_ref[0, :, pl.ds(j * tk, tk)]
        hi, lo = hi_ref[j], lo_ref[j]
        kidx = krow16 + as16(j * tk)
        sel = (hi > t_hi) | ((hi == t_hi) & ((lo > t_lo) | ((lo == t_lo) & (kidx <= jlim16))))
        sel = sel & (kidx <= qcol16)
        bias = jnp.where(sel, zero_b, neg_b).astype(F32)
        m, l, acc = carry
        s = (jnp.dot(kb, q_all, preferred_element_type=F32) * scale
             + jnp.concatenate([bias] * DSA_HEADS, axis=1))
        m_new = jnp.maximum(m, jnp.max(s, axis=0, keepdims=True))
        alpha = jnp.exp2(m - m_new)
        p = jnp.exp2(s - m_new)
        l = alpha * l + jnp.sum(p, axis=0, keepdims=True)
        pb = p.astype(BF16)
        pv = jnp.concatenate(
            [jnp.dot(vtb, pb[:, h * tq:(h + 1) * tq], preferred_element_type=F32)
             for h in range(DSA_HEADS)], axis=1)
        return m_new, l, alpha * acc + pv

    wide_q = DSA_HEADS * tq
    init = (jnp.full((1, wide_q), NEG_BIG, F32), jnp.zeros((1, wide_q), F32),
            jnp.zeros((DSA_DIM, wide_q), F32))
    _, l, acc = lax.fori_loop(0, nkb, attn_block, init)
    o = acc / l
    for h in range(DSA_HEADS):
        o_ref[0, :, h * DSA_DIM:(h + 1) * DSA_DIM] = jnp.transpose(
            o[:, h * tq:(h + 1) * tq]).astype(o_ref.dtype)


def _dsa_attn(z3):
    b, l, _ = z3.shape
    tq = min(l, DSA_TQ)
    tk = min(l, DSA_TK)
    topk = min(TOPK_MAX, l // 4)
    seq_bits = max(1, (l - 1).bit_length())
    wide = 4
    v_t = jnp.swapaxes(z3[:, :, COL_BV * LANES:(COL_BV + 1) * LANES], 1, 2)
    return pl.pallas_call(
        functools.partial(_dsa_kernel, tq=tq, tk=tk, topk=topk, seq_bits=seq_bits),
        out_shape=jax.ShapeDtypeStruct((b, l, DSA_HEADS * DSA_DIM), BF16),
        grid=(b, l // tq),
        in_specs=[pl.BlockSpec((1, tq, 4 * LANES), lambda bi, i: (bi, i, COL_BQ // wide)),
                  pl.BlockSpec((1, tq, 4 * LANES), lambda bi, i: (bi, i, COL_IQ // wide)),
                  pl.BlockSpec((1, tq, LANES), lambda bi, i: (bi, i, COL_IKW)),
                  pl.BlockSpec((1, l, LANES), lambda bi, i: (bi, 0, COL_BK)),
                  pl.BlockSpec((1, DSA_DIM, l), lambda bi, i: (bi, 0, 0)),
                  pl.BlockSpec((1, l, LANES), lambda bi, i: (bi, 0, COL_IKW))],
        out_specs=pl.BlockSpec((1, tq, DSA_HEADS * DSA_DIM), lambda bi, i: (bi, i, 0)),
        scratch_shapes=[pltpu.VMEM((l // tk, tk, tq), jnp.int16),
                        pltpu.VMEM((l // tk, tk, tq), jnp.int16)],
        compiler_params=_cparams(("parallel", "arbitrary")),
        name="dsa_attention",
    )(z3, z3, z3, z3, v_t, z3)


def _cmul(ar, ai, br, bi):
    return ar * br - ai * bi, ar * bi + ai * br


def _ssm_prep_kernel(lre_ref, lim_ref, lstep_ref, btr_ref, bti_ref, cr_ref, ci_ref,
                     w1_ref, mt_ref, tzt_ref, at_ref, tz_ref, *, tc, gb):
    c = SSM_C
    p = SSM_P
    for gi in range(gb):
        lr = lre_ref[gi].reshape(1, 1, p)
        li = lim_ref[gi].reshape(1, 1, p)
        step = jnp.exp(lstep_ref[gi]).reshape(1, 1, 1)
        mag = jnp.exp(lr * step)
        a_re, a_im = mag * jnp.cos(li * step), mag * jnp.sin(li * step)
        den = lr * lr + li * li
        nr, ni = a_re - 1.0, a_im
        f_re, f_im = (nr * lr + ni * li) / den, (ni * lr - nr * li) / den
        bt_r, bt_i = btr_ref[gi][None], bti_ref[gi][None]
        bb_re = f_re * bt_r - f_im * bt_i
        bb_im = f_re * bt_i + f_im * bt_r
        fr, fi = jnp.ones_like(a_re), jnp.zeros_like(a_im)
        rr, ri = fr, fi
        pr, pi = a_re, a_im
        n = 1
        while n < tc:
            xr, xi = _cmul(fr, fi, pr, pi)
            fr, fi = jnp.concatenate([fr, xr], 0), jnp.concatenate([fi, xi], 0)
            xr, xi = _cmul(rr, ri, pr, pi)
            rr, ri = jnp.concatenate([xr, rr], 0), jnp.concatenate([xi, ri], 0)
            pr, pi = _cmul(pr, pi, pr, pi)
            n *= 2
        at_ref[gi] = jnp.concatenate([pr[0], pi[0]], axis=1)
        wr, wi = _cmul(rr, ri, bb_re, bb_im)
        w1 = jnp.concatenate([wr, wi], axis=2).reshape(tc * c, 2 * p)
        w1_ref[gi] = w1.astype(w1_ref.dtype)
        f1r, f1i = _cmul(fr, fi, a_re, a_im)
        c_re, c_im = cr_ref[gi][None], ci_ref[gi][None]
        mr, mi = _cmul(f1r, f1i, c_re, c_im)
        mt_ref[gi] = jnp.concatenate([mr, -mi], axis=2).reshape(tc * c, 2 * p).astype(mt_ref.dtype)
        er, ei = _cmul(fr, fi, c_re, c_im)
        e2 = jnp.concatenate([er, -ei], axis=2).reshape(tc * c, 2 * p)
        bcat = jnp.concatenate([bb_re[0], bb_im[0]], axis=1)
        kflat = lax.dot_general(e2, bcat, (((1,), (1,)), ((), ())),
                                precision=lax.Precision.HIGHEST,
                                preferred_element_type=F32)
        tz_ref[...] = jnp.zeros(tz_ref.shape, tz_ref.dtype)
        for s in range(tc):
            tz_ref[s * c:, s * c:(s + 1) * c] = kflat[:(tc - s) * c, :]
        tzt_ref[gi] = tz_ref[...].astype(tzt_ref.dtype)


def _ssm_prep(lam_re, lam_im, log_step, b_re, b_im, c_re, c_im, tc, gb):
    g, p = lam_re.shape
    c = SSM_C
    n = tc * c
    vecp = pl.BlockSpec((gb, 1, p), lambda i: (i, 0, 0))
    mat = pl.BlockSpec((gb, c, p), lambda i: (i, 0, 0))
    op = pl.BlockSpec((gb, n, 2 * p), lambda i: (i, 0, 0))
    return pl.pallas_call(
        functools.partial(_ssm_prep_kernel, tc=tc, gb=gb),
        out_shape=(jax.ShapeDtypeStruct((g, n, 2 * p), BF16),
                   jax.ShapeDtypeStruct((g, n, 2 * p), BF16),
                   jax.ShapeDtypeStruct((g, n, n), BF16),
                   jax.ShapeDtypeStruct((g, 1, 2 * p), F32)),
        grid=(g // gb,),
        in_specs=[vecp, vecp, pl.BlockSpec((gb, 1, 1), lambda i: (i, 0, 0)), mat, mat, mat, mat],
        out_specs=(op, op, pl.BlockSpec((gb, n, n), lambda i: (i, 0, 0)),
                   pl.BlockSpec((gb, 1, 2 * p), lambda i: (i, 0, 0))),
        scratch_shapes=[pltpu.VMEM((n, n), F32)],
        compiler_params=_cparams(("parallel",)),
        name="ssm_prep",
    )(lam_re.reshape(g, 1, p), lam_im.reshape(g, 1, p), log_step.reshape(g, 1, 1),
      jnp.swapaxes(b_re, 1, 2), jnp.swapaxes(b_im, 1, 2), c_re, c_im)


def _ssm_fused_kernel(u_ref, w1_ref, mt_ref, tzt_ref, at_ref, d_ref, y_ref,
                      uf_ref, ufl_ref, s_ref, s2_ref, x_ref, yfl_ref, yt_ref, *, tc, gb):
    c = SSM_C
    l = uf_ref.shape[0]
    nch = l // tc
    uf_ref[...] = u_ref[0].astype(F32)
    for s in range(tc):
        tile = uf_ref[pl.ds(s, nch, stride=tc), :]
        for g in range(gb):
            ufl_ref[g, :, s * c:(s + 1) * c] = tile[:, g * c:(g + 1) * c]
    for g in range(gb):
        sg = jnp.dot(ufl_ref[g].astype(BF16), w1_ref[g], preferred_element_type=F32)
        s_ref[pl.ds(g, nch, stride=gb), :] = sg
    a = at_ref[...].reshape(gb, 2 * SSM_P)
    lane = lax.broadcasted_iota(I32, a.shape, 1)
    a_sw = pltpu.roll(a, SSM_P, 1)
    a_re = jnp.where(lane < SSM_P, a, a_sw)
    a_im_s = jnp.where(lane < SSM_P, -a_sw, a)

    s2_ref[...] = pltpu.roll(s_ref[...], SSM_P, 1)

    def step(ci, carry):
        x1, x2 = carry
        r0 = pl.multiple_of(ci * gb, gb)
        x_ref[pl.ds(r0, gb), :] = x1
        n1 = x1 * a_re + x2 * a_im_s + s_ref[pl.ds(r0, gb), :]
        n2 = x2 * a_re - x1 * a_im_s + s2_ref[pl.ds(r0, gb), :]
        return n1, n2

    zero = jnp.zeros((gb, 2 * SSM_P), F32)
    lax.fori_loop(0, nch, step, (zero, zero))
    for g in range(gb):
        u = ufl_ref[g]
        xg = x_ref[pl.ds(g, nch, stride=gb), :]
        y = _dot_nt(u.astype(BF16), tzt_ref[g]) + _dot_nt(xg.astype(BF16), mt_ref[g])
        yfl_ref[g] = jax.nn.gelu(y + u * d_ref[g])
    for t in range(tc):
        tile = jnp.concatenate([yfl_ref[g, :, t * c:(t + 1) * c] for g in range(gb)], axis=1)
        yt_ref[pl.ds(t, nch, stride=tc), :] = tile
    y_ref[0] = yt_ref[...].astype(y_ref.dtype)


def _s5(z3, ops, d_skip, tc, gb):
    w1, mt, tzt, at = ops
    b, l, _ = z3.shape
    c = SSM_C
    g = w1.shape[0]
    nch = l // tc
    n = tc * c
    p2 = 2 * SSM_P
    wl = gb * c
    dt = jnp.tile(d_skip.reshape(g, 1, c), (1, 1, tc)).astype(F32)
    col0 = COL_CU * LANES // wl
    return pl.pallas_call(
        functools.partial(_ssm_fused_kernel, tc=tc, gb=gb),
        out_shape=jax.ShapeDtypeStruct((b, l, g * c), BF16),
        grid=(g // gb, b),
        in_specs=[pl.BlockSpec((1, l, wl), lambda gi, bi: (bi, 0, col0 + gi)),
                  pl.BlockSpec((gb, n, p2), lambda gi, bi: (gi, 0, 0)),
                  pl.BlockSpec((gb, n, p2), lambda gi, bi: (gi, 0, 0)),
                  pl.BlockSpec((gb, n, n), lambda gi, bi: (gi, 0, 0)),
                  pl.BlockSpec((gb, 1, p2), lambda gi, bi: (gi, 0, 0)),
                  pl.BlockSpec((gb, 1, n), lambda gi, bi: (gi, 0, 0))],
        out_specs=pl.BlockSpec((1, l, wl), lambda gi, bi: (bi, 0, gi)),
        scratch_shapes=[pltpu.VMEM((l, wl), F32),
                        pltpu.VMEM((gb, nch, n), F32),
                        pltpu.VMEM((nch * gb, p2), F32),
                        pltpu.VMEM((nch * gb, p2), F32),
                        pltpu.VMEM((nch * gb, p2), F32),
                        pltpu.VMEM((gb, nch, n), F32),
                        pltpu.VMEM((l, wl), F32)],
        compiler_params=_cparams(("parallel", "arbitrary")),
        name="ssm_fused",
    )(z3, w1, mt, tzt, at, dt)


def _glu_kernel(y_ref, yn_ref, w_ref, o_ref):
    gate = jnp.dot(y_ref[...], w_ref[...], preferred_element_type=F32)
    o_ref[...] = (yn_ref[...].astype(F32) * jax.nn.sigmoid(gate)).astype(o_ref.dtype)


def _glu(y2d, w):
    t, k = y2d.shape
    tm = min(t, 1024)
    tn = min(k, 512)
    return pl.pallas_call(
        _glu_kernel,
        out_shape=jax.ShapeDtypeStruct((t, k), BF16),
        grid=(t // tm, k // tn),
        in_specs=[pl.BlockSpec((tm, k), lambda m, n: (m, 0)),
                  pl.BlockSpec((tm, tn), lambda m, n: (m, n)),
                  pl.BlockSpec((k, tn), lambda m, n: (0, n))],
        out_specs=pl.BlockSpec((tm, tn), lambda m, n: (m, n)),
        compiler_params=_cparams(("parallel", "arbitrary")),
        name="ssm_glu",
    )(y2d, y2d, w)


def _outproj_kernel(h_ref, ya_ref, yb_ref, yc_ref, wa_ref, wb_ref, wc_ref, o_ref):
    acc = jnp.dot(ya_ref[...], wa_ref[...], preferred_element_type=F32)
    acc += jnp.dot(yb_ref[...], wb_ref[...], preferred_element_type=F32)
    acc += jnp.dot(yc_ref[...], wc_ref[...], preferred_element_type=F32)
    o_ref[...] = h_ref[...] + acc


def _outproj(h2d, ya, yb, yc, w):
    t, d = h2d.shape
    ka, kb, kc = ya.shape[1], yb.shape[1], yc.shape[1]
    tm = min(t, 512)
    tn = min(d, 512)
    return pl.pallas_call(
        _outproj_kernel,
        out_shape=jax.ShapeDtypeStruct((t, d), F32),
        grid=(t // tm, d // tn),
        in_specs=[pl.BlockSpec((tm, tn), lambda m, n: (m, n)),
                  pl.BlockSpec((tm, ka), lambda m, n: (m, 0)),
                  pl.BlockSpec((tm, kb), lambda m, n: (m, 0)),
                  pl.BlockSpec((tm, kc), lambda m, n: (m, 0)),
                  pl.BlockSpec((ka, tn), lambda m, n: (0, n)),
                  pl.BlockSpec((kb, tn), lambda m, n: (0, n)),
                  pl.BlockSpec((kc, tn), lambda m, n: (0, n))],
        out_specs=pl.BlockSpec((tm, tn), lambda m, n: (m, n)),
        compiler_params=_cparams(("parallel", "arbitrary")),
        name="out_proj",
    )(h2d, ya, yb, yc, w[:ka], w[ka:ka + kb], w[ka + kb:])


def _mlp_kernel(h_ref, g_ref, wu_ref, wd_ref, o_ref, xn_ref, acc_ref):
    f = pl.program_id(1)

    @pl.when(f == 0)
    def _():
        xn_ref[...] = _rms(h_ref[...], g_ref[...]).astype(BF16)
        acc_ref[...] = h_ref[...]

    hid = jnp.dot(xn_ref[...], wu_ref[...], preferred_element_type=F32)
    hid = jnp.square(jnp.maximum(hid, 0.0)).astype(BF16)
    acc_ref[...] += jnp.dot(hid, wd_ref[...], preferred_element_type=F32)

    @pl.when(f == pl.num_programs(1) - 1)
    def _():
        o_ref[...] = acc_ref[...]


def _mlp(h2d, g, w_up, w_down):
    t, d = h2d.shape
    ff = w_up.shape[1]
    tm = min(t, 512)
    tf = min(ff, 512)
    return pl.pallas_call(
        _mlp_kernel,
        out_shape=jax.ShapeDtypeStruct((t, d), F32),
        grid=(t // tm, ff // tf),
        in_specs=[pl.BlockSpec((tm, d), lambda m, f: (m, 0)),
                  pl.BlockSpec((1, d), lambda m, f: (0, 0)),
                  pl.BlockSpec((d, tf), lambda m, f: (0, f)),
                  pl.BlockSpec((tf, d), lambda m, f: (f, 0))],
        out_specs=pl.BlockSpec((tm, d), lambda m, f: (m, 0)),
        scratch_shapes=[pltpu.VMEM((tm, d), BF16), pltpu.VMEM((tm, d), F32)],
        compiler_params=_cparams(("parallel", "arbitrary")),
        name="mlp_relu2",
    )(h2d, g.reshape(1, d), w_up, w_down)


def _ple_kernel(h_ref, g_ref, p_ref, wg_ref, wp_ref, fg_ref, o_ref, *, tn, final):
    h = h_ref[...]
    xn = _rms(h, g_ref[...]).astype(BF16)
    pb = p_ref[...]
    d = h.shape[1]
    for j in range(d // tn):
        sl = slice(j * tn, (j + 1) * tn)
        gate = jax.nn.sigmoid(jnp.dot(xn, wg_ref[:, sl], preferred_element_type=F32))
        proj = jnp.dot(pb, wp_ref[:, sl], preferred_element_type=F32)
        o_ref[:, sl] = h[:, sl] + gate * proj
    if final:
        o_ref[...] = _rms(o_ref[...], fg_ref[...])


def _ple(h2d, g, p2d, wg, wp, final_g, final):
    t, d = h2d.shape
    pd = p2d.shape[1]
    tm = min(t, 256)
    return pl.pallas_call(
        functools.partial(_ple_kernel, tn=min(d, 512), final=final),
        out_shape=jax.ShapeDtypeStruct((t, d), F32),
        grid=(t // tm,),
        in_specs=[pl.BlockSpec((tm, d), lambda m: (m, 0)),
                  pl.BlockSpec((1, d), lambda m: (0, 0)),
                  pl.BlockSpec((tm, pd), lambda m: (m, 0)),
                  pl.BlockSpec((d, d), lambda m: (0, 0)),
                  pl.BlockSpec((pd, d), lambda m: (0, 0)),
                  pl.BlockSpec((1, d), lambda m: (0, 0))],
        out_specs=pl.BlockSpec((tm, d), lambda m: (m, 0)),
        compiler_params=_cparams(("parallel",)),
        name="ple_gate",
    )(h2d, g.reshape(1, d), p2d, wg, wp, final_g.reshape(1, d))


SSM_CHUNK = 32
SSM_GROUP_BLOCK = 8


def kernel(x, p, positions, norm_mix_g, w_in, w_out, diff_lq1, diff_lk1, diff_lq2, diff_lk2,
           diff_subln_g, ssm_lambda_re, ssm_lambda_im, ssm_log_step, ssm_B_re, ssm_B_im,
           ssm_C_re, ssm_C_im, ssm_D, ssm_w_glu, norm_mlp_g, w_up, w_down, norm_ple_g,
           w_ple_gate, w_ple_proj, final_g):
    b, l, d = x.shape
    depth = w_in.shape[0]
    t = b * l
    tc = min(SSM_CHUNK, l)
    tabs = _rope_tables(positions)
    h = x.reshape(t, d)
    for i in range(depth):
        lambda_init = 0.8 - 0.6 * math.exp(-0.3 * i)
        z = _inproj(h, norm_mix_g[i], _permute_w_in(w_in[i]), tabs)
        z3 = z.reshape(b, l, Z_WIDTH)
        y_a = _diff_attn(z3, diff_lq1[i], diff_lk1[i], diff_lq2[i], diff_lk2[i],
                         diff_subln_g[i], lambda_init)
        y_b = _dsa_attn(z3)
        ops = _ssm_prep(ssm_lambda_re[i], ssm_lambda_im[i], ssm_log_step[i], ssm_B_re[i],
                        ssm_B_im[i], ssm_C_re[i], ssm_C_im[i], tc, SSM_GROUP_BLOCK)
        y_s = _s5(z3, ops, ssm_D[i], tc, SSM_GROUP_BLOCK)
        y_c = _glu(y_s.reshape(t, -1), ssm_w_glu[i].astype(BF16))
        h = _outproj(h, y_a.reshape(t, -1), y_b.reshape(t, -1), y_c, w_out[i].astype(BF16))
        h = _mlp(h, norm_mlp_g[i], w_up[i].astype(BF16), w_down[i].astype(BF16))
        h = _ple(h, norm_ple_g[i], p[i].reshape(t, -1).astype(BF16), w_ple_gate[i].astype(BF16),
                 w_ple_proj[i].astype(BF16), final_g, final=(i == depth - 1))
    return h.reshape(b, l, d)
```

```python
import functools
import math

import jax
import jax.numpy as jnp
from jax import lax
from jax.experimental import pallas as pl
from jax.experimental.pallas import tpu as pltpu

F32 = jnp.float32
BF16 = jnp.bfloat16
I32 = jnp.int32

LANES = 128
EPS = 1e-6
ROPE_THETA = 10000.0

DIFF_QK = 64
DIFF_V = 128
DIFF_HEADS = 4
DSA_DIM = 128
DSA_HEADS = 4
IDX_HEADS = 8
IDX_DIM = 64
TOPK_MAX = 256
SSM_C = 16
SSM_P = 64

COL_AQ, COL_AK, COL_IQ, COL_BQ, COL_AV = 0, 4, 8, 12, 16
COL_BK, COL_BV, COL_IKW, COL_CU = 20, 21, 22, 24
Z_WIDTH = 32 * LANES
IN_TN = 512

DIFF_TQ = 512
DIFF_TK = 512
DSA_TQ = 256
DSA_TK = 512
NEG_BIG = -1e30
LOG2E = math.log2(math.e)
VMEM_LIMIT = 56 * 1024 * 1024


def _cparams(sem):
    return pltpu.CompilerParams(dimension_semantics=sem, vmem_limit_bytes=VMEM_LIMIT)


def _rope_tables_kernel(pos_ref, f64_ref, f128_ref, c64_ref, s64_ref, c128_ref, s128_ref):
    pos = pos_ref[...]
    a64 = pos * f64_ref[...]
    a128 = pos * f128_ref[...]
    lane = lax.broadcasted_iota(I32, a64.shape, 1)
    c64_ref[...] = jnp.cos(a64)
    sn = jnp.sin(a64)
    s64_ref[...] = jnp.where((lane & 32) == 0, -sn, sn)
    c128_ref[...] = jnp.cos(a128)
    sn = jnp.sin(a128)
    s128_ref[...] = jnp.where((lane & 64) == 0, -sn, sn)


def _rope_tables(positions):
    t = positions.size
    pos = positions.reshape(t, 1).astype(F32)
    fr64 = ROPE_THETA ** (-jnp.arange(0, 64, 2, dtype=F32) / 64)
    fr128 = ROPE_THETA ** (-jnp.arange(0, 128, 2, dtype=F32) / 128)
    f64 = jnp.tile(fr64, 4).reshape(1, LANES)
    f128 = jnp.tile(fr128, 2).reshape(1, LANES)
    tm = min(t, 1024)
    tab = jax.ShapeDtypeStruct((t, LANES), F32)
    row = pl.BlockSpec((tm, LANES), lambda i: (i, 0))
    return pl.pallas_call(
        _rope_tables_kernel,
        out_shape=(tab, tab, tab, tab),
        grid=(t // tm,),
        in_specs=[pl.BlockSpec((tm, 1), lambda i: (i, 0)),
                  pl.BlockSpec((1, LANES), lambda i: (0, 0)),
                  pl.BlockSpec((1, LANES), lambda i: (0, 0))],
        out_specs=(row, row, row, row),
        compiler_params=_cparams(("parallel",)),
        name="rope_tables",
    )(pos, f64, f128)


def _rms(x, g):
    ms = jnp.mean(x * x, axis=-1, keepdims=True)
    return x * lax.rsqrt(ms + EPS) * g


def _swap_halves(a, half):
    lane = lax.broadcasted_iota(I32, a.shape, 1)
    return jnp.where((lane & half) == 0,
                     pltpu.roll(a, LANES - half, 1), pltpu.roll(a, half, 1))


def _inproj_kernel(x_ref, g_ref, w_ref, c64_ref, s64_ref, c128_ref, s128_ref, z_ref, xn_ref):
    n = pl.program_id(1)

    @pl.when(n == 0)
    def _():
        xn_ref[...] = _rms(x_ref[...], g_ref[...]).astype(BF16)

    acc = jnp.dot(xn_ref[...], w_ref[...], preferred_element_type=F32)
    groups = acc.shape[1] // LANES

    def rope(a, half):
        if half == 32:
            return a * c64_ref[...] + _swap_halves(a, 32) * s64_ref[...]
        return a * c128_ref[...] + _swap_halves(a, 64) * s128_ref[...]

    def grp(j):
        return acc[:, j * LANES:(j + 1) * LANES]

    def put(j, v):
        z_ref[:, j * LANES:(j + 1) * LANES] = v.astype(z_ref.dtype)

    @pl.when(n < 3)
    def _():
        for j in range(groups):
            put(j, rope(grp(j), 32))

    @pl.when(n == 3)
    def _():
        for j in range(groups):
            put(j, rope(grp(j), 64))

    @pl.when(n == 5)
    def _():
        put(0, rope(grp(0), 64))
        put(1, grp(1))
        a = grp(2)
        lane = lax.broadcasted_iota(I32, a.shape, 1)
        put(2, jnp.where(lane < IDX_DIM, rope(a, 32), a))
        put(3, grp(3))

    @pl.when((n == 4) | (n > 5))
    def _():
        z_ref[...] = acc.astype(z_ref.dtype)


def _inproj(h2d, g, w, tabs):
    t, d = h2d.shape
    tm = min(t, 512)
    tn = IN_TN
    c64, s64, c128, s128 = tabs
    tab = pl.BlockSpec((tm, LANES), lambda m, n: (m, 0))
    return pl.pallas_call(
        _inproj_kernel,
        out_shape=jax.ShapeDtypeStruct((t, Z_WIDTH), BF16),
        grid=(t // tm, Z_WIDTH // tn),
        in_specs=[pl.BlockSpec((tm, d), lambda m, n: (m, 0)),
                  pl.BlockSpec((1, d), lambda m, n: (0, 0)),
                  pl.BlockSpec((d, tn), lambda m, n: (0, n)),
                  tab, tab, tab, tab],
        out_specs=pl.BlockSpec((tm, tn), lambda m, n: (m, n)),
        scratch_shapes=[pltpu.VMEM((tm, d), BF16)],
        compiler_params=_cparams(("parallel", "arbitrary")),
        name="norm_inproj_rope",
    )(h2d, g.reshape(1, d), w, c64, s64, c128, s128)


def _permute_w_in(w):
    d = w.shape[0]
    pad = jnp.zeros((d, 56 + LANES), w.dtype)
    return jnp.concatenate(
        [w[:, 0:512], w[:, 512:1024], w[:, 2304:2816], w[:, 1536:2048], w[:, 1024:1536],
         w[:, 2048:2176], w[:, 2176:2304], w[:, 2816:2888], pad, w[:, 2888:3912]],
        axis=1).astype(BF16)


def _flash_step(carry, s, vb):
    m, l, acc = carry
    m_new = jnp.maximum(m, jnp.max(s, axis=1, keepdims=True))
    alpha = jnp.exp2(m - m_new)
    p = jnp.exp2(s - m_new)
    l = alpha * l + jnp.sum(p, axis=1, keepdims=True)
    acc = alpha * acc + jnp.dot(p.astype(BF16), vb, preferred_element_type=F32)
    return m_new, l, acc


def _flash_init(rows, dv):
    return (jnp.full((rows, 1), NEG_BIG, F32), jnp.zeros((rows, 1), F32),
            jnp.zeros((rows, dv), F32))


def _dot_nt(a, b):
    return lax.dot_general(a, b, (((1,), (1,)), ((), ())), preferred_element_type=F32)


def _diff_attn_kernel(q_ref, k_ref, vt_ref, lq1_ref, lk1_ref, lq2_ref, lk2_ref, g_ref, o_ref,
                      *, tq, tk, lambda_init):
    i = pl.program_id(2)
    q_t = jnp.transpose(q_ref[0].astype(F32)) * (DIFF_QK ** -0.5)
    dim = lax.broadcasted_iota(I32, q_t.shape, 0)
    q2 = jnp.concatenate([jnp.where(dim < DIFF_QK, q_t, 0.0),
                          jnp.where(dim >= DIFF_QK, q_t, 0.0)], axis=1).astype(BF16)

    def block(j, carry, diag):
        m, l, acc = carry
        kb = k_ref[0, pl.ds(j * tk, tk), :]
        vtb = vt_ref[0, 0, :, pl.ds(j * tk, tk)]
        s = jnp.dot(kb, q2, preferred_element_type=F32) * LOG2E
        if diag:
            key = j * tk + lax.broadcasted_iota(I32, s.shape, 0)
            qi = lax.broadcasted_iota(I32, s.shape, 1)
            qi = i * tq + jnp.where(qi >= tq, qi - tq, qi)
            s = jnp.where(key <= qi, s, NEG_BIG)
        m_new = jnp.maximum(m, jnp.max(s, axis=0, keepdims=True))
        alpha = jnp.exp2(m - m_new)
        p = jnp.exp2(s - m_new)
        l = alpha * l + jnp.sum(p, axis=0, keepdims=True)
        acc = alpha * acc + jnp.dot(vtb, p.astype(BF16), preferred_element_type=F32)
        return m_new, l, acc

    init = (jnp.full((1, 2 * tq), NEG_BIG, F32), jnp.zeros((1, 2 * tq), F32),
            jnp.zeros((DIFF_V, 2 * tq), F32))
    nfull = (i * tq) // tk
    carry = lax.fori_loop(0, nfull, lambda j, c: block(j, c, False), init)
    _, l, acc = block(nfull, carry, True)
    o = acc / l
    lam = (jnp.exp(jnp.sum(lq1_ref[...] * lk1_ref[...], keepdims=True))
           - jnp.exp(jnp.sum(lq2_ref[...] * lk2_ref[...], keepdims=True)) + lambda_init)
    out = jnp.transpose(o[:, :tq] - lam * o[:, tq:])
    out = _rms(out, g_ref[...]) * (1.0 - lambda_init)
    o_ref[0] = out.astype(o_ref.dtype)


def _diff_attn(z3, lq1, lk1, lq2, lk2, subln_g, lambda_init):
    b, l, _ = z3.shape
    tq = min(l, DIFF_TQ)
    tk = min(l, DIFF_TK)
    vec = pl.BlockSpec((1, DIFF_QK), lambda bi, h, i: (0, 0))
    v = z3[:, :, COL_AV * LANES:(COL_AV + DIFF_HEADS) * LANES]
    v_t = v.reshape(b, l, DIFF_HEADS, DIFF_V).transpose(0, 2, 3, 1)
    return pl.pallas_call(
        functools.partial(_diff_attn_kernel, tq=tq, tk=tk, lambda_init=lambda_init),
        out_shape=jax.ShapeDtypeStruct((b, l, DIFF_HEADS * DIFF_V), BF16),
        grid=(b, DIFF_HEADS, l // tq),
        in_specs=[pl.BlockSpec((1, tq, LANES), lambda bi, h, i: (bi, i, COL_AQ + h)),
                  pl.BlockSpec((1, l, LANES), lambda bi, h, i: (bi, 0, COL_AK + h)),
                  pl.BlockSpec((1, 1, DIFF_V, l), lambda bi, h, i: (bi, h, 0, 0)),
                  vec, vec, vec, vec,
                  pl.BlockSpec((1, DIFF_V), lambda bi, h, i: (0, 0))],
        out_specs=pl.BlockSpec((1, tq, DIFF_V), lambda bi, h, i: (bi, i, h)),
        compiler_params=_cparams(("parallel", "parallel", "arbitrary")),
        name="diff_attention",
    )(z3, z3, v_t, lq1.reshape(1, -1), lk1.reshape(1, -1), lq2.reshape(1, -1),
      lk2.reshape(1, -1), subln_g.reshape(1, -1))


def _sortable_key(s):
    bits = lax.bitcast_convert_type(s, I32)
    return bits ^ ((bits >> 31) & jnp.int32(0x7FFFFFFF))


def _dsa_kernel(bq_ref, iq_ref, iwq_ref, bk_ref, bvt_ref, ik_ref, o_ref, hi_ref, lo_ref,
                *, tq, tk, topk, seq_bits):
    i = pl.program_id(1)
    nkb = ((i + 1) * tq + tk - 1) // tk
    low16 = -(2 ** 15)
    i16 = jnp.int16

    def as16(v):
        return v.astype(i16)

    krow = lax.broadcasted_iota(I32, (tk, tq), 0)
    qcol = i * tq + lax.broadcasted_iota(I32, (tk, tq), 1)

    iq_t = jnp.transpose(iq_ref[0].astype(F32))
    iq_all = jnp.concatenate([iq_t[h * IDX_DIM:(h + 1) * IDX_DIM, :] for h in range(IDX_HEADS)],
                             axis=1).astype(BF16)
    iw_t = jnp.transpose(iwq_ref[0].astype(F32))[IDX_DIM:IDX_DIM + IDX_HEADS, :]
    iw_t = iw_t * (IDX_HEADS ** -0.5) * (IDX_DIM ** -0.5)

    def score_block(j, _):
        ikb = ik_ref[0, pl.ds(j * tk, tk), :][:, :IDX_DIM]
        r_all = jnp.dot(ikb, iq_all, preferred_element_type=F32)
        sc = jnp.zeros((tk, tq), F32)
        for h in range(IDX_HEADS):
            sc = sc + jnp.maximum(r_all[:, h * tq:(h + 1) * tq], 0.0) * iw_t[h:h + 1, :]
        sc = jnp.where(j * tk + krow <= qcol, sc + 0.0, -jnp.inf)
        key = _sortable_key(sc)
        hi_ref[j] = as16(key >> 16)
        lo_ref[j] = as16((key & 0xFFFF) + low16)
        return 0

    lax.fori_loop(0, nkb, score_block, 0)

    one16, zero16 = jnp.ones((), i16), jnp.zeros((), i16)

    def count(pred_fn):
        def blk(j, cnt):
            m = jnp.where(pred_fn(j), one16, zero16)
            parts = [m[r * 16:(r + 1) * 16, :] for r in range(tk // 16)]
            while len(parts) > 1:
                parts = [a + b for a, b in zip(parts[::2], parts[1::2])]
            return cnt + parts[0]
        cnt = lax.fori_loop(0, nkb, blk, jnp.zeros((16, tq), i16))
        return jnp.sum(cnt.astype(I32), axis=0, keepdims=True)

    def radix_select(ref, target):
        c0 = count(lambda j: ref[j] >= zero16)
        thr = jnp.where(c0 >= target, 0, low16).astype(I32)

        def bit_step(it, thr):
            cand = thr | (jnp.int32(1) << (14 - it))
            c16 = as16(cand)
            return jnp.where(count(lambda j: ref[j] >= c16) >= target, cand, thr)

        return lax.fori_loop(0, 15, bit_step, thr)

    t_hi = as16(radix_select(hi_ref, topk))
    need_lo = topk - count(lambda j: hi_ref[j] > t_hi)

    def mask_lo(j, _):
        lo_ref[j] = jnp.where(hi_ref[j] == t_hi, lo_ref[j], jnp.int16(low16))
        return 0

    lax.fori_loop(0, nkb, mask_lo, 0)
    t_lo = as16(radix_select(lo_ref, need_lo))

    def in_tie(j):
        return (hi_ref[j] == t_hi) & (lo_ref[j] == t_lo)

    need = need_lo - count(lambda j: lo_ref[j] > t_lo)
    has_excess = jnp.max(jnp.where(count(in_tie) > need, 1, 0)) > 0
    krow16 = as16(krow)

    def tie_limit():
        def step(it, q):
            cand = q + (jnp.int32(1) << (seq_bits - 1 - it))
            c16 = as16(cand)
            c = count(lambda j: in_tie(j) & (krow16 + as16(j * tk) < c16))
            return jnp.where(c < need, cand, q)
        return lax.fori_loop(0, seq_bits, step, jnp.zeros((1, tq), I32))

    jlim16 = as16(lax.cond(has_excess, tie_limit, lambda: jnp.full((1, tq), 2 ** seq_bits, I32)))

    q_t = jnp.transpose(bq_ref[0].astype(F32))
    q_all = jnp.concatenate([q_t[h * DSA_DIM:(h + 1) * DSA_DIM, :] for h in range(DSA_HEADS)],
                            axis=1).astype(BF16)
    scale = DSA_DIM ** -0.5 * LOG2E
    zero_b, neg_b = jnp.zeros((), BF16), jnp.full((), NEG_BIG, BF16)
    qcol16 = as16(qcol)

    def attn_block(j, carry):
        kb = bk_ref[0, pl.ds(j * tk, tk), :]
        vtb = bvt_ref[0, :, pl.ds(j * tk, tk)]
        hi, lo = hi_ref[j], lo_ref[j]
        kidx = krow16 + as16(j * tk)
        sel = (hi > t_hi) | ((hi == t_hi) & ((lo > t_lo) | ((lo == t_lo) & (kidx <= jlim16))))
        sel = sel & (kidx <= qcol16)
        bias = jnp.where(sel, zero_b, neg_b).astype(F32)
        m, l, acc = carry
        s = (jnp.dot(kb, q_all, preferred_element_type=F32) * scale
             + jnp.concatenate([bias] * DSA_HEADS, axis=1))
        m_new = jnp.maximum(m, jnp.max(s, axis=0, keepdims=True))
        alpha = jnp.exp2(m - m_new)
        p = jnp.exp2(s - m_new)
        l = alpha * l + jnp.sum(p, axis=0, keepdims=True)
        pb = p.astype(BF16)
        pv = jnp.concatenate(
            [jnp.dot(vtb, pb[:, h * tq:(h + 1) * tq], preferred_element_type=F32)
             for h in range(DSA_HEADS)], axis=1)
        return m_new, l, alpha * acc + pv

    wide_q = DSA_HEADS * tq
    init = (jnp.full((1, wide_q), NEG_BIG, F32), jnp.zeros((1, wide_q), F32),
            jnp.zeros((DSA_DIM, wide_q), F32))
    _, l, acc = lax.fori_loop(0, nkb, attn_block, init)
    o = acc / l
    for h in range(DSA_HEADS):
        o_ref[0, :, h * DSA_DIM:(h + 1) * DSA_DIM] = jnp.transpose(
            o[:, h * tq:(h + 1) * tq]).astype(o_ref.dtype)


def _dsa_attn(z3):
    b, l, _ = z3.shape
    tq = min(l, DSA_TQ)
    tk = min(l, DSA_TK)
    topk = min(TOPK_MAX, l // 4)
    seq_bits = max(1, (l - 1).bit_length())
    wide = 4
    v_t = jnp.swapaxes(z3[:, :, COL_BV * LANES:(COL_BV + 1) * LANES], 1, 2)
    return pl.pallas_call(
        functools.partial(_dsa_kernel, tq=tq, tk=tk, topk=topk, seq_bits=seq_bits),
        out_shape=jax.ShapeDtypeStruct((b, l, DSA_HEADS * DSA_DIM), BF16),
        grid=(b, l // tq),
        in_specs=[pl.BlockSpec((1, tq, 4 * LANES), lambda bi, i: (bi, i, COL_BQ // wide)),
                  pl.BlockSpec((1, tq, 4 * LANES), lambda bi, i: (bi, i, COL_IQ // wide)),
                  pl.BlockSpec((1, tq, LANES), lambda bi, i: (bi, i, COL_IKW)),
                  pl.BlockSpec((1, l, LANES), lambda bi, i: (bi, 0, COL_BK)),
                  pl.BlockSpec((1, DSA_DIM, l), lambda bi, i: (bi, 0, 0)),
                  pl.BlockSpec((1, l, LANES), lambda bi, i: (bi, 0, COL_IKW))],
        out_specs=pl.BlockSpec((1, tq, DSA_HEADS * DSA_DIM), lambda bi, i: (bi, i, 0)),
        scratch_shapes=[pltpu.VMEM((l // tk, tk, tq), jnp.int16),
                        pltpu.VMEM((l // tk, tk, tq), jnp.int16)],
        compiler_params=_cparams(("parallel", "arbitrary")),
        name="dsa_attention",
    )(z3, z3, z3, z3, v_t, z3)


def _cmul(ar, ai, br, bi):
    return ar * br - ai * bi, ar * bi + ai * br


def _ssm_prep_kernel(lre_ref, lim_ref, lstep_ref, btr_ref, bti_ref, cr_ref, ci_ref,
                     w1_ref, mt_ref, tzt_ref, at_ref, tz_ref, *, tc, gb):
    c = SSM_C
    p = SSM_P
    for gi in range(gb):
        lr = lre_ref[gi].reshape(1, 1, p)
        li = lim_ref[gi].reshape(1, 1, p)
        step = jnp.exp(lstep_ref[gi]).reshape(1, 1, 1)
        mag = jnp.exp(lr * step)
        a_re, a_im = mag * jnp.cos(li * step), mag * jnp.sin(li * step)
        den = lr * lr + li * li
        nr, ni = a_re - 1.0, a_im
        f_re, f_im = (nr * lr + ni * li) / den, (ni * lr - nr * li) / den
        bt_r, bt_i = btr_ref[gi][None], bti_ref[gi][None]
        bb_re = f_re * bt_r - f_im * bt_i
        bb_im = f_re * bt_i + f_im * bt_r
        fr, fi = jnp.ones_like(a_re), jnp.zeros_like(a_im)
        rr, ri = fr, fi
        pr, pi = a_re, a_im
        n = 1
        while n < tc:
            xr, xi = _cmul(fr, fi, pr, pi)
            fr, fi = jnp.concatenate([fr, xr], 0), jnp.concatenate([fi, xi], 0)
            xr, xi = _cmul(rr, ri, pr, pi)
            rr, ri = jnp.concatenate([xr, rr], 0), jnp.concatenate([xi, ri], 0)
            pr, pi = _cmul(pr, pi, pr, pi)
            n *= 2
        at_ref[gi] = jnp.concatenate([pr[0], pi[0]], axis=1)
        wr, wi = _cmul(rr, ri, bb_re, bb_im)
        w1 = jnp.concatenate([wr, wi], axis=2).reshape(tc * c, 2 * p)
        w1_ref[gi] = w1.astype(w1_ref.dtype)
        f1r, f1i = _cmul(fr, fi, a_re, a_im)
        c_re, c_im = cr_ref[gi][None], ci_ref[gi][None]
        mr, mi = _cmul(f1r, f1i, c_re, c_im)
        mt_ref[gi] = jnp.concatenate([mr, -mi], axis=2).reshape(tc * c, 2 * p).astype(mt_ref.dtype)
        er, ei = _cmul(fr, fi, c_re, c_im)
        e2 = jnp.concatenate([er, -ei], axis=2).reshape(tc * c, 2 * p)
        bcat = jnp.concatenate([bb_re[0], bb_im[0]], axis=1)
        kflat = lax.dot_general(e2, bcat, (((1,), (1,)), ((), ())),
                                precision=lax.Precision.HIGHEST,
                                preferred_element_type=F32)
        tz_ref[...] = jnp.zeros(tz_ref.shape, tz_ref.dtype)
        for s in range(tc):
            tz_ref[s * c:, s * c:(s + 1) * c] = kflat[:(tc - s) * c, :]
        tzt_ref[gi] = tz_ref[...].astype(tzt_ref.dtype)


def _ssm_prep(lam_re, lam_im, log_step, b_re, b_im, c_re, c_im, tc, gb):
    g, p = lam_re.shape
    c = SSM_C
    n = tc * c
    vecp = pl.BlockSpec((gb, 1, p), lambda i: (i, 0, 0))
    mat = pl.BlockSpec((gb, c, p), lambda i: (i, 0, 0))
    op = pl.BlockSpec((gb, n, 2 * p), lambda i: (i, 0, 0))
    return pl.pallas_call(
        functools.partial(_ssm_prep_kernel, tc=tc, gb=gb),
        out_shape=(jax.ShapeDtypeStruct((g, n, 2 * p), BF16),
                   jax.ShapeDtypeStruct((g, n, 2 * p), BF16),
                   jax.ShapeDtypeStruct((g, n, n), BF16),
                   jax.ShapeDtypeStruct((g, 1, 2 * p), F32)),
        grid=(g // gb,),
        in_specs=[vecp, vecp, pl.BlockSpec((gb, 1, 1), lambda i: (i, 0, 0)), mat, mat, mat, mat],
        out_specs=(op, op, pl.BlockSpec((gb, n, n), lambda i: (i, 0, 0)),
                   pl.BlockSpec((gb, 1, 2 * p), lambda i: (i, 0, 0))),
        scratch_shapes=[pltpu.VMEM((n, n), F32)],
        compiler_params=_cparams(("parallel",)),
        name="ssm_prep",
    )(lam_re.reshape(g, 1, p), lam_im.reshape(g, 1, p), log_step.reshape(g, 1, 1),
      jnp.swapaxes(b_re, 1, 2), jnp.swapaxes(b_im, 1, 2), c_re, c_im)


def _ssm_fused_kernel(u_ref, w1_ref, mt_ref, tzt_ref, at_ref, d_ref, y_ref,
                      uf_ref, ufl_ref, s_ref, s2_ref, x_ref, yfl_ref, yt_ref, *, tc, gb):
    c = SSM_C
    l = uf_ref.shape[0]
    nch = l // tc
    uf_ref[...] = u_ref[0].astype(F32)
    for s in range(tc):
        tile = uf_ref[pl.ds(s, nch, stride=tc), :]
        for g in range(gb):
            ufl_ref[g, :, s * c:(s + 1) * c] = tile[:, g * c:(g + 1) * c]
    for g in range(gb):
        sg = jnp.dot(ufl_ref[g].astype(BF16), w1_ref[g], preferred_element_type=F32)
        s_ref[pl.ds(g, nch, stride=gb), :] = sg
    a = at_ref[...].reshape(gb, 2 * SSM_P)
    lane = lax.broadcasted_iota(I32, a.shape, 1)
    a_sw = pltpu.roll(a, SSM_P, 1)
    a_re = jnp.where(lane < SSM_P, a, a_sw)
    a_im_s = jnp.where(lane < SSM_P, -a_sw, a)

    s2_ref[...] = pltpu.roll(s_ref[...], SSM_P, 1)

    def step(ci, carry):
        x1, x2 = carry
        r0 = pl.multiple_of(ci * gb, gb)
        x_ref[pl.ds(r0, gb), :] = x1
        n1 = x1 * a_re + x2 * a_im_s + s_ref[pl.ds(r0, gb), :]
        n2 = x2 * a_re - x1 * a_im_s + s2_ref[pl.ds(r0, gb), :]
        return n1, n2

    zero = jnp.zeros((gb, 2 * SSM_P), F32)
    lax.fori_loop(0, nch, step, (zero, zero))
    for g in range(gb):
        u = ufl_ref[g]
        xg = x_ref[pl.ds(g, nch, stride=gb), :]
        y = _dot_nt(u.astype(BF16), tzt_ref[g]) + _dot_nt(xg.astype(BF16), mt_ref[g])
        yfl_ref[g] = jax.nn.gelu(y + u * d_ref[g])
    for t in range(tc):
        tile = jnp.concatenate([yfl_ref[g, :, t * c:(t + 1) * c] for g in range(gb)], axis=1)
        yt_ref[pl.ds(t, nch, stride=tc), :] = tile
    y_ref[0] = yt_ref[...].astype(y_ref.dtype)


def _s5(z3, ops, d_skip, tc, gb):
    w1, mt, tzt, at = ops
    b, l, _ = z3.shape
    c = SSM_C
    g = w1.shape[0]
    nch = l // tc
    n = tc * c
    p2 = 2 * SSM_P
    wl = gb * c
    dt = jnp.tile(d_skip.reshape(g, 1, c), (1, 1, tc)).astype(F32)
    col0 = COL_CU * LANES // wl
    return pl.pallas_call(
        functools.partial(_ssm_fused_kernel, tc=tc, gb=gb),
        out_shape=jax.ShapeDtypeStruct((b, l, g * c), BF16),
        grid=(g // gb, b),
        in_specs=[pl.BlockSpec((1, l, wl), lambda gi, bi: (bi, 0, col0 + gi)),
                  pl.BlockSpec((gb, n, p2), lambda gi, bi: (gi, 0, 0)),
                  pl.BlockSpec((gb, n, p2), lambda gi, bi: (gi, 0, 0)),
                  pl.BlockSpec((gb, n, n), lambda gi, bi: (gi, 0, 0)),
                  pl.BlockSpec((gb, 1, p2), lambda gi, bi: (gi, 0, 0)),
                  pl.BlockSpec((gb, 1, n), lambda gi, bi: (gi, 0, 0))],
        out_specs=pl.BlockSpec((1, l, wl), lambda gi, bi: (bi, 0, gi)),
        scratch_shapes=[pltpu.VMEM((l, wl), F32),
                        pltpu.VMEM((gb, nch, n), F32),
                        pltpu.VMEM((nch * gb, p2), F32),
                        pltpu.VMEM((nch * gb, p2), F32),
                        pltpu.VMEM((nch * gb, p2), F32),
                        pltpu.VMEM((gb, nch, n), F32),
                        pltpu.VMEM((l, wl), F32)],
        compiler_params=_cparams(("parallel", "arbitrary")),
        name="ssm_fused",
    )(z3, w1, mt, tzt, at, dt)


def _glu_kernel(y_ref, yn_ref, w_ref, o_ref):
    gate = jnp.dot(y_ref[...], w_ref[...], preferred_element_type=F32)
    o_ref[...] = (yn_ref[...].astype(F32) * jax.nn.sigmoid(gate)).astype(o_ref.dtype)


def _glu(y2d, w):
    t, k = y2d.shape
    tm = min(t, 1024)
    tn = min(k, 512)
    return pl.pallas_call(
        _glu_kernel,
        out_shape=jax.ShapeDtypeStruct((t, k), BF16),
        grid=(t // tm, k // tn),
        in_specs=[pl.BlockSpec((tm, k), lambda m, n: (m, 0)),
                  pl.BlockSpec((tm, tn), lambda m, n: (m, n)),
                  pl.BlockSpec((k, tn), lambda m, n: (0, n))],
        out_specs=pl.BlockSpec((tm, tn), lambda m, n: (m, n)),
        compiler_params=_cparams(("parallel", "arbitrary")),
        name="ssm_glu",
    )(y2d, y2d, w)


def _outproj_kernel(h_ref, ya_ref, yb_ref, yc_ref, w_ref, o_ref, *, tn):
    ka, kb = ya_ref.shape[1], yb_ref.shape[1]
    for j in range(h_ref.shape[1] // tn):
        sl = slice(j * tn, (j + 1) * tn)
        acc = jnp.dot(ya_ref[...], w_ref[:ka, sl], preferred_element_type=F32)
        acc += jnp.dot(yb_ref[...], w_ref[ka:ka + kb, sl], preferred_element_type=F32)
        acc += jnp.dot(yc_ref[...], w_ref[ka + kb:, sl], preferred_element_type=F32)
        o_ref[:, sl] = h_ref[:, sl] + acc


def _outproj(h2d, ya, yb, yc, w):
    t, d = h2d.shape
    ka, kb, kc = ya.shape[1], yb.shape[1], yc.shape[1]
    tm = min(t, 512)
    row = lambda m: (m, 0)
    return pl.pallas_call(
        functools.partial(_outproj_kernel, tn=min(d, 512)),
        out_shape=jax.ShapeDtypeStruct((t, d), F32),
        grid=(t // tm,),
        in_specs=[pl.BlockSpec((tm, d), row),
                  pl.BlockSpec((tm, ka), row),
                  pl.BlockSpec((tm, kb), row),
                  pl.BlockSpec((tm, kc), row),
                  pl.BlockSpec((ka + kb + kc, d), lambda m: (0, 0))],
        out_specs=pl.BlockSpec((tm, d), row),
        compiler_params=_cparams(("parallel",)),
        name="out_proj",
    )(h2d, ya, yb, yc, w)


def _mlp_kernel(h_ref, g_ref, wu_ref, wd_ref, o_ref, xn_ref, acc_ref):
    f = pl.program_id(1)

    @pl.when(f == 0)
    def _():
        xn_ref[...] = _rms(h_ref[...], g_ref[...]).astype(BF16)
        acc_ref[...] = h_ref[...]

    hid = jnp.dot(xn_ref[...], wu_ref[...], preferred_element_type=F32)
    hid = jnp.square(jnp.maximum(hid, 0.0)).astype(BF16)
    acc_ref[...] += jnp.dot(hid, wd_ref[...], preferred_element_type=F32)

    @pl.when(f == pl.num_programs(1) - 1)
    def _():
        o_ref[...] = acc_ref[...]


def _mlp(h2d, g, w_up, w_down):
    t, d = h2d.shape
    ff = w_up.shape[1]
    tm = min(t, 512)
    tf = min(ff, 512)
    return pl.pallas_call(
        _mlp_kernel,
        out_shape=jax.ShapeDtypeStruct((t, d), F32),
        grid=(t // tm, ff // tf),
        in_specs=[pl.BlockSpec((tm, d), lambda m, f: (m, 0)),
                  pl.BlockSpec((1, d), lambda m, f: (0, 0)),
                  pl.BlockSpec((d, tf), lambda m, f: (0, f)),
                  pl.BlockSpec((tf, d), lambda m, f: (f, 0))],
        out_specs=pl.BlockSpec((tm, d), lambda m, f: (m, 0)),
        scratch_shapes=[pltpu.VMEM((tm, d), BF16), pltpu.VMEM((tm, d), F32)],
        compiler_params=_cparams(("parallel", "arbitrary")),
        name="mlp_relu2",
    )(h2d, g.reshape(1, d), w_up, w_down)


def _ple_kernel(h_ref, g_ref, p_ref, wg_ref, wp_ref, fg_ref, o_ref, *, tn, final):
    h = h_ref[...]
    xn = _rms(h, g_ref[...]).astype(BF16)
    pb = p_ref[...]
    d = h.shape[1]
    for j in range(d // tn):
        sl = slice(j * tn, (j + 1) * tn)
        gate = jax.nn.sigmoid(jnp.dot(xn, wg_ref[:, sl], preferred_element_type=F32))
        proj = jnp.dot(pb, wp_ref[:, sl], preferred_element_type=F32)
        o_ref[:, sl] = h[:, sl] + gate * proj
    if final:
        o_ref[...] = _rms(o_ref[...], fg_ref[...])


def _ple(h2d, g, p2d, wg, wp, final_g, final):
    t, d = h2d.shape
    pd = p2d.shape[1]
    tm = min(t, 256)
    return pl.pallas_call(
        functools.partial(_ple_kernel, tn=min(d, 512), final=final),
        out_shape=jax.ShapeDtypeStruct((t, d), F32),
        grid=(t // tm,),
        in_specs=[pl.BlockSpec((tm, d), lambda m: (m, 0)),
                  pl.BlockSpec((1, d), lambda m: (0, 0)),
                  pl.BlockSpec((tm, pd), lambda m: (m, 0)),
                  pl.BlockSpec((d, d), lambda m: (0, 0)),
                  pl.BlockSpec((pd, d), lambda m: (0, 0)),
                  pl.BlockSpec((1, d), lambda m: (0, 0))],
        out_specs=pl.BlockSpec((tm, d), lambda m: (m, 0)),
        compiler_params=_cparams(("parallel",)),
        name="ple_gate",
    )(h2d, g.reshape(1, d), p2d, wg, wp, final_g.reshape(1, d))


SSM_CHUNK = 32
SSM_GROUP_BLOCK = 8


def kernel(x, p, positions, norm_mix_g, w_in, w_out, diff_lq1, diff_lk1, diff_lq2, diff_lk2,
           diff_subln_g, ssm_lambda_re, ssm_lambda_im, ssm_log_step, ssm_B_re, ssm_B_im,
           ssm_C_re, ssm_C_im, ssm_D, ssm_w_glu, norm_mlp_g, w_up, w_down, norm_ple_g,
           w_ple_gate, w_ple_proj, final_g):
    b, l, d = x.shape
    depth = w_in.shape[0]
    t = b * l
    tc = min(SSM_CHUNK, l)
    tabs = _rope_tables(positions)
    h = x.reshape(t, d)
    for i in range(depth):
        lambda_init = 0.8 - 0.6 * math.exp(-0.3 * i)
        z = _inproj(h, norm_mix_g[i], _permute_w_in(w_in[i]), tabs)
        z3 = z.reshape(b, l, Z_WIDTH)
        y_a = _diff_attn(z3, diff_lq1[i], diff_lk1[i], diff_lq2[i], diff_lk2[i],
                         diff_subln_g[i], lambda_init)
        y_b = _dsa_attn(z3)
        ops = _ssm_prep(ssm_lambda_re[i], ssm_lambda_im[i], ssm_log_step[i], ssm_B_re[i],
                        ssm_B_im[i], ssm_C_re[i], ssm_C_im[i], tc, SSM_GROUP_BLOCK)
        y_s = _s5(z3, ops, ssm_D[i], tc, SSM_GROUP_BLOCK)
        y_c = _glu(y_s.reshape(t, -1), ssm_w_glu[i].astype(BF16))
        h = _outproj(h, y_a.reshape(t, -1), y_b.reshape(t, -1), y_c, w_out[i].astype(BF16))
        h = _mlp(h, norm_mlp_g[i], w_up[i].astype(BF16), w_down[i].astype(BF16))
        h = _ple(h, norm_ple_g[i], p[i].reshape(t, -1).astype(BF16), w_ple_gate[i].astype(BF16),
                 w_ple_proj[i].astype(BF16), final_g, final=(i == depth - 1))
    return h.reshape(b, l, d)
```

```python
import functools
import math

import jax
import jax.numpy as jnp
from jax import lax
from jax.experimental import pallas as pl
from jax.experimental.pallas import tpu as pltpu

F32 = jnp.float32
BF16 = jnp.bfloat16
I32 = jnp.int32

LANES = 128
EPS = 1e-6
ROPE_THETA = 10000.0

DIFF_QK = 64
DIFF_V = 128
DIFF_HEADS = 4
DSA_DIM = 128
DSA_HEADS = 4
IDX_HEADS = 8
IDX_DIM = 64
TOPK_MAX = 256
SSM_C = 16
SSM_P = 64

COL_AQ, COL_AK, COL_IQ, COL_BQ, COL_AV = 0, 4, 8, 12, 16
COL_BK, COL_BV, COL_IKW, COL_CU = 20, 21, 22, 24
Z_WIDTH = 32 * LANES
IN_TN = 512

DIFF_TQ = 512
DIFF_TK = 512
DSA_TQ = 256
DSA_TK = 512
NEG_BIG = -1e30
LOG2E = math.log2(math.e)
VMEM_LIMIT = 56 * 1024 * 1024


def _cparams(sem):
    return pltpu.CompilerParams(dimension_semantics=sem, vmem_limit_bytes=VMEM_LIMIT)


def _rope_tables_kernel(pos_ref, f64_ref, f128_ref, c64_ref, s64_ref, c128_ref, s128_ref):
    pos = pos_ref[...]
    a64 = pos * f64_ref[...]
    a128 = pos * f128_ref[...]
    lane = lax.broadcasted_iota(I32, a64.shape, 1)
    c64_ref[...] = jnp.cos(a64)
    sn = jnp.sin(a64)
    s64_ref[...] = jnp.where((lane & 32) == 0, -sn, sn)
    c128_ref[...] = jnp.cos(a128)
    sn = jnp.sin(a128)
    s128_ref[...] = jnp.where((lane & 64) == 0, -sn, sn)


def _rope_tables(positions):
    t = positions.size
    pos = positions.reshape(t, 1).astype(F32)
    fr64 = ROPE_THETA ** (-jnp.arange(0, 64, 2, dtype=F32) / 64)
    fr128 = ROPE_THETA ** (-jnp.arange(0, 128, 2, dtype=F32) / 128)
    f64 = jnp.tile(fr64, 4).reshape(1, LANES)
    f128 = jnp.tile(fr128, 2).reshape(1, LANES)
    tm = min(t, 1024)
    tab = jax.ShapeDtypeStruct((t, LANES), F32)
    row = pl.BlockSpec((tm, LANES), lambda i: (i, 0))
    return pl.pallas_call(
        _rope_tables_kernel,
        out_shape=(tab, tab, tab, tab),
        grid=(t // tm,),
        in_specs=[pl.BlockSpec((tm, 1), lambda i: (i, 0)),
                  pl.BlockSpec((1, LANES), lambda i: (0, 0)),
                  pl.BlockSpec((1, LANES), lambda i: (0, 0))],
        out_specs=(row, row, row, row),
        compiler_params=_cparams(("parallel",)),
        name="rope_tables",
    )(pos, f64, f128)


def _rms(x, g):
    ms = jnp.mean(x * x, axis=-1, keepdims=True)
    return x * lax.rsqrt(ms + EPS) * g


def _swap_halves(a, half):
    lane = lax.broadcasted_iota(I32, a.shape, 1)
    return jnp.where((lane & half) == 0,
                     pltpu.roll(a, LANES - half, 1), pltpu.roll(a, half, 1))


def _inproj_kernel(x_ref, g_ref, w_ref, c64_ref, s64_ref, c128_ref, s128_ref, z_ref, *, tn):
    xn = _rms(x_ref[...], g_ref[...]).astype(BF16)

    def rope(a, half):
        if half == 32:
            return a * c64_ref[...] + _swap_halves(a, 32) * s64_ref[...]
        return a * c128_ref[...] + _swap_halves(a, 64) * s128_ref[...]

    def epilogue(col, a):
        if col < COL_BQ:
            return rope(a, 32)
        if col < COL_AV or col == COL_BK:
            return rope(a, 64)
        if col == COL_IKW:
            lane = lax.broadcasted_iota(I32, a.shape, 1)
            return jnp.where(lane < IDX_DIM, rope(a, 32), a)
        return a

    groups = tn // LANES
    for n in range(z_ref.shape[1] // tn):
        acc = jnp.dot(xn, w_ref[:, n * tn:(n + 1) * tn], preferred_element_type=F32)
        for j in range(groups):
            col = n * groups + j
            z_ref[:, col * LANES:(col + 1) * LANES] = epilogue(
                col, acc[:, j * LANES:(j + 1) * LANES]).astype(z_ref.dtype)


def _inproj(h2d, g, w, tabs):
    t, d = h2d.shape
    tm = min(t, 512)
    c64, s64, c128, s128 = tabs
    row = lambda m: (m, 0)
    tab = pl.BlockSpec((tm, LANES), row)
    return pl.pallas_call(
        functools.partial(_inproj_kernel, tn=IN_TN),
        out_shape=jax.ShapeDtypeStruct((t, Z_WIDTH), BF16),
        grid=(t // tm,),
        in_specs=[pl.BlockSpec((tm, d), row),
                  pl.BlockSpec((1, d), lambda m: (0, 0)),
                  pl.BlockSpec((d, Z_WIDTH), lambda m: (0, 0), pipeline_mode=pl.Buffered(1)),
                  tab, tab, tab, tab],
        out_specs=pl.BlockSpec((tm, Z_WIDTH), row),
        compiler_params=_cparams(("parallel",)),
        name="norm_inproj_rope",
    )(h2d, g.reshape(1, d), w, c64, s64, c128, s128)


def _permute_w_in(w):
    d = w.shape[0]
    pad = jnp.zeros((d, 56 + LANES), w.dtype)
    return jnp.concatenate(
        [w[:, 0:512], w[:, 512:1024], w[:, 2304:2816], w[:, 1536:2048], w[:, 1024:1536],
         w[:, 2048:2176], w[:, 2176:2304], w[:, 2816:2888], pad, w[:, 2888:3912]],
        axis=1).astype(BF16)


def _flash_step(carry, s, vb):
    m, l, acc = carry
    m_new = jnp.maximum(m, jnp.max(s, axis=1, keepdims=True))
    alpha = jnp.exp2(m - m_new)
    p = jnp.exp2(s - m_new)
    l = alpha * l + jnp.sum(p, axis=1, keepdims=True)
    acc = alpha * acc + jnp.dot(p.astype(BF16), vb, preferred_element_type=F32)
    return m_new, l, acc


def _flash_init(rows, dv):
    return (jnp.full((rows, 1), NEG_BIG, F32), jnp.zeros((rows, 1), F32),
            jnp.zeros((rows, dv), F32))


def _dot_nt(a, b):
    return lax.dot_general(a, b, (((1,), (1,)), ((), ())), preferred_element_type=F32)


def _diff_attn_kernel(q_ref, k_ref, vt_ref, lq1_ref, lk1_ref, lq2_ref, lk2_ref, g_ref, o_ref,
                      *, tq, tk, lambda_init):
    i = pl.program_id(2)
    q_t = jnp.transpose(q_ref[0].astype(F32)) * (DIFF_QK ** -0.5)
    dim = lax.broadcasted_iota(I32, q_t.shape, 0)
    q2 = jnp.concatenate([jnp.where(dim < DIFF_QK, q_t, 0.0),
                          jnp.where(dim >= DIFF_QK, q_t, 0.0)], axis=1).astype(BF16)

    def block(j, carry, diag):
        m, l, acc = carry
        kb = k_ref[0, pl.ds(j * tk, tk), :]
        vtb = vt_ref[0, 0, :, pl.ds(j * tk, tk)]
        s = jnp.dot(kb, q2, preferred_element_type=F32) * LOG2E
        if diag:
            key = j * tk + lax.broadcasted_iota(I32, s.shape, 0)
            qi = lax.broadcasted_iota(I32, s.shape, 1)
            qi = i * tq + jnp.where(qi >= tq, qi - tq, qi)
            s = jnp.where(key <= qi, s, NEG_BIG)
        m_new = jnp.maximum(m, jnp.max(s, axis=0, keepdims=True))
        alpha = jnp.exp2(m - m_new)
        p = jnp.exp2(s - m_new)
        l = alpha * l + jnp.sum(p, axis=0, keepdims=True)
        acc = alpha * acc + jnp.dot(vtb, p.astype(BF16), preferred_element_type=F32)
        return m_new, l, acc

    init = (jnp.full((1, 2 * tq), NEG_BIG, F32), jnp.zeros((1, 2 * tq), F32),
            jnp.zeros((DIFF_V, 2 * tq), F32))
    nfull = (i * tq) // tk
    carry = lax.fori_loop(0, nfull, lambda j, c: block(j, c, False), init)
    _, l, acc = block(nfull, carry, True)
    o = acc / l
    lam = (jnp.exp(jnp.sum(lq1_ref[...] * lk1_ref[...], keepdims=True))
           - jnp.exp(jnp.sum(lq2_ref[...] * lk2_ref[...], keepdims=True)) + lambda_init)
    out = jnp.transpose(o[:, :tq] - lam * o[:, tq:])
    out = _rms(out, g_ref[...]) * (1.0 - lambda_init)
    o_ref[0] = out.astype(o_ref.dtype)


def _diff_attn(z3, lq1, lk1, lq2, lk2, subln_g, lambda_init):
    b, l, _ = z3.shape
    tq = min(l, DIFF_TQ)
    tk = min(l, DIFF_TK)
    vec = pl.BlockSpec((1, DIFF_QK), lambda bi, h, i: (0, 0))
    v = z3[:, :, COL_AV * LANES:(COL_AV + DIFF_HEADS) * LANES]
    v_t = v.reshape(b, l, DIFF_HEADS, DIFF_V).transpose(0, 2, 3, 1)
    return pl.pallas_call(
        functools.partial(_diff_attn_kernel, tq=tq, tk=tk, lambda_init=lambda_init),
        out_shape=jax.ShapeDtypeStruct((b, l, DIFF_HEADS * DIFF_V), BF16),
        grid=(b, DIFF_HEADS, l // tq),
        in_specs=[pl.BlockSpec((1, tq, LANES), lambda bi, h, i: (bi, i, COL_AQ + h)),
                  pl.BlockSpec((1, l, LANES), lambda bi, h, i: (bi, 0, COL_AK + h)),
                  pl.BlockSpec((1, 1, DIFF_V, l), lambda bi, h, i: (bi, h, 0, 0)),
                  vec, vec, vec, vec,
                  pl.BlockSpec((1, DIFF_V), lambda bi, h, i: (0, 0))],
        out_specs=pl.BlockSpec((1, tq, DIFF_V), lambda bi, h, i: (bi, i, h)),
        compiler_params=_cparams(("parallel", "parallel", "arbitrary")),
        name="diff_attention",
    )(z3, z3, v_t, lq1.reshape(1, -1), lk1.reshape(1, -1), lq2.reshape(1, -1),
      lk2.reshape(1, -1), subln_g.reshape(1, -1))


def _sortable_key(s):
    bits = lax.bitcast_convert_type(s, I32)
    return bits ^ ((bits >> 31) & jnp.int32(0x7FFFFFFF))


def _dsa_kernel(bq_ref, iq_ref, iwq_ref, bk_ref, bvt_ref, ik_ref, o_ref, hi_ref, lo_ref,
                *, tq, tk, topk, seq_bits):
    i = pl.program_id(1)
    nkb = ((i + 1) * tq + tk - 1) // tk
    low16 = -(2 ** 15)
    i16 = jnp.int16

    def as16(v):
        return v.astype(i16)

    krow = lax.broadcasted_iota(I32, (tk, tq), 0)
    qcol = i * tq + lax.broadcasted_iota(I32, (tk, tq), 1)

    iq_t = jnp.transpose(iq_ref[0].astype(F32))
    iq_all = jnp.concatenate([iq_t[h * IDX_DIM:(h + 1) * IDX_DIM, :] for h in range(IDX_HEADS)],
                             axis=1).astype(BF16)
    iw_t = jnp.transpose(iwq_ref[0].astype(F32))[IDX_DIM:IDX_DIM + IDX_HEADS, :]
    iw_t = iw_t * (IDX_HEADS ** -0.5) * (IDX_DIM ** -0.5)

    def score_block(j, _):
        ikb = ik_ref[0, pl.ds(j * tk, tk), :][:, :IDX_DIM]
        r_all = jnp.dot(ikb, iq_all, preferred_element_type=F32)
        sc = jnp.zeros((tk, tq), F32)
        for h in range(IDX_HEADS):
            sc = sc + jnp.maximum(r_all[:, h * tq:(h + 1) * tq], 0.0) * iw_t[h:h + 1, :]
        sc = jnp.where(j * tk + krow <= qcol, sc + 0.0, -jnp.inf)
        key = _sortable_key(sc)
        hi_ref[j] = as16(key >> 16)
        lo_ref[j] = as16((key & 0xFFFF) + low16)
        return 0

    lax.fori_loop(0, nkb, score_block, 0)

    one16, zero16 = jnp.ones((), i16), jnp.zeros((), i16)

    def count(pred_fn):
        def blk(j, cnt):
            m = jnp.where(pred_fn(j), one16, zero16)
            parts = [m[r * 16:(r + 1) * 16, :] for r in range(tk // 16)]
            while len(parts) > 1:
                parts = [a + b for a, b in zip(parts[::2], parts[1::2])]
            return cnt + parts[0]
        cnt = lax.fori_loop(0, nkb, blk, jnp.zeros((16, tq), i16))
        return jnp.sum(cnt.astype(I32), axis=0, keepdims=True)

    def radix_select(ref, target):
        c0 = count(lambda j: ref[j] >= zero16)
        thr = jnp.where(c0 >= target, 0, low16).astype(I32)

        def bit_step(it, thr):
            cand = thr | (jnp.int32(1) << (14 - it))
            c16 = as16(cand)
            return jnp.where(count(lambda j: ref[j] >= c16) >= target, cand, thr)

        return lax.fori_loop(0, 15, bit_step, thr)

    t_hi = as16(radix_select(hi_ref, topk))
    need_lo = topk - count(lambda j: hi_ref[j] > t_hi)

    def mask_lo(j, _):
        lo_ref[j] = jnp.where(hi_ref[j] == t_hi, lo_ref[j], jnp.int16(low16))
        return 0

    lax.fori_loop(0, nkb, mask_lo, 0)
    t_lo = as16(radix_select(lo_ref, need_lo))

    def in_tie(j):
        return (hi_ref[j] == t_hi) & (lo_ref[j] == t_lo)

    need = need_lo - count(lambda j: lo_ref[j] > t_lo)
    has_excess = jnp.max(jnp.where(count(in_tie) > need, 1, 0)) > 0
    krow16 = as16(krow)

    def tie_limit():
        def step(it, q):
            cand = q + (jnp.int32(1) << (seq_bits - 1 - it))
            c16 = as16(cand)
            c = count(lambda j: in_tie(j) & (krow16 + as16(j * tk) < c16))
            return jnp.where(c < need, cand, q)
        return lax.fori_loop(0, seq_bits, step, jnp.zeros((1, tq), I32))

    jlim16 = as16(lax.cond(has_excess, tie_limit, lambda: jnp.full((1, tq), 2 ** seq_bits, I32)))

    q_t = jnp.transpose(bq_ref[0].astype(F32))
    q_all = jnp.concatenate([q_t[h * DSA_DIM:(h + 1) * DSA_DIM, :] for h in range(DSA_HEADS)],
                            axis=1).astype(BF16)
    scale = DSA_DIM ** -0.5 * LOG2E
    zero_b, neg_b = jnp.zeros((), BF16), jnp.full((), NEG_BIG, BF16)
    qcol16 = as16(qcol)

    def attn_block(j, carry):
        kb = bk_ref[0, pl.ds(j * tk, tk), :]
        vtb = bvt_ref[0, :, pl.ds(j * tk, tk)]
        hi, lo = hi_ref[j], lo_ref[j]
        kidx = krow16 + as16(j * tk)
        sel = (hi > t_hi) | ((hi == t_hi) & ((lo > t_lo) | ((lo == t_lo) & (kidx <= jlim16))))
        sel = sel & (kidx <= qcol16)
        bias = jnp.where(sel, zero_b, neg_b).astype(F32)
        m, l, acc = carry
        s = (jnp.dot(kb, q_all, preferred_element_type=F32) * scale
             + jnp.concatenate([bias] * DSA_HEADS, axis=1))
        m_new = jnp.maximum(m, jnp.max(s, axis=0, keepdims=True))
        alpha = jnp.exp2(m - m_new)
        p = jnp.exp2(s - m_new)
        l = alpha * l + jnp.sum(p, axis=0, keepdims=True)
        pb = p.astype(BF16)
        pv = jnp.concatenate(
            [jnp.dot(vtb, pb[:, h * tq:(h + 1) * tq], preferred_element_type=F32)
             for h in range(DSA_HEADS)], axis=1)
        return m_new, l, alpha * acc + pv

    wide_q = DSA_HEADS * tq
    init = (jnp.full((1, wide_q), NEG_BIG, F32), jnp.zeros((1, wide_q), F32),
            jnp.zeros((DSA_DIM, wide_q), F32))
    _, l, acc = lax.fori_loop(0, nkb, attn_block, init)
    o = acc / l
    for h in range(DSA_HEADS):
        o_ref[0, :, h * DSA_DIM:(h + 1) * DSA_DIM] = jnp.transpose(
            o[:, h * tq:(h + 1) * tq]).astype(o_ref.dtype)


def _dsa_attn(z3):
    b, l, _ = z3.shape
    tq = min(l, DSA_TQ)
    tk = min(l, DSA_TK)
    topk = min(TOPK_MAX, l // 4)
    seq_bits = max(1, (l - 1).bit_length())
    wide = 4
    v_t = jnp.swapaxes(z3[:, :, COL_BV * LANES:(COL_BV + 1) * LANES], 1, 2)
    return pl.pallas_call(
        functools.partial(_dsa_kernel, tq=tq, tk=tk, topk=topk, seq_bits=seq_bits),
        out_shape=jax.ShapeDtypeStruct((b, l, DSA_HEADS * DSA_DIM), BF16),
        grid=(b, l // tq),
        in_specs=[pl.BlockSpec((1, tq, 4 * LANES), lambda bi, i: (bi, i, COL_BQ // wide)),
                  pl.BlockSpec((1, tq, 4 * LANES), lambda bi, i: (bi, i, COL_IQ // wide)),
                  pl.BlockSpec((1, tq, LANES), lambda bi, i: (bi, i, COL_IKW)),
                  pl.BlockSpec((1, l, LANES), lambda bi, i: (bi, 0, COL_BK)),
                  pl.BlockSpec((1, DSA_DIM, l), lambda bi, i: (bi, 0, 0)),
                  pl.BlockSpec((1, l, LANES), lambda bi, i: (bi, 0, COL_IKW))],
        out_specs=pl.BlockSpec((1, tq, DSA_HEADS * DSA_DIM), lambda bi, i: (bi, i, 0)),
        scratch_shapes=[pltpu.VMEM((l // tk, tk, tq), jnp.int16),
                        pltpu.VMEM((l // tk, tk, tq), jnp.int16)],
        compiler_params=_cparams(("parallel", "arbitrary")),
        name="dsa_attention",
    )(z3, z3, z3, z3, v_t, z3)


def _cmul(ar, ai, br, bi):
    return ar * br - ai * bi, ar * bi + ai * br


def _ssm_prep_kernel(lre_ref, lim_ref, lstep_ref, btr_ref, bti_ref, cr_ref, ci_ref,
                     w1_ref, mt_ref, tzt_ref, at_ref, tz_ref, *, tc, gb):
    c = SSM_C
    p = SSM_P
    for gi in range(gb):
        lr = lre_ref[gi].reshape(1, 1, p)
        li = lim_ref[gi].reshape(1, 1, p)
        step = jnp.exp(lstep_ref[gi]).reshape(1, 1, 1)
        mag = jnp.exp(lr * step)
        a_re, a_im = mag * jnp.cos(li * step), mag * jnp.sin(li * step)
        den = lr * lr + li * li
        nr, ni = a_re - 1.0, a_im
        f_re, f_im = (nr * lr + ni * li) / den, (ni * lr - nr * li) / den
        bt_r, bt_i = btr_ref[gi][None], bti_ref[gi][None]
        bb_re = f_re * bt_r - f_im * bt_i
        bb_im = f_re * bt_i + f_im * bt_r
        fr, fi = jnp.ones_like(a_re), jnp.zeros_like(a_im)
        rr, ri = fr, fi
        pr, pi = a_re, a_im
        n = 1
        while n < tc:
            xr, xi = _cmul(fr, fi, pr, pi)
            fr, fi = jnp.concatenate([fr, xr], 0), jnp.concatenate([fi, xi], 0)
            xr, xi = _cmul(rr, ri, pr, pi)
            rr, ri = jnp.concatenate([xr, rr], 0), jnp.concatenate([xi, ri], 0)
            pr, pi = _cmul(pr, pi, pr, pi)
            n *= 2
        at_ref[gi] = jnp.concatenate([pr[0], pi[0]], axis=1)
        wr, wi = _cmul(rr, ri, bb_re, bb_im)
        w1 = jnp.concatenate([wr, wi], axis=2).reshape(tc * c, 2 * p)
        w1_ref[gi] = w1.astype(w1_ref.dtype)
        f1r, f1i = _cmul(fr, fi, a_re, a_im)
        c_re, c_im = cr_ref[gi][None], ci_ref[gi][None]
        mr, mi = _cmul(f1r, f1i, c_re, c_im)
        mt_ref[gi] = jnp.concatenate([mr, -mi], axis=2).reshape(tc * c, 2 * p).astype(mt_ref.dtype)
        er, ei = _cmul(fr, fi, c_re, c_im)
        e2 = jnp.concatenate([er, -ei], axis=2).reshape(tc * c, 2 * p)
        bcat = jnp.concatenate([bb_re[0], bb_im[0]], axis=1)
        kflat = lax.dot_general(e2, bcat, (((1,), (1,)), ((), ())),
                                precision=lax.Precision.HIGHEST,
                                preferred_element_type=F32)
        tz_ref[...] = jnp.zeros(tz_ref.shape, tz_ref.dtype)
        for s in range(tc):
            tz_ref[s * c:, s * c:(s + 1) * c] = kflat[:(tc - s) * c, :]
        tzt_ref[gi] = tz_ref[...].astype(tzt_ref.dtype)


def _ssm_prep(lam_re, lam_im, log_step, b_re, b_im, c_re, c_im, tc, gb):
    g, p = lam_re.shape
    c = SSM_C
    n = tc * c
    vecp = pl.BlockSpec((gb, 1, p), lambda i: (i, 0, 0))
    mat = pl.BlockSpec((gb, c, p), lambda i: (i, 0, 0))
    op = pl.BlockSpec((gb, n, 2 * p), lambda i: (i, 0, 0))
    return pl.pallas_call(
        functools.partial(_ssm_prep_kernel, tc=tc, gb=gb),
        out_shape=(jax.ShapeDtypeStruct((g, n, 2 * p), BF16),
                   jax.ShapeDtypeStruct((g, n, 2 * p), BF16),
                   jax.ShapeDtypeStruct((g, n, n), BF16),
                   jax.ShapeDtypeStruct((g, 1, 2 * p), F32)),
        grid=(g // gb,),
        in_specs=[vecp, vecp, pl.BlockSpec((gb, 1, 1), lambda i: (i, 0, 0)), mat, mat, mat, mat],
        out_specs=(op, op, pl.BlockSpec((gb, n, n), lambda i: (i, 0, 0)),
                   pl.BlockSpec((gb, 1, 2 * p), lambda i: (i, 0, 0))),
        scratch_shapes=[pltpu.VMEM((n, n), F32)],
        compiler_params=_cparams(("parallel",)),
        name="ssm_prep",
    )(lam_re.reshape(g, 1, p), lam_im.reshape(g, 1, p), log_step.reshape(g, 1, 1),
      jnp.swapaxes(b_re, 1, 2), jnp.swapaxes(b_im, 1, 2), c_re, c_im)


def _ssm_fused_kernel(u_ref, w1_ref, mt_ref, tzt_ref, at_ref, d_ref, y_ref,
                      uf_ref, ufl_ref, s_ref, s2_ref, x_ref, yfl_ref, yt_ref, *, tc, gb):
    c = SSM_C
    l = uf_ref.shape[0]
    nch = l // tc
    uf_ref[...] = u_ref[0].astype(F32)
    for s in range(tc):
        tile = uf_ref[pl.ds(s, nch, stride=tc), :]
        for g in range(gb):
            ufl_ref[g, :, s * c:(s + 1) * c] = tile[:, g * c:(g + 1) * c]
    for g in range(gb):
        sg = jnp.dot(ufl_ref[g].astype(BF16), w1_ref[g], preferred_element_type=F32)
        s_ref[pl.ds(g, nch, stride=gb), :] = sg
    a = at_ref[...].reshape(gb, 2 * SSM_P)
    lane = lax.broadcasted_iota(I32, a.shape, 1)
    a_sw = pltpu.roll(a, SSM_P, 1)
    a_re = jnp.where(lane < SSM_P, a, a_sw)
    a_im_s = jnp.where(lane < SSM_P, -a_sw, a)

    s2_ref[...] = pltpu.roll(s_ref[...], SSM_P, 1)

    def step(ci, carry):
        x1, x2 = carry
        r0 = pl.multiple_of(ci * gb, gb)
        x_ref[pl.ds(r0, gb), :] = x1
        n1 = x1 * a_re + x2 * a_im_s + s_ref[pl.ds(r0, gb), :]
        n2 = x2 * a_re - x1 * a_im_s + s2_ref[pl.ds(r0, gb), :]
        return n1, n2

    zero = jnp.zeros((gb, 2 * SSM_P), F32)
    lax.fori_loop(0, nch, step, (zero, zero))
    for g in range(gb):
        u = ufl_ref[g]
        xg = x_ref[pl.ds(g, nch, stride=gb), :]
        y = _dot_nt(u.astype(BF16), tzt_ref[g]) + _dot_nt(xg.astype(BF16), mt_ref[g])
        yfl_ref[g] = jax.nn.gelu(y + u * d_ref[g])
    for t in range(tc):
        tile = jnp.concatenate([yfl_ref[g, :, t * c:(t + 1) * c] for g in range(gb)], axis=1)
        yt_ref[pl.ds(t, nch, stride=tc), :] = tile
    y_ref[0] = yt_ref[...].astype(y_ref.dtype)


def _s5(z3, ops, d_skip, tc, gb):
    w1, mt, tzt, at = ops
    b, l, _ = z3.shape
    c = SSM_C
    g = w1.shape[0]
    nch = l // tc
    n = tc * c
    p2 = 2 * SSM_P
    wl = gb * c
    dt = jnp.tile(d_skip.reshape(g, 1, c), (1, 1, tc)).astype(F32)
    col0 = COL_CU * LANES // wl
    return pl.pallas_call(
        functools.partial(_ssm_fused_kernel, tc=tc, gb=gb),
        out_shape=jax.ShapeDtypeStruct((b, l, g * c), BF16),
        grid=(g // gb, b),
        in_specs=[pl.BlockSpec((1, l, wl), lambda gi, bi: (bi, 0, col0 + gi)),
                  pl.BlockSpec((gb, n, p2), lambda gi, bi: (gi, 0, 0)),
                  pl.BlockSpec((gb, n, p2), lambda gi, bi: (gi, 0, 0)),
                  pl.BlockSpec((gb, n, n), lambda gi, bi: (gi, 0, 0)),
                  pl.BlockSpec((gb, 1, p2), lambda gi, bi: (gi, 0, 0)),
                  pl.BlockSpec((gb, 1, n), lambda gi, bi: (gi, 0, 0))],
        out_specs=pl.BlockSpec((1, l, wl), lambda gi, bi: (bi, 0, gi)),
        scratch_shapes=[pltpu.VMEM((l, wl), F32),
                        pltpu.VMEM((gb, nch, n), F32),
                        pltpu.VMEM((nch * gb, p2), F32),
                        pltpu.VMEM((nch * gb, p2), F32),
                        pltpu.VMEM((nch * gb, p2), F32),
                        pltpu.VMEM((gb, nch, n), F32),
                        pltpu.VMEM((l, wl), F32)],
        compiler_params=_cparams(("parallel", "arbitrary")),
        name="ssm_fused",
    )(z3, w1, mt, tzt, at, dt)


def _glu_kernel(y_ref, yn_ref, w_ref, o_ref):
    gate = jnp.dot(y_ref[...], w_ref[...], preferred_element_type=F32)
    o_ref[...] = (yn_ref[...].astype(F32) * jax.nn.sigmoid(gate)).astype(o_ref.dtype)


def _glu(y2d, w):
    t, k = y2d.shape
    tm = min(t, 1024)
    tn = min(k, 512)
    return pl.pallas_call(
        _glu_kernel,
        out_shape=jax.ShapeDtypeStruct((t, k), BF16),
        grid=(t // tm, k // tn),
        in_specs=[pl.BlockSpec((tm, k), lambda m, n: (m, 0)),
                  pl.BlockSpec((tm, tn), lambda m, n: (m, n)),
                  pl.BlockSpec((k, tn), lambda m, n: (0, n))],
        out_specs=pl.BlockSpec((tm, tn), lambda m, n: (m, n)),
        compiler_params=_cparams(("parallel", "arbitrary")),
        name="ssm_glu",
    )(y2d, y2d, w)


def _outproj_kernel(h_ref, ya_ref, yb_ref, yc_ref, w_ref, o_ref, *, tn):
    ka, kb = ya_ref.shape[1], yb_ref.shape[1]
    for j in range(h_ref.shape[1] // tn):
        sl = slice(j * tn, (j + 1) * tn)
        acc = jnp.dot(ya_ref[...], w_ref[:ka, sl], preferred_element_type=F32)
        acc += jnp.dot(yb_ref[...], w_ref[ka:ka + kb, sl], preferred_element_type=F32)
        acc += jnp.dot(yc_ref[...], w_ref[ka + kb:, sl], preferred_element_type=F32)
        o_ref[:, sl] = h_ref[:, sl] + acc


def _outproj(h2d, ya, yb, yc, w):
    t, d = h2d.shape
    ka, kb, kc = ya.shape[1], yb.shape[1], yc.shape[1]
    tm = min(t, 512)
    row = lambda m: (m, 0)
    return pl.pallas_call(
        functools.partial(_outproj_kernel, tn=min(d, 512)),
        out_shape=jax.ShapeDtypeStruct((t, d), F32),
        grid=(t // tm,),
        in_specs=[pl.BlockSpec((tm, d), row),
                  pl.BlockSpec((tm, ka), row),
                  pl.BlockSpec((tm, kb), row),
                  pl.BlockSpec((tm, kc), row),
                  pl.BlockSpec((ka + kb + kc, d), lambda m: (0, 0))],
        out_specs=pl.BlockSpec((tm, d), row),
        compiler_params=_cparams(("parallel",)),
        name="out_proj",
    )(h2d, ya, yb, yc, w)


def _mlp_kernel(h_ref, g_ref, wu_ref, wd_ref, o_ref, xn_ref, acc_ref):
    f = pl.program_id(1)

    @pl.when(f == 0)
    def _():
        xn_ref[...] = _rms(h_ref[...], g_ref[...]).astype(BF16)
        acc_ref[...] = h_ref[...]

    hid = jnp.dot(xn_ref[...], wu_ref[...], preferred_element_type=F32)
    hid = jnp.square(jnp.maximum(hid, 0.0)).astype(BF16)
    acc_ref[...] += jnp.dot(hid, wd_ref[...], preferred_element_type=F32)

    @pl.when(f == pl.num_programs(1) - 1)
    def _():
        o_ref[...] = acc_ref[...]


def _mlp(h2d, g, w_up, w_down):
    t, d = h2d.shape
    ff = w_up.shape[1]
    tm = min(t, 512)
    tf = min(ff, 512)
    return pl.pallas_call(
        _mlp_kernel,
        out_shape=jax.ShapeDtypeStruct((t, d), F32),
        grid=(t // tm, ff // tf),
        in_specs=[pl.BlockSpec((tm, d), lambda m, f: (m, 0)),
                  pl.BlockSpec((1, d), lambda m, f: (0, 0)),
                  pl.BlockSpec((d, tf), lambda m, f: (0, f)),
                  pl.BlockSpec((tf, d), lambda m, f: (f, 0))],
        out_specs=pl.BlockSpec((tm, d), lambda m, f: (m, 0)),
        scratch_shapes=[pltpu.VMEM((tm, d), BF16), pltpu.VMEM((tm, d), F32)],
        compiler_params=_cparams(("parallel", "arbitrary")),
        name="mlp_relu2",
    )(h2d, g.reshape(1, d), w_up, w_down)


def _ple_kernel(h_ref, g_ref, p_ref, wg_ref, wp_ref, fg_ref, o_ref, *, tn, final):
    h = h_ref[...]
    xn = _rms(h, g_ref[...]).astype(BF16)
    pb = p_ref[...]
    d = h.shape[1]
    for j in range(d // tn):
        sl = slice(j * tn, (j + 1) * tn)
        gate = jax.nn.sigmoid(jnp.dot(xn, wg_ref[:, sl], preferred_element_type=F32))
        proj = jnp.dot(pb, wp_ref[:, sl], preferred_element_type=F32)
        o_ref[:, sl] = h[:, sl] + gate * proj
    if final:
        o_ref[...] = _rms(o_ref[...], fg_ref[...])


def _ple(h2d, g, p2d, wg, wp, final_g, final):
    t, d = h2d.shape
    pd = p2d.shape[1]
    tm = min(t, 256)
    return pl.pallas_call(
        functools.partial(_ple_kernel, tn=min(d, 512), final=final),
        out_shape=jax.ShapeDtypeStruct((t, d), F32),
        grid=(t // tm,),
        in_specs=[pl.BlockSpec((tm, d), lambda m: (m, 0)),
                  pl.BlockSpec((1, d), lambda m: (0, 0)),
                  pl.BlockSpec((tm, pd), lambda m: (m, 0)),
                  pl.BlockSpec((d, d), lambda m: (0, 0)),
                  pl.BlockSpec((pd, d), lambda m: (0, 0)),
                  pl.BlockSpec((1, d), lambda m: (0, 0))],
        out_specs=pl.BlockSpec((tm, d), lambda m: (m, 0)),
        compiler_params=_cparams(("parallel",)),
        name="ple_gate",
    )(h2d, g.reshape(1, d), p2d, wg, wp, final_g.reshape(1, d))


SSM_CHUNK = 32
SSM_GROUP_BLOCK = 8


def kernel(x, p, positions, norm_mix_g, w_in, w_out, diff_lq1, diff_lk1, diff_lq2, diff_lk2,
           diff_subln_g, ssm_lambda_re, ssm_lambda_im, ssm_log_step, ssm_B_re, ssm_B_im,
           ssm_C_re, ssm_C_im, ssm_D, ssm_w_glu, norm_mlp_g, w_up, w_down, norm_ple_g,
           w_ple_gate, w_ple_proj, final_g):
    b, l, d = x.shape
    depth = w_in.shape[0]
    t = b * l
    tc = min(SSM_CHUNK, l)
    tabs = _rope_tables(positions)
    h = x.reshape(t, d)
    for i in range(depth):
        lambda_init = 0.8 - 0.6 * math.exp(-0.3 * i)
        z = _inproj(h, norm_mix_g[i], _permute_w_in(w_in[i]), tabs)
        z3 = z.reshape(b, l, Z_WIDTH)
        y_a = _diff_attn(z3, diff_lq1[i], diff_lk1[i], diff_lq2[i], diff_lk2[i],
                         diff_subln_g[i], lambda_init)
        y_b = _dsa_attn(z3)
        ops = _ssm_prep(ssm_lambda_re[i], ssm_lambda_im[i], ssm_log_step[i], ssm_B_re[i],
                        ssm_B_im[i], ssm_C_re[i], ssm_C_im[i], tc, SSM_GROUP_BLOCK)
        y_s = _s5(z3, ops, ssm_D[i], tc, SSM_GROUP_BLOCK)
        y_c = _glu(y_s.reshape(t, -1), ssm_w_glu[i].astype(BF16))
        h = _outproj(h, y_a.reshape(t, -1), y_b.reshape(t, -1), y_c, w_out[i].astype(BF16))
        h = _mlp(h, norm_mlp_g[i], w_up[i].astype(BF16), w_down[i].astype(BF16))
        h = _ple(h, norm_ple_g[i], p[i].reshape(t, -1).astype(BF16), w_ple_gate[i].astype(BF16),
                 w_ple_proj[i].astype(BF16), final_g, final=(i == depth - 1))
    return h.reshape(b, l, d)
```

```python
import functools
import math

import jax
import jax.numpy as jnp
from jax import lax
from jax.experimental import pallas as pl
from jax.experimental.pallas import tpu as pltpu

F32 = jnp.float32
BF16 = jnp.bfloat16
I32 = jnp.int32

LANES = 128
EPS = 1e-6
ROPE_THETA = 10000.0

DIFF_QK = 64
DIFF_V = 128
DIFF_HEADS = 4
DSA_DIM = 128
DSA_HEADS = 4
IDX_HEADS = 8
IDX_DIM = 64
TOPK_MAX = 256
SSM_C = 16
SSM_P = 64

COL_AQ, COL_AK, COL_IQ, COL_BQ, COL_AV = 0, 4, 8, 12, 16
COL_BK, COL_BV, COL_IKW, COL_CU = 20, 21, 22, 24
Z_WIDTH = 32 * LANES
IN_TN = 512

MLP_TF = 1024
DIFF_TQ = 512
DIFF_TK = 512
DSA_TQ = 256
DSA_TK = 512
NEG_BIG = -1e30
LOG2E = math.log2(math.e)
VMEM_LIMIT = 56 * 1024 * 1024


def _cparams(sem):
    return pltpu.CompilerParams(dimension_semantics=sem, vmem_limit_bytes=VMEM_LIMIT)


def _rope_tables_kernel(pos_ref, f64_ref, f128_ref, c64_ref, s64_ref, c128_ref, s128_ref):
    pos = pos_ref[...]
    a64 = pos * f64_ref[...]
    a128 = pos * f128_ref[...]
    lane = lax.broadcasted_iota(I32, a64.shape, 1)
    c64_ref[...] = jnp.cos(a64)
    sn = jnp.sin(a64)
    s64_ref[...] = jnp.where((lane & 32) == 0, -sn, sn)
    c128_ref[...] = jnp.cos(a128)
    sn = jnp.sin(a128)
    s128_ref[...] = jnp.where((lane & 64) == 0, -sn, sn)


def _rope_tables(positions):
    t = positions.size
    pos = positions.reshape(t, 1).astype(F32)
    fr64 = ROPE_THETA ** (-jnp.arange(0, 64, 2, dtype=F32) / 64)
    fr128 = ROPE_THETA ** (-jnp.arange(0, 128, 2, dtype=F32) / 128)
    f64 = jnp.tile(fr64, 4).reshape(1, LANES)
    f128 = jnp.tile(fr128, 2).reshape(1, LANES)
    tm = min(t, 1024)
    tab = jax.ShapeDtypeStruct((t, LANES), F32)
    row = pl.BlockSpec((tm, LANES), lambda i: (i, 0))
    return pl.pallas_call(
        _rope_tables_kernel,
        out_shape=(tab, tab, tab, tab),
        grid=(t // tm,),
        in_specs=[pl.BlockSpec((tm, 1), lambda i: (i, 0)),
                  pl.BlockSpec((1, LANES), lambda i: (0, 0)),
                  pl.BlockSpec((1, LANES), lambda i: (0, 0))],
        out_specs=(row, row, row, row),
        compiler_params=_cparams(("parallel",)),
        name="rope_tables",
    )(pos, f64, f128)


def _rms(x, g):
    ms = jnp.mean(x * x, axis=-1, keepdims=True)
    return x * lax.rsqrt(ms + EPS) * g


def _swap_halves(a, half):
    lane = lax.broadcasted_iota(I32, a.shape, 1)
    return jnp.where((lane & half) == 0,
                     pltpu.roll(a, LANES - half, 1), pltpu.roll(a, half, 1))


def _inproj_kernel(x_ref, g_ref, w_ref, c64_ref, s64_ref, c128_ref, s128_ref, z_ref, *, tn):
    xn = _rms(x_ref[...], g_ref[...]).astype(BF16)

    def rope(a, half):
        if half == 32:
            return a * c64_ref[...] + _swap_halves(a, 32) * s64_ref[...]
        return a * c128_ref[...] + _swap_halves(a, 64) * s128_ref[...]

    def epilogue(col, a):
        if col < COL_BQ:
            return rope(a, 32)
        if col < COL_AV or col == COL_BK:
            return rope(a, 64)
        if col == COL_IKW:
            lane = lax.broadcasted_iota(I32, a.shape, 1)
            return jnp.where(lane < IDX_DIM, rope(a, 32), a)
        return a

    groups = tn // LANES
    for n in range(z_ref.shape[1] // tn):
        acc = jnp.dot(xn, w_ref[:, n * tn:(n + 1) * tn], preferred_element_type=F32)
        for j in range(groups):
            col = n * groups + j
            z_ref[:, col * LANES:(col + 1) * LANES] = epilogue(
                col, acc[:, j * LANES:(j + 1) * LANES]).astype(z_ref.dtype)


def _inproj(h2d, g, w, tabs):
    t, d = h2d.shape
    tm = min(t, 512)
    c64, s64, c128, s128 = tabs
    row = lambda m: (m, 0)
    tab = pl.BlockSpec((tm, LANES), row)
    return pl.pallas_call(
        functools.partial(_inproj_kernel, tn=IN_TN),
        out_shape=jax.ShapeDtypeStruct((t, Z_WIDTH), BF16),
        grid=(t // tm,),
        in_specs=[pl.BlockSpec((tm, d), row),
                  pl.BlockSpec((1, d), lambda m: (0, 0)),
                  pl.BlockSpec((d, Z_WIDTH), lambda m: (0, 0), pipeline_mode=pl.Buffered(1)),
                  tab, tab, tab, tab],
        out_specs=pl.BlockSpec((tm, Z_WIDTH), row),
        compiler_params=_cparams(("parallel",)),
        name="norm_inproj_rope",
    )(h2d, g.reshape(1, d), w, c64, s64, c128, s128)


def _permute_w_in(w):
    d = w.shape[0]
    pad = jnp.zeros((d, 56 + LANES), w.dtype)
    return jnp.concatenate(
        [w[:, 0:512], w[:, 512:1024], w[:, 2304:2816], w[:, 1536:2048], w[:, 1024:1536],
         w[:, 2048:2176], w[:, 2176:2304], w[:, 2816:2888], pad, w[:, 2888:3912]],
        axis=1).astype(BF16)


def _flash_step(carry, s, vb):
    m, l, acc = carry
    m_new = jnp.maximum(m, jnp.max(s, axis=1, keepdims=True))
    alpha = jnp.exp2(m - m_new)
    p = jnp.exp2(s - m_new)
    l = alpha * l + jnp.sum(p, axis=1, keepdims=True)
    acc = alpha * acc + jnp.dot(p.astype(BF16), vb, preferred_element_type=F32)
    return m_new, l, acc


def _flash_init(rows, dv):
    return (jnp.full((rows, 1), NEG_BIG, F32), jnp.zeros((rows, 1), F32),
            jnp.zeros((rows, dv), F32))


def _dot_nt(a, b):
    return lax.dot_general(a, b, (((1,), (1,)), ((), ())), preferred_element_type=F32)


def _diff_attn_kernel(q_ref, k_ref, vt_ref, lq1_ref, lk1_ref, lq2_ref, lk2_ref, g_ref, o_ref,
                      *, tq, tk, lambda_init):
    i = pl.program_id(2)
    q_t = jnp.transpose(q_ref[0].astype(F32)) * (DIFF_QK ** -0.5)
    dim = lax.broadcasted_iota(I32, q_t.shape, 0)
    q2 = jnp.concatenate([jnp.where(dim < DIFF_QK, q_t, 0.0),
                          jnp.where(dim >= DIFF_QK, q_t, 0.0)], axis=1).astype(BF16)

    def block(j, carry, diag):
        m, l, acc = carry
        kb = k_ref[0, pl.ds(j * tk, tk), :]
        vtb = vt_ref[0, 0, :, pl.ds(j * tk, tk)]
        s = jnp.dot(kb, q2, preferred_element_type=F32) * LOG2E
        if diag:
            key = j * tk + lax.broadcasted_iota(I32, s.shape, 0)
            qi = lax.broadcasted_iota(I32, s.shape, 1)
            qi = i * tq + jnp.where(qi >= tq, qi - tq, qi)
            s = jnp.where(key <= qi, s, NEG_BIG)
        m_new = jnp.maximum(m, jnp.max(s, axis=0, keepdims=True))
        alpha = jnp.exp2(m - m_new)
        p = jnp.exp2(s - m_new)
        l = alpha * l + jnp.sum(p, axis=0, keepdims=True)
        acc = alpha * acc + jnp.dot(vtb, p.astype(BF16), preferred_element_type=F32)
        return m_new, l, acc

    init = (jnp.full((1, 2 * tq), NEG_BIG, F32), jnp.zeros((1, 2 * tq), F32),
            jnp.zeros((DIFF_V, 2 * tq), F32))
    nfull = (i * tq) // tk
    carry = lax.fori_loop(0, nfull, lambda j, c: block(j, c, False), init)
    _, l, acc = block(nfull, carry, True)
    o = acc / l
    lam = (jnp.exp(jnp.sum(lq1_ref[...] * lk1_ref[...], keepdims=True))
           - jnp.exp(jnp.sum(lq2_ref[...] * lk2_ref[...], keepdims=True)) + lambda_init)
    out = jnp.transpose(o[:, :tq] - lam * o[:, tq:])
    out = _rms(out, g_ref[...]) * (1.0 - lambda_init)
    o_ref[0] = out.astype(o_ref.dtype)


def _diff_attn(z3, lq1, lk1, lq2, lk2, subln_g, lambda_init):
    b, l, _ = z3.shape
    tq = min(l, DIFF_TQ)
    tk = min(l, DIFF_TK)
    vec = pl.BlockSpec((1, DIFF_QK), lambda bi, h, i: (0, 0))
    v = z3[:, :, COL_AV * LANES:(COL_AV + DIFF_HEADS) * LANES]
    v_t = v.reshape(b, l, DIFF_HEADS, DIFF_V).transpose(0, 2, 3, 1)
    return pl.pallas_call(
        functools.partial(_diff_attn_kernel, tq=tq, tk=tk, lambda_init=lambda_init),
        out_shape=jax.ShapeDtypeStruct((b, l, DIFF_HEADS * DIFF_V), BF16),
        grid=(b, DIFF_HEADS, l // tq),
        in_specs=[pl.BlockSpec((1, tq, LANES), lambda bi, h, i: (bi, i, COL_AQ + h)),
                  pl.BlockSpec((1, l, LANES), lambda bi, h, i: (bi, 0, COL_AK + h)),
                  pl.BlockSpec((1, 1, DIFF_V, l), lambda bi, h, i: (bi, h, 0, 0)),
                  vec, vec, vec, vec,
                  pl.BlockSpec((1, DIFF_V), lambda bi, h, i: (0, 0))],
        out_specs=pl.BlockSpec((1, tq, DIFF_V), lambda bi, h, i: (bi, i, h)),
        compiler_params=_cparams(("parallel", "parallel", "arbitrary")),
        name="diff_attention",
    )(z3, z3, v_t, lq1.reshape(1, -1), lk1.reshape(1, -1), lq2.reshape(1, -1),
      lk2.reshape(1, -1), subln_g.reshape(1, -1))


def _sortable_key(s):
    bits = lax.bitcast_convert_type(s, I32)
    return bits ^ ((bits >> 31) & jnp.int32(0x7FFFFFFF))


def _dsa_kernel(bq_ref, iq_ref, iwq_ref, bk_ref, bvt_ref, ik_ref, o_ref, hi_ref, lo_ref,
                *, tq, tk, topk, seq_bits):
    i = pl.program_id(1)
    nkb = ((i + 1) * tq + tk - 1) // tk
    low16 = -(2 ** 15)
    i16 = jnp.int16

    def as16(v):
        return v.astype(i16)

    krow = lax.broadcasted_iota(I32, (tk, tq), 0)
    qcol = i * tq + lax.broadcasted_iota(I32, (tk, tq), 1)

    iq_t = jnp.transpose(iq_ref[0].astype(F32))
    iq_all = jnp.concatenate([iq_t[h * IDX_DIM:(h + 1) * IDX_DIM, :] for h in range(IDX_HEADS)],
                             axis=1).astype(BF16)
    iw_t = jnp.transpose(iwq_ref[0].astype(F32))[IDX_DIM:IDX_DIM + IDX_HEADS, :]
    iw_t = iw_t * (IDX_HEADS ** -0.5) * (IDX_DIM ** -0.5)

    def score_block(j, _):
        ikb = ik_ref[0, pl.ds(j * tk, tk), :][:, :IDX_DIM]
        r_all = jnp.dot(ikb, iq_all, preferred_element_type=F32)
        sc = jnp.zeros((tk, tq), F32)
        for h in range(IDX_HEADS):
            sc = sc + jnp.maximum(r_all[:, h * tq:(h + 1) * tq], 0.0) * iw_t[h:h + 1, :]
        sc = jnp.where(j * tk + krow <= qcol, sc + 0.0, -jnp.inf)
        key = _sortable_key(sc)
        hi_ref[j] = as16(key >> 16)
        lo_ref[j] = as16((key & 0xFFFF) + low16)
        return 0

    lax.fori_loop(0, nkb, score_block, 0)

    one16, zero16 = jnp.ones((), i16), jnp.zeros((), i16)

    def count(pred_fn):
        def blk(j, cnt):
            m = jnp.where(pred_fn(j), one16, zero16)
            parts = [m[r * 16:(r + 1) * 16, :] for r in range(tk // 16)]
            while len(parts) > 1:
                parts = [a + b for a, b in zip(parts[::2], parts[1::2])]
            return cnt + parts[0]
        cnt = lax.fori_loop(0, nkb, blk, jnp.zeros((16, tq), i16))
        return jnp.sum(cnt.astype(I32), axis=0, keepdims=True)

    def radix_select(ref, target):
        c0 = count(lambda j: ref[j] >= zero16)
        thr = jnp.where(c0 >= target, 0, low16).astype(I32)

        def bit_step(it, thr):
            cand = thr | (jnp.int32(1) << (14 - it))
            c16 = as16(cand)
            return jnp.where(count(lambda j: ref[j] >= c16) >= target, cand, thr)

        return lax.fori_loop(0, 15, bit_step, thr)

    t_hi = as16(radix_select(hi_ref, topk))
    need_lo = topk - count(lambda j: hi_ref[j] > t_hi)

    def mask_lo(j, _):
        lo_ref[j] = jnp.where(hi_ref[j] == t_hi, lo_ref[j], jnp.int16(low16))
        return 0

    lax.fori_loop(0, nkb, mask_lo, 0)
    t_lo = as16(radix_select(lo_ref, need_lo))

    def in_tie(j):
        return (hi_ref[j] == t_hi) & (lo_ref[j] == t_lo)

    need = need_lo - count(lambda j: lo_ref[j] > t_lo)
    has_excess = jnp.max(jnp.where(count(in_tie) > need, 1, 0)) > 0
    krow16 = as16(krow)

    def tie_limit():
        def step(it, q):
            cand = q + (jnp.int32(1) << (seq_bits - 1 - it))
            c16 = as16(cand)
            c = count(lambda j: in_tie(j) & (krow16 + as16(j * tk) < c16))
            return jnp.where(c < need, cand, q)
        return lax.fori_loop(0, seq_bits, step, jnp.zeros((1, tq), I32))

    jlim16 = as16(lax.cond(has_excess, tie_limit, lambda: jnp.full((1, tq), 2 ** seq_bits, I32)))

    q_t = jnp.transpose(bq_ref[0].astype(F32))
    q_all = jnp.concatenate([q_t[h * DSA_DIM:(h + 1) * DSA_DIM, :] for h in range(DSA_HEADS)],
                            axis=1).astype(BF16)
    scale = DSA_DIM ** -0.5 * LOG2E
    zero_b, neg_b = jnp.zeros((), BF16), jnp.full((), NEG_BIG, BF16)
    qcol16 = as16(qcol)

    def attn_block(j, carry):
        kb = bk_ref[0, pl.ds(j * tk, tk), :]
        vtb = bvt_ref[0, :, pl.ds(j * tk, tk)]
        hi, lo = hi_ref[j], lo_ref[j]
        kidx = krow16 + as16(j * tk)
        sel = (hi > t_hi) | ((hi == t_hi) & ((lo > t_lo) | ((lo == t_lo) & (kidx <= jlim16))))
        sel = sel & (kidx <= qcol16)
        bias = jnp.where(sel, zero_b, neg_b).astype(F32)
        m, l, acc = carry
        s = (jnp.dot(kb, q_all, preferred_element_type=F32) * scale
             + jnp.concatenate([bias] * DSA_HEADS, axis=1))
        m_new = jnp.maximum(m, jnp.max(s, axis=0, keepdims=True))
        alpha = jnp.exp2(m - m_new)
        p = jnp.exp2(s - m_new)
        l = alpha * l + jnp.sum(p, axis=0, keepdims=True)
        pb = p.astype(BF16)
        pv = jnp.concatenate(
            [jnp.dot(vtb, pb[:, h * tq:(h + 1) * tq], preferred_element_type=F32)
             for h in range(DSA_HEADS)], axis=1)
        return m_new, l, alpha * acc + pv

    wide_q = DSA_HEADS * tq
    init = (jnp.full((1, wide_q), NEG_BIG, F32), jnp.zeros((1, wide_q), F32),
            jnp.zeros((DSA_DIM, wide_q), F32))
    _, l, acc = lax.fori_loop(0, nkb, attn_block, init)
    o = acc / l
    for h in range(DSA_HEADS):
        o_ref[0, :, h * DSA_DIM:(h + 1) * DSA_DIM] = jnp.transpose(
            o[:, h * tq:(h + 1) * tq]).astype(o_ref.dtype)


def _dsa_attn(z3):
    b, l, _ = z3.shape
    tq = min(l, DSA_TQ)
    tk = min(l, DSA_TK)
    topk = min(TOPK_MAX, l // 4)
    seq_bits = max(1, (l - 1).bit_length())
    wide = 4
    v_t = jnp.swapaxes(z3[:, :, COL_BV * LANES:(COL_BV + 1) * LANES], 1, 2)
    return pl.pallas_call(
        functools.partial(_dsa_kernel, tq=tq, tk=tk, topk=topk, seq_bits=seq_bits),
        out_shape=jax.ShapeDtypeStruct((b, l, DSA_HEADS * DSA_DIM), BF16),
        grid=(b, l // tq),
        in_specs=[pl.BlockSpec((1, tq, 4 * LANES), lambda bi, i: (bi, i, COL_BQ // wide)),
                  pl.BlockSpec((1, tq, 4 * LANES), lambda bi, i: (bi, i, COL_IQ // wide)),
                  pl.BlockSpec((1, tq, LANES), lambda bi, i: (bi, i, COL_IKW)),
                  pl.BlockSpec((1, l, LANES), lambda bi, i: (bi, 0, COL_BK)),
                  pl.BlockSpec((1, DSA_DIM, l), lambda bi, i: (bi, 0, 0)),
                  pl.BlockSpec((1, l, LANES), lambda bi, i: (bi, 0, COL_IKW))],
        out_specs=pl.BlockSpec((1, tq, DSA_HEADS * DSA_DIM), lambda bi, i: (bi, i, 0)),
        scratch_shapes=[pltpu.VMEM((l // tk, tk, tq), jnp.int16),
                        pltpu.VMEM((l // tk, tk, tq), jnp.int16)],
        compiler_params=_cparams(("parallel", "arbitrary")),
        name="dsa_attention",
    )(z3, z3, z3, z3, v_t, z3)


def _cmul(ar, ai, br, bi):
    return ar * br - ai * bi, ar * bi + ai * br


def _ssm_prep_kernel(lre_ref, lim_ref, lstep_ref, btr_ref, bti_ref, cr_ref, ci_ref,
                     w1_ref, mt_ref, tzt_ref, at_ref, tz_ref, *, tc, gb):
    c = SSM_C
    p = SSM_P
    for gi in range(gb):
        lr = lre_ref[gi].reshape(1, 1, p)
        li = lim_ref[gi].reshape(1, 1, p)
        step = jnp.exp(lstep_ref[gi]).reshape(1, 1, 1)
        mag = jnp.exp(lr * step)
        a_re, a_im = mag * jnp.cos(li * step), mag * jnp.sin(li * step)
        den = lr * lr + li * li
        nr, ni = a_re - 1.0, a_im
        f_re, f_im = (nr * lr + ni * li) / den, (ni * lr - nr * li) / den
        bt_r, bt_i = btr_ref[gi][None], bti_ref[gi][None]
        bb_re = f_re * bt_r - f_im * bt_i
        bb_im = f_re * bt_i + f_im * bt_r
        fr, fi = jnp.ones_like(a_re), jnp.zeros_like(a_im)
        rr, ri = fr, fi
        pr, pi = a_re, a_im
        n = 1
        while n < tc:
            xr, xi = _cmul(fr, fi, pr, pi)
            fr, fi = jnp.concatenate([fr, xr], 0), jnp.concatenate([fi, xi], 0)
            xr, xi = _cmul(rr, ri, pr, pi)
            rr, ri = jnp.concatenate([xr, rr], 0), jnp.concatenate([xi, ri], 0)
            pr, pi = _cmul(pr, pi, pr, pi)
            n *= 2
        at_ref[gi] = jnp.concatenate([pr[0], pi[0]], axis=1)
        wr, wi = _cmul(rr, ri, bb_re, bb_im)
        w1 = jnp.concatenate([wr, wi], axis=2).reshape(tc * c, 2 * p)
        w1_ref[gi] = w1.astype(w1_ref.dtype)
        f1r, f1i = _cmul(fr, fi, a_re, a_im)
        c_re, c_im = cr_ref[gi][None], ci_ref[gi][None]
        mr, mi = _cmul(f1r, f1i, c_re, c_im)
        mt_ref[gi] = jnp.concatenate([mr, -mi], axis=2).reshape(tc * c, 2 * p).astype(mt_ref.dtype)
        er, ei = _cmul(fr, fi, c_re, c_im)
        e2 = jnp.concatenate([er, -ei], axis=2).reshape(tc * c, 2 * p)
        bcat = jnp.concatenate([bb_re[0], bb_im[0]], axis=1)
        kflat = lax.dot_general(e2, bcat, (((1,), (1,)), ((), ())),
                                precision=lax.Precision.HIGHEST,
                                preferred_element_type=F32)
        tz_ref[...] = jnp.zeros(tz_ref.shape, tz_ref.dtype)
        for s in range(tc):
            tz_ref[s * c:, s * c:(s + 1) * c] = kflat[:(tc - s) * c, :]
        tzt_ref[gi] = tz_ref[...].astype(tzt_ref.dtype)


def _ssm_prep(lam_re, lam_im, log_step, b_re, b_im, c_re, c_im, tc, gb):
    g, p = lam_re.shape
    c = SSM_C
    n = tc * c
    vecp = pl.BlockSpec((gb, 1, p), lambda i: (i, 0, 0))
    mat = pl.BlockSpec((gb, c, p), lambda i: (i, 0, 0))
    op = pl.BlockSpec((gb, n, 2 * p), lambda i: (i, 0, 0))
    return pl.pallas_call(
        functools.partial(_ssm_prep_kernel, tc=tc, gb=gb),
        out_shape=(jax.ShapeDtypeStruct((g, n, 2 * p), BF16),
                   jax.ShapeDtypeStruct((g, n, 2 * p), BF16),
                   jax.ShapeDtypeStruct((g, n, n), BF16),
                   jax.ShapeDtypeStruct((g, 1, 2 * p), F32)),
        grid=(g // gb,),
        in_specs=[vecp, vecp, pl.BlockSpec((gb, 1, 1), lambda i: (i, 0, 0)), mat, mat, mat, mat],
        out_specs=(op, op, pl.BlockSpec((gb, n, n), lambda i: (i, 0, 0)),
                   pl.BlockSpec((gb, 1, 2 * p), lambda i: (i, 0, 0))),
        scratch_shapes=[pltpu.VMEM((n, n), F32)],
        compiler_params=_cparams(("parallel",)),
        name="ssm_prep",
    )(lam_re.reshape(g, 1, p), lam_im.reshape(g, 1, p), log_step.reshape(g, 1, 1),
      jnp.swapaxes(b_re, 1, 2), jnp.swapaxes(b_im, 1, 2), c_re, c_im)


def _ssm_fused_kernel(u_ref, w1_ref, mt_ref, tzt_ref, at_ref, d_ref, y_ref,
                      uf_ref, ufl_ref, s_ref, s2_ref, x_ref, yfl_ref, yt_ref, *, tc, gb):
    c = SSM_C
    l = uf_ref.shape[0]
    nch = l // tc
    uf_ref[...] = u_ref[0].astype(F32)
    for s in range(tc):
        tile = uf_ref[pl.ds(s, nch, stride=tc), :]
        for g in range(gb):
            ufl_ref[g, :, s * c:(s + 1) * c] = tile[:, g * c:(g + 1) * c]
    for g in range(gb):
        sg = jnp.dot(ufl_ref[g].astype(BF16), w1_ref[g], preferred_element_type=F32)
        s_ref[pl.ds(g, nch, stride=gb), :] = sg
    a = at_ref[...].reshape(gb, 2 * SSM_P)
    lane = lax.broadcasted_iota(I32, a.shape, 1)
    a_sw = pltpu.roll(a, SSM_P, 1)
    a_re = jnp.where(lane < SSM_P, a, a_sw)
    a_im_s = jnp.where(lane < SSM_P, -a_sw, a)

    s2_ref[...] = pltpu.roll(s_ref[...], SSM_P, 1)

    def step(ci, carry):
        x1, x2 = carry
        r0 = pl.multiple_of(ci * gb, gb)
        x_ref[pl.ds(r0, gb), :] = x1
        n1 = x1 * a_re + x2 * a_im_s + s_ref[pl.ds(r0, gb), :]
        n2 = x2 * a_re - x1 * a_im_s + s2_ref[pl.ds(r0, gb), :]
        return n1, n2

    zero = jnp.zeros((gb, 2 * SSM_P), F32)
    lax.fori_loop(0, nch, step, (zero, zero))
    for g in range(gb):
        u = ufl_ref[g]
        xg = x_ref[pl.ds(g, nch, stride=gb), :]
        y = _dot_nt(u.astype(BF16), tzt_ref[g]) + _dot_nt(xg.astype(BF16), mt_ref[g])
        yfl_ref[g] = jax.nn.gelu(y + u * d_ref[g])
    for t in range(tc):
        tile = jnp.concatenate([yfl_ref[g, :, t * c:(t + 1) * c] for g in range(gb)], axis=1)
        yt_ref[pl.ds(t, nch, stride=tc), :] = tile
    y_ref[0] = yt_ref[...].astype(y_ref.dtype)


def _s5(z3, ops, d_skip, tc, gb):
    w1, mt, tzt, at = ops
    b, l, _ = z3.shape
    c = SSM_C
    g = w1.shape[0]
    nch = l // tc
    n = tc * c
    p2 = 2 * SSM_P
    wl = gb * c
    dt = jnp.tile(d_skip.reshape(g, 1, c), (1, 1, tc)).astype(F32)
    col0 = COL_CU * LANES // wl
    return pl.pallas_call(
        functools.partial(_ssm_fused_kernel, tc=tc, gb=gb),
        out_shape=jax.ShapeDtypeStruct((b, l, g * c), BF16),
        grid=(g // gb, b),
        in_specs=[pl.BlockSpec((1, l, wl), lambda gi, bi: (bi, 0, col0 + gi)),
                  pl.BlockSpec((gb, n, p2), lambda gi, bi: (gi, 0, 0)),
                  pl.BlockSpec((gb, n, p2), lambda gi, bi: (gi, 0, 0)),
                  pl.BlockSpec((gb, n, n), lambda gi, bi: (gi, 0, 0)),
                  pl.BlockSpec((gb, 1, p2), lambda gi, bi: (gi, 0, 0)),
                  pl.BlockSpec((gb, 1, n), lambda gi, bi: (gi, 0, 0))],
        out_specs=pl.BlockSpec((1, l, wl), lambda gi, bi: (bi, 0, gi)),
        scratch_shapes=[pltpu.VMEM((l, wl), F32),
                        pltpu.VMEM((gb, nch, n), F32),
                        pltpu.VMEM((nch * gb, p2), F32),
                        pltpu.VMEM((nch * gb, p2), F32),
                        pltpu.VMEM((nch * gb, p2), F32),
                        pltpu.VMEM((gb, nch, n), F32),
                        pltpu.VMEM((l, wl), F32)],
        compiler_params=_cparams(("parallel", "arbitrary")),
        name="ssm_fused",
    )(z3, w1, mt, tzt, at, dt)


def _outproj_kernel(h_ref, ya_ref, yb_ref, ys_ref, wg_ref, w_ref, o_ref, *, tn):
    ka, kb = ya_ref.shape[1], yb_ref.shape[1]
    ys = ys_ref[...]
    gate = jnp.dot(ys, wg_ref[...], preferred_element_type=F32)
    yc = (ys.astype(F32) * jax.nn.sigmoid(gate)).astype(BF16)
    for j in range(h_ref.shape[1] // tn):
        sl = slice(j * tn, (j + 1) * tn)
        acc = jnp.dot(ya_ref[...], w_ref[:ka, sl], preferred_element_type=F32)
        acc += jnp.dot(yb_ref[...], w_ref[ka:ka + kb, sl], preferred_element_type=F32)
        acc += jnp.dot(yc, w_ref[ka + kb:, sl], preferred_element_type=F32)
        o_ref[:, sl] = h_ref[:, sl] + acc


def _outproj(h2d, ya, yb, ys, w_glu, w):
    t, d = h2d.shape
    ka, kb, kc = ya.shape[1], yb.shape[1], ys.shape[1]
    tm = min(t, 512)
    row = lambda m: (m, 0)
    return pl.pallas_call(
        functools.partial(_outproj_kernel, tn=min(d, 512)),
        out_shape=jax.ShapeDtypeStruct((t, d), F32),
        grid=(t // tm,),
        in_specs=[pl.BlockSpec((tm, d), row),
                  pl.BlockSpec((tm, ka), row),
                  pl.BlockSpec((tm, kb), row),
                  pl.BlockSpec((tm, kc), row),
                  pl.BlockSpec((kc, kc), lambda m: (0, 0), pipeline_mode=pl.Buffered(1)),
                  pl.BlockSpec((ka + kb + kc, d), lambda m: (0, 0), pipeline_mode=pl.Buffered(1))],
        out_specs=pl.BlockSpec((tm, d), row),
        compiler_params=_cparams(("parallel",)),
        name="glu_out_proj",
    )(h2d, ya, yb, ys, w_glu, w)


def _mlp_kernel(h_ref, g_ref, wu_ref, wd_ref, o_ref, xn_ref, acc_ref):
    f = pl.program_id(1)

    @pl.when(f == 0)
    def _():
        xn_ref[...] = _rms(h_ref[...], g_ref[...]).astype(BF16)
        acc_ref[...] = h_ref[...]

    hid = jnp.dot(xn_ref[...], wu_ref[...], preferred_element_type=F32)
    hid = jnp.square(jnp.maximum(hid, 0.0)).astype(BF16)
    acc_ref[...] += jnp.dot(hid, wd_ref[...], preferred_element_type=F32)

    @pl.when(f == pl.num_programs(1) - 1)
    def _():
        o_ref[...] = acc_ref[...]


def _mlp(h2d, g, w_up, w_down):
    t, d = h2d.shape
    ff = w_up.shape[1]
    tm = min(t, 512)
    tf = min(ff, MLP_TF)
    return pl.pallas_call(
        _mlp_kernel,
        out_shape=jax.ShapeDtypeStruct((t, d), F32),
        grid=(t // tm, ff // tf),
        in_specs=[pl.BlockSpec((tm, d), lambda m, f: (m, 0)),
                  pl.BlockSpec((1, d), lambda m, f: (0, 0)),
                  pl.BlockSpec((d, tf), lambda m, f: (0, f)),
                  pl.BlockSpec((tf, d), lambda m, f: (f, 0))],
        out_specs=pl.BlockSpec((tm, d), lambda m, f: (m, 0)),
        scratch_shapes=[pltpu.VMEM((tm, d), BF16), pltpu.VMEM((tm, d), F32)],
        compiler_params=_cparams(("parallel", "arbitrary")),
        name="mlp_relu2",
    )(h2d, g.reshape(1, d), w_up, w_down)


def _ple_kernel(h_ref, g_ref, p_ref, wg_ref, wp_ref, fg_ref, o_ref, *, tn, final):
    h = h_ref[...]
    xn = _rms(h, g_ref[...]).astype(BF16)
    pb = p_ref[...]
    d = h.shape[1]
    for j in range(d // tn):
        sl = slice(j * tn, (j + 1) * tn)
        gate = jax.nn.sigmoid(jnp.dot(xn, wg_ref[:, sl], preferred_element_type=F32))
        proj = jnp.dot(pb, wp_ref[:, sl], preferred_element_type=F32)
        o_ref[:, sl] = h[:, sl] + gate * proj
    if final:
        o_ref[...] = _rms(o_ref[...], fg_ref[...])


def _ple(h2d, g, p2d, wg, wp, final_g, final):
    t, d = h2d.shape
    pd = p2d.shape[1]
    tm = min(t, 512)
    return pl.pallas_call(
        functools.partial(_ple_kernel, tn=min(d, 512), final=final),
        out_shape=jax.ShapeDtypeStruct((t, d), F32),
        grid=(t // tm,),
        in_specs=[pl.BlockSpec((tm, d), lambda m: (m, 0)),
                  pl.BlockSpec((1, d), lambda m: (0, 0)),
                  pl.BlockSpec((tm, pd), lambda m: (m, 0)),
                  pl.BlockSpec((d, d), lambda m: (0, 0), pipeline_mode=pl.Buffered(1)),
                  pl.BlockSpec((pd, d), lambda m: (0, 0), pipeline_mode=pl.Buffered(1)),
                  pl.BlockSpec((1, d), lambda m: (0, 0))],
        out_specs=pl.BlockSpec((tm, d), lambda m: (m, 0)),
        compiler_params=_cparams(("parallel",)),
        name="ple_gate",
    )(h2d, g.reshape(1, d), p2d, wg, wp, final_g.reshape(1, d))


SSM_CHUNK = 32
SSM_GROUP_BLOCK = 8


def kernel(x, p, positions, norm_mix_g, w_in, w_out, diff_lq1, diff_lk1, diff_lq2, diff_lk2,
           diff_subln_g, ssm_lambda_re, ssm_lambda_im, ssm_log_step, ssm_B_re, ssm_B_im,
           ssm_C_re, ssm_C_im, ssm_D, ssm_w_glu, norm_mlp_g, w_up, w_down, norm_ple_g,
           w_ple_gate, w_ple_proj, final_g):
    b, l, d = x.shape
    depth = w_in.shape[0]
    t = b * l
    tc = min(SSM_CHUNK, l)
    tabs = _rope_tables(positions)
    h = x.reshape(t, d)
    for i in range(depth):
        lambda_init = 0.8 - 0.6 * math.exp(-0.3 * i)
        z = _inproj(h, norm_mix_g[i], _permute_w_in(w_in[i]), tabs)
        z3 = z.reshape(b, l, Z_WIDTH)
        y_a = _diff_attn(z3, diff_lq1[i], diff_lk1[i], diff_lq2[i], diff_lk2[i],
                         diff_subln_g[i], lambda_init)
        y_b = _dsa_attn(z3)
        ops = _ssm_prep(ssm_lambda_re[i], ssm_lambda_im[i], ssm_log_step[i], ssm_B_re[i],
                        ssm_B_im[i], ssm_C_re[i], ssm_C_im[i], tc, SSM_GROUP_BLOCK)
        y_s = _s5(z3, ops, ssm_D[i], tc, SSM_GROUP_BLOCK)
        h = _outproj(h, y_a.reshape(t, -1), y_b.reshape(t, -1), y_s.reshape(t, -1),
                     ssm_w_glu[i].astype(BF16), w_out[i].astype(BF16))
        h = _mlp(h, norm_mlp_g[i], w_up[i].astype(BF16), w_down[i].astype(BF16))
        h = _ple(h, norm_ple_g[i], p[i].reshape(t, -1).astype(BF16), w_ple_gate[i].astype(BF16),
                 w_ple_proj[i].astype(BF16), final_g, final=(i == depth - 1))
    return h.reshape(b, l, d)
```

```python
import functools
import math

import jax
import jax.numpy as jnp
from jax import lax
from jax.experimental import pallas as pl
from jax.experimental.pallas import tpu as pltpu

F32 = jnp.float32
BF16 = jnp.bfloat16
I32 = jnp.int32

LANES = 128
EPS = 1e-6
ROPE_THETA = 10000.0

DIFF_QK = 64
DIFF_V = 128
DIFF_HEADS = 4
DSA_DIM = 128
DSA_HEADS = 4
IDX_HEADS = 8
IDX_DIM = 64
TOPK_MAX = 256
SSM_C = 16
SSM_P = 64

COL_AQ, COL_AK, COL_IQ, COL_BQ, COL_AV = 0, 4, 8, 12, 16
COL_BK, COL_BV, COL_IKW, COL_CU = 20, 21, 22, 24
Z_WIDTH = 32 * LANES
IN_TN = 512

MLP_TF = 1024
DIFF_TQ = 512
DIFF_TK = 512
DSA_TQ = 256
DSA_TK = 512
NEG_BIG = -1e30
LOG2E = math.log2(math.e)
VMEM_LIMIT = 56 * 1024 * 1024


def _cparams(sem):
    return pltpu.CompilerParams(dimension_semantics=sem, vmem_limit_bytes=VMEM_LIMIT)


def _rope_tables_kernel(pos_ref, f64_ref, f128_ref, c64_ref, s64_ref, c128_ref, s128_ref):
    pos = pos_ref[...]
    a64 = pos * f64_ref[...]
    a128 = pos * f128_ref[...]
    lane = lax.broadcasted_iota(I32, a64.shape, 1)
    c64_ref[...] = jnp.cos(a64)
    sn = jnp.sin(a64)
    s64_ref[...] = jnp.where((lane & 32) == 0, -sn, sn)
    c128_ref[...] = jnp.cos(a128)
    sn = jnp.sin(a128)
    s128_ref[...] = jnp.where((lane & 64) == 0, -sn, sn)


def _rope_tables(positions):
    t = positions.size
    pos = positions.reshape(t, 1).astype(F32)
    fr64 = ROPE_THETA ** (-jnp.arange(0, 64, 2, dtype=F32) / 64)
    fr128 = ROPE_THETA ** (-jnp.arange(0, 128, 2, dtype=F32) / 128)
    f64 = jnp.tile(fr64, 4).reshape(1, LANES)
    f128 = jnp.tile(fr128, 2).reshape(1, LANES)
    tm = min(t, 1024)
    tab = jax.ShapeDtypeStruct((t, LANES), F32)
    row = pl.BlockSpec((tm, LANES), lambda i: (i, 0))
    return pl.pallas_call(
        _rope_tables_kernel,
        out_shape=(tab, tab, tab, tab),
        grid=(t // tm,),
        in_specs=[pl.BlockSpec((tm, 1), lambda i: (i, 0)),
                  pl.BlockSpec((1, LANES), lambda i: (0, 0)),
                  pl.BlockSpec((1, LANES), lambda i: (0, 0))],
        out_specs=(row, row, row, row),
        compiler_params=_cparams(("parallel",)),
        name="rope_tables",
    )(pos, f64, f128)


def _rms(x, g):
    ms = jnp.mean(x * x, axis=-1, keepdims=True)
    return x * lax.rsqrt(ms + EPS) * g


def _swap_halves(a, half):
    lane = lax.broadcasted_iota(I32, a.shape, 1)
    return jnp.where((lane & half) == 0,
                     pltpu.roll(a, LANES - half, 1), pltpu.roll(a, half, 1))


def _inproj_kernel(x_ref, g_ref, w_ref, c64_ref, s64_ref, c128_ref, s128_ref, z_ref, *, tn):
    xn = _rms(x_ref[...], g_ref[...]).astype(BF16)

    def rope(a, half):
        if half == 32:
            return a * c64_ref[...] + _swap_halves(a, 32) * s64_ref[...]
        return a * c128_ref[...] + _swap_halves(a, 64) * s128_ref[...]

    def epilogue(col, a):
        if col < COL_BQ:
            return rope(a, 32)
        if col < COL_AV or col == COL_BK:
            return rope(a, 64)
        if col == COL_IKW:
            lane = lax.broadcasted_iota(I32, a.shape, 1)
            return jnp.where(lane < IDX_DIM, rope(a, 32), a)
        return a

    groups = tn // LANES
    for n in range(z_ref.shape[1] // tn):
        acc = jnp.dot(xn, w_ref[:, n * tn:(n + 1) * tn], preferred_element_type=F32)
        for j in range(groups):
            col = n * groups + j
            z_ref[:, col * LANES:(col + 1) * LANES] = epilogue(
                col, acc[:, j * LANES:(j + 1) * LANES]).astype(z_ref.dtype)


def _inproj(h2d, g, w, tabs):
    t, d = h2d.shape
    tm = min(t, 512)
    c64, s64, c128, s128 = tabs
    row = lambda m: (m, 0)
    tab = pl.BlockSpec((tm, LANES), row)
    return pl.pallas_call(
        functools.partial(_inproj_kernel, tn=IN_TN),
        out_shape=jax.ShapeDtypeStruct((t, Z_WIDTH), BF16),
        grid=(t // tm,),
        in_specs=[pl.BlockSpec((tm, d), row),
                  pl.BlockSpec((1, d), lambda m: (0, 0)),
                  pl.BlockSpec((d, Z_WIDTH), lambda m: (0, 0), pipeline_mode=pl.Buffered(1)),
                  tab, tab, tab, tab],
        out_specs=pl.BlockSpec((tm, Z_WIDTH), row),
        compiler_params=_cparams(("parallel",)),
        name="norm_inproj_rope",
    )(h2d, g.reshape(1, d), w, c64, s64, c128, s128)


def _permute_w_in(w):
    d = w.shape[0]
    pad = jnp.zeros((d, 56 + LANES), w.dtype)
    return jnp.concatenate(
        [w[:, 0:512], w[:, 512:1024], w[:, 2304:2816], w[:, 1536:2048], w[:, 1024:1536],
         w[:, 2048:2176], w[:, 2176:2304], w[:, 2816:2888], pad, w[:, 2888:3912]],
        axis=1).astype(BF16)


def _flash_step(carry, s, vb):
    m, l, acc = carry
    m_new = jnp.maximum(m, jnp.max(s, axis=1, keepdims=True))
    alpha = jnp.exp2(m - m_new)
    p = jnp.exp2(s - m_new)
    l = alpha * l + jnp.sum(p, axis=1, keepdims=True)
    acc = alpha * acc + jnp.dot(p.astype(BF16), vb, preferred_element_type=F32)
    return m_new, l, acc


def _flash_init(rows, dv):
    return (jnp.full((rows, 1), NEG_BIG, F32), jnp.zeros((rows, 1), F32),
            jnp.zeros((rows, dv), F32))


def _dot_nt(a, b):
    return lax.dot_general(a, b, (((1,), (1,)), ((), ())), preferred_element_type=F32)


def _diff_attn_kernel(q_ref, k_ref, vt_ref, lq1_ref, lk1_ref, lq2_ref, lk2_ref, g_ref, o_ref,
                      *, tq, tk, lambda_init):
    i = pl.program_id(2)
    q_t = jnp.transpose(q_ref[0].astype(F32)) * (DIFF_QK ** -0.5)
    dim = lax.broadcasted_iota(I32, q_t.shape, 0)
    q2 = jnp.concatenate([jnp.where(dim < DIFF_QK, q_t, 0.0),
                          jnp.where(dim >= DIFF_QK, q_t, 0.0)], axis=1).astype(BF16)

    def block(j, carry, diag):
        m, l, acc = carry
        kb = k_ref[0, pl.ds(j * tk, tk), :]
        vtb = vt_ref[0, 0, :, pl.ds(j * tk, tk)]
        s = jnp.dot(kb, q2, preferred_element_type=F32) * LOG2E
        if diag:
            key = j * tk + lax.broadcasted_iota(I32, s.shape, 0)
            qi = lax.broadcasted_iota(I32, s.shape, 1)
            qi = i * tq + jnp.where(qi >= tq, qi - tq, qi)
            s = jnp.where(key <= qi, s, NEG_BIG)
        m_new = jnp.maximum(m, jnp.max(s, axis=0, keepdims=True))
        alpha = jnp.exp2(m - m_new)
        p = jnp.exp2(s - m_new)
        l = alpha * l + jnp.sum(p, axis=0, keepdims=True)
        acc = alpha * acc + jnp.dot(vtb, p.astype(BF16), preferred_element_type=F32)
        return m_new, l, acc

    init = (jnp.full((1, 2 * tq), NEG_BIG, F32), jnp.zeros((1, 2 * tq), F32),
            jnp.zeros((DIFF_V, 2 * tq), F32))
    nfull = (i * tq) // tk
    carry = lax.fori_loop(0, nfull, lambda j, c: block(j, c, False), init)
    _, l, acc = block(nfull, carry, True)
    o = acc / l
    lam = (jnp.exp(jnp.sum(lq1_ref[...] * lk1_ref[...], keepdims=True))
           - jnp.exp(jnp.sum(lq2_ref[...] * lk2_ref[...], keepdims=True)) + lambda_init)
    out = jnp.transpose(o[:, :tq] - lam * o[:, tq:])
    out = _rms(out, g_ref[...]) * (1.0 - lambda_init)
    o_ref[0] = out.astype(o_ref.dtype)


def _diff_attn(z3, lq1, lk1, lq2, lk2, subln_g, lambda_init):
    b, l, _ = z3.shape
    tq = min(l, DIFF_TQ)
    tk = min(l, DIFF_TK)
    vec = pl.BlockSpec((1, DIFF_QK), lambda bi, h, i: (0, 0))
    v = z3[:, :, COL_AV * LANES:(COL_AV + DIFF_HEADS) * LANES]
    v_t = v.reshape(b, l, DIFF_HEADS, DIFF_V).transpose(0, 2, 3, 1)
    return pl.pallas_call(
        functools.partial(_diff_attn_kernel, tq=tq, tk=tk, lambda_init=lambda_init),
        out_shape=jax.ShapeDtypeStruct((b, l, DIFF_HEADS * DIFF_V), BF16),
        grid=(b, DIFF_HEADS, l // tq),
        in_specs=[pl.BlockSpec((1, tq, LANES), lambda bi, h, i: (bi, i, COL_AQ + h)),
                  pl.BlockSpec((1, l, LANES), lambda bi, h, i: (bi, 0, COL_AK + h)),
                  pl.BlockSpec((1, 1, DIFF_V, l), lambda bi, h, i: (bi, h, 0, 0)),
                  vec, vec, vec, vec,
                  pl.BlockSpec((1, DIFF_V), lambda bi, h, i: (0, 0))],
        out_specs=pl.BlockSpec((1, tq, DIFF_V), lambda bi, h, i: (bi, i, h)),
        compiler_params=_cparams(("parallel", "parallel", "arbitrary")),
        name="diff_attention",
    )(z3, z3, v_t, lq1.reshape(1, -1), lk1.reshape(1, -1), lq2.reshape(1, -1),
      lk2.reshape(1, -1), subln_g.reshape(1, -1))


def _sortable_key(s):
    bits = lax.bitcast_convert_type(s, I32)
    return bits ^ ((bits >> 31) & jnp.int32(0x7FFFFFFF))


def _dsa_kernel(bq_ref, iq_ref, iwq_ref, bk_ref, bvt_ref, ik_ref, o_ref, hi_ref, lo_ref,
                *, tq, tk, topk, seq_bits):
    i = pl.program_id(1)
    nkb = ((i + 1) * tq + tk - 1) // tk
    low16 = -(2 ** 15)
    i16 = jnp.int16

    def as16(v):
        return v.astype(i16)

    krow = lax.broadcasted_iota(I32, (tk, tq), 0)
    qcol = i * tq + lax.broadcasted_iota(I32, (tk, tq), 1)

    iq_t = jnp.transpose(iq_ref[0].astype(F32))
    iq_all = jnp.concatenate([iq_t[h * IDX_DIM:(h + 1) * IDX_DIM, :] for h in range(IDX_HEADS)],
                             axis=1).astype(BF16)
    iw_t = jnp.transpose(iwq_ref[0].astype(F32))[IDX_DIM:IDX_DIM + IDX_HEADS, :]
    iw_t = iw_t * (IDX_HEADS ** -0.5) * (IDX_DIM ** -0.5)

    def score_block(j, _):
        ikb = ik_ref[0, pl.ds(j * tk, tk), :][:, :IDX_DIM]
        r_all = jnp.dot(ikb, iq_all, preferred_element_type=F32)
        sc = jnp.zeros((tk, tq), F32)
        for h in range(IDX_HEADS):
            sc = sc + jnp.maximum(r_all[:, h * tq:(h + 1) * tq], 0.0) * iw_t[h:h + 1, :]
        sc = jnp.where(j * tk + krow <= qcol, sc + 0.0, -jnp.inf)
        key = _sortable_key(sc)
        hi_ref[j] = as16(key >> 16)
        lo_ref[j] = as16((key & 0xFFFF) + low16)
        return 0

    lax.fori_loop(0, nkb, score_block, 0)

    one16, zero16 = jnp.ones((), i16), jnp.zeros((), i16)

    def counts(block_fn, n):
        def blk(j, cnts):
            out = []
            for pred, cnt in zip(block_fn(j), cnts):
                m = jnp.where(pred, one16, zero16)
                parts = [m[r * 16:(r + 1) * 16, :] for r in range(tk // 16)]
                while len(parts) > 1:
                    parts = [a + b for a, b in zip(parts[::2], parts[1::2])]
                out.append(cnt + parts[0])
            return tuple(out)
        cnts = lax.fori_loop(0, nkb, blk, tuple(jnp.zeros((16, tq), i16) for _ in range(n)))
        return [jnp.sum(cnt.astype(I32), axis=0, keepdims=True) for cnt in cnts]

    def count(pred_fn):
        return counts(lambda j: (pred_fn(j),), 1)[0]

    def radix_select(ref, target, c0=None):
        if c0 is None:
            c0 = count(lambda j: ref[j] >= zero16)
        thr = jnp.where(c0 >= target, 0, low16).astype(I32)

        def bit_step(it, thr):
            cand = thr | (jnp.int32(1) << (14 - it))
            c16 = as16(cand)
            return jnp.where(count(lambda j: ref[j] >= c16) >= target, cand, thr)

        return lax.fori_loop(0, 15, bit_step, thr)

    t_hi = as16(radix_select(hi_ref, topk))
    def above_and_mask_lo(j):
        hi = hi_ref[j]
        lo = jnp.where(hi == t_hi, lo_ref[j], jnp.int16(low16))
        lo_ref[j] = lo
        return hi > t_hi, lo >= zero16

    c_above, c0_lo = counts(above_and_mask_lo, 2)
    need_lo = topk - c_above
    t_lo = as16(radix_select(lo_ref, need_lo, c0_lo))

    def in_tie(j):
        return (hi_ref[j] == t_hi) & (lo_ref[j] == t_lo)

    c_gt_lo, c_tie = counts(lambda j: (lo_ref[j] > t_lo, in_tie(j)), 2)
    need = need_lo - c_gt_lo
    has_excess = jnp.max(jnp.where(c_tie > need, 1, 0)) > 0
    krow16 = as16(krow)

    def tie_limit():
        def step(it, q):
            cand = q + (jnp.int32(1) << (seq_bits - 1 - it))
            c16 = as16(cand)
            c = count(lambda j: in_tie(j) & (krow16 + as16(j * tk) < c16))
            return jnp.where(c < need, cand, q)
        return lax.fori_loop(0, seq_bits, step, jnp.zeros((1, tq), I32))

    jlim16 = as16(lax.cond(has_excess, tie_limit, lambda: jnp.full((1, tq), 2 ** seq_bits, I32)))

    q_t = jnp.transpose(bq_ref[0].astype(F32))
    q_all = jnp.concatenate([q_t[h * DSA_DIM:(h + 1) * DSA_DIM, :] for h in range(DSA_HEADS)],
                            axis=1).astype(BF16)
    scale = DSA_DIM ** -0.5 * LOG2E
    zero_b, neg_b = jnp.zeros((), BF16), jnp.full((), NEG_BIG, BF16)
    qcol16 = as16(qcol)

    def attn_block(j, carry):
        kb = bk_ref[0, pl.ds(j * tk, tk), :]
        vtb = bvt_ref[0, :, pl.ds(j * tk, tk)]
        hi, lo = hi_ref[j], lo_ref[j]
        kidx = krow16 + as16(j * tk)
        sel = (hi > t_hi) | ((hi == t_hi) & ((lo > t_lo) | ((lo == t_lo) & (kidx <= jlim16))))
        sel = sel & (kidx <= qcol16)
        bias = jnp.where(sel, zero_b, neg_b).astype(F32)
        m, l, acc = carry
        s = (jnp.dot(kb, q_all, preferred_element_type=F32) * scale
             + jnp.concatenate([bias] * DSA_HEADS, axis=1))
        m_new = jnp.maximum(m, jnp.max(s, axis=0, keepdims=True))
        alpha = jnp.exp2(m - m_new)
        p = jnp.exp2(s - m_new)
        l = alpha * l + jnp.sum(p, axis=0, keepdims=True)
        pb = p.astype(BF16)
        pv = jnp.concatenate(
            [jnp.dot(vtb, pb[:, h * tq:(h + 1) * tq], preferred_element_type=F32)
             for h in range(DSA_HEADS)], axis=1)
        return m_new, l, alpha * acc + pv

    wide_q = DSA_HEADS * tq
    init = (jnp.full((1, wide_q), NEG_BIG, F32), jnp.zeros((1, wide_q), F32),
            jnp.zeros((DSA_DIM, wide_q), F32))
    _, l, acc = lax.fori_loop(0, nkb, attn_block, init)
    o = acc / l
    for h in range(DSA_HEADS):
        o_ref[0, :, h * DSA_DIM:(h + 1) * DSA_DIM] = jnp.transpose(
            o[:, h * tq:(h + 1) * tq]).astype(o_ref.dtype)


def _dsa_attn(z3):
    b, l, _ = z3.shape
    tq = min(l, DSA_TQ)
    tk = min(l, DSA_TK)
    topk = min(TOPK_MAX, l // 4)
    seq_bits = max(1, (l - 1).bit_length())
    wide = 4
    v_t = jnp.swapaxes(z3[:, :, COL_BV * LANES:(COL_BV + 1) * LANES], 1, 2)
    return pl.pallas_call(
        functools.partial(_dsa_kernel, tq=tq, tk=tk, topk=topk, seq_bits=seq_bits),
        out_shape=jax.ShapeDtypeStruct((b, l, DSA_HEADS * DSA_DIM), BF16),
        grid=(b, l // tq),
        in_specs=[pl.BlockSpec((1, tq, 4 * LANES), lambda bi, i: (bi, i, COL_BQ // wide)),
                  pl.BlockSpec((1, tq, 4 * LANES), lambda bi, i: (bi, i, COL_IQ // wide)),
                  pl.BlockSpec((1, tq, LANES), lambda bi, i: (bi, i, COL_IKW)),
                  pl.BlockSpec((1, l, LANES), lambda bi, i: (bi, 0, COL_BK)),
                  pl.BlockSpec((1, DSA_DIM, l), lambda bi, i: (bi, 0, 0)),
                  pl.BlockSpec((1, l, LANES), lambda bi, i: (bi, 0, COL_IKW))],
        out_specs=pl.BlockSpec((1, tq, DSA_HEADS * DSA_DIM), lambda bi, i: (bi, i, 0)),
        scratch_shapes=[pltpu.VMEM((l // tk, tk, tq), jnp.int16),
                        pltpu.VMEM((l // tk, tk, tq), jnp.int16)],
        compiler_params=_cparams(("parallel", "arbitrary")),
        name="dsa_attention",
    )(z3, z3, z3, z3, v_t, z3)


def _cmul(ar, ai, br, bi):
    return ar * br - ai * bi, ar * bi + ai * br


def _ssm_prep_kernel(lre_ref, lim_ref, lstep_ref, btr_ref, bti_ref, cr_ref, ci_ref,
                     w1_ref, mt_ref, tzt_ref, at_ref, *, tc, gb):
    c = SSM_C
    p = SSM_P
    for gi in range(gb):
        lr = lre_ref[gi].reshape(1, 1, p)
        li = lim_ref[gi].reshape(1, 1, p)
        step = jnp.exp(lstep_ref[gi]).reshape(1, 1, 1)
        mag = jnp.exp(lr * step)
        a_re, a_im = mag * jnp.cos(li * step), mag * jnp.sin(li * step)
        den = lr * lr + li * li
        nr, ni = a_re - 1.0, a_im
        f_re, f_im = (nr * lr + ni * li) / den, (ni * lr - nr * li) / den
        bt_r, bt_i = btr_ref[gi][None], bti_ref[gi][None]
        bb_re = f_re * bt_r - f_im * bt_i
        bb_im = f_re * bt_i + f_im * bt_r
        fr, fi = jnp.ones_like(a_re), jnp.zeros_like(a_im)
        rr, ri = fr, fi
        pr, pi = a_re, a_im
        n = 1
        while n < tc:
            xr, xi = _cmul(fr, fi, pr, pi)
            fr, fi = jnp.concatenate([fr, xr], 0), jnp.concatenate([fi, xi], 0)
            xr, xi = _cmul(rr, ri, pr, pi)
            rr, ri = jnp.concatenate([xr, rr], 0), jnp.concatenate([xi, ri], 0)
            pr, pi = _cmul(pr, pi, pr, pi)
            n *= 2
        at_ref[gi] = jnp.concatenate([pr[0], pi[0]], axis=1)
        wr, wi = _cmul(rr, ri, bb_re, bb_im)
        w1 = jnp.concatenate([wr, wi], axis=2).reshape(tc * c, 2 * p)
        w1_ref[gi] = w1.astype(w1_ref.dtype)
        f1r, f1i = _cmul(fr, fi, a_re, a_im)
        c_re, c_im = cr_ref[gi][None], ci_ref[gi][None]
        mr, mi = _cmul(f1r, f1i, c_re, c_im)
        mt_ref[gi] = jnp.concatenate([mr, -mi], axis=2).reshape(tc * c, 2 * p).astype(mt_ref.dtype)
        er, ei = _cmul(fr, fi, c_re, c_im)
        e2 = jnp.concatenate([er, -ei], axis=2).reshape(tc * c, 2 * p)
        bcat = jnp.concatenate([bb_re[0], bb_im[0]], axis=1)
        kflat = lax.dot_general(e2, bcat, (((1,), (1,)), ((), ())),
                                precision=lax.Precision.HIGHEST,
                                preferred_element_type=F32)
        per = LANES // c
        kpad = jnp.concatenate([kflat, jnp.zeros((tc * c, LANES - c), F32)], axis=1)
        rolled = [kpad] + [pltpu.roll(kpad, r * c, 1) for r in range(1, per)]
        for a in range(tc // per):
            tile = None
            for r in range(per):
                s = a * per + r
                piece = rolled[r][:(tc - s) * c, :]
                if s:
                    piece = jnp.concatenate([jnp.zeros((s * c, LANES), F32), piece], axis=0)
                tile = piece if tile is None else tile + piece
            tzt_ref[gi, :, a * LANES:(a + 1) * LANES] = tile.astype(tzt_ref.dtype)


def _ssm_prep(lam_re, lam_im, log_step, b_re, b_im, c_re, c_im, tc, gb):
    g, p = lam_re.shape
    c = SSM_C
    n = tc * c
    vecp = pl.BlockSpec((gb, 1, p), lambda i: (i, 0, 0))
    mat = pl.BlockSpec((gb, c, p), lambda i: (i, 0, 0))
    op = pl.BlockSpec((gb, n, 2 * p), lambda i: (i, 0, 0))
    return pl.pallas_call(
        functools.partial(_ssm_prep_kernel, tc=tc, gb=gb),
        out_shape=(jax.ShapeDtypeStruct((g, n, 2 * p), BF16),
                   jax.ShapeDtypeStruct((g, n, 2 * p), BF16),
                   jax.ShapeDtypeStruct((g, n, n), BF16),
                   jax.ShapeDtypeStruct((g, 1, 2 * p), F32)),
        grid=(g // gb,),
        in_specs=[vecp, vecp, pl.BlockSpec((gb, 1, 1), lambda i: (i, 0, 0)), mat, mat, mat, mat],
        out_specs=(op, op, pl.BlockSpec((gb, n, n), lambda i: (i, 0, 0)),
                   pl.BlockSpec((gb, 1, 2 * p), lambda i: (i, 0, 0))),
        compiler_params=_cparams(("parallel",)),
        name="ssm_prep",
    )(lam_re.reshape(g, 1, p), lam_im.reshape(g, 1, p), log_step.reshape(g, 1, 1),
      jnp.swapaxes(b_re, 1, 2), jnp.swapaxes(b_im, 1, 2), c_re, c_im)


def _ssm_fused_kernel(u_ref, w1_ref, mt_ref, tzt_ref, at_ref, d_ref, y_ref,
                      uf_ref, ufl_ref, s_ref, s2_ref, x_ref, yfl_ref, yt_ref, *, tc, gb):
    c = SSM_C
    l = uf_ref.shape[0]
    nch = l // tc
    uf_ref[...] = u_ref[0].astype(F32)
    for s in range(tc):
        tile = uf_ref[pl.ds(s, nch, stride=tc), :]
        for g in range(gb):
            ufl_ref[g, :, s * c:(s + 1) * c] = tile[:, g * c:(g + 1) * c]
    for g in range(gb):
        sg = jnp.dot(ufl_ref[g].astype(BF16), w1_ref[g], preferred_element_type=F32)
        s_ref[pl.ds(g, nch, stride=gb), :] = sg
    a = at_ref[...].reshape(gb, 2 * SSM_P)
    lane = lax.broadcasted_iota(I32, a.shape, 1)
    a_sw = pltpu.roll(a, SSM_P, 1)
    a_re = jnp.where(lane < SSM_P, a, a_sw)
    a_im_s = jnp.where(lane < SSM_P, -a_sw, a)

    s2_ref[...] = pltpu.roll(s_ref[...], SSM_P, 1)

    def step(ci, carry):
        x1, x2 = carry
        r0 = pl.multiple_of(ci * gb, gb)
        x_ref[pl.ds(r0, gb), :] = x1
        n1 = x1 * a_re + x2 * a_im_s + s_ref[pl.ds(r0, gb), :]
        n2 = x2 * a_re - x1 * a_im_s + s2_ref[pl.ds(r0, gb), :]
        return n1, n2

    zero = jnp.zeros((gb, 2 * SSM_P), F32)
    lax.fori_loop(0, nch, step, (zero, zero))
    for g in range(gb):
        u = ufl_ref[g]
        xg = x_ref[pl.ds(g, nch, stride=gb), :]
        y = _dot_nt(u.astype(BF16), tzt_ref[g]) + _dot_nt(xg.astype(BF16), mt_ref[g])
        yfl_ref[g] = jax.nn.gelu(y + u * d_ref[g])
    for t in range(tc):
        tile = jnp.concatenate([yfl_ref[g, :, t * c:(t + 1) * c] for g in range(gb)], axis=1)
        yt_ref[pl.ds(t, nch, stride=tc), :] = tile
    y_ref[0] = yt_ref[...].astype(y_ref.dtype)


def _s5(z3, ops, d_skip, tc, gb):
    w1, mt, tzt, at = ops
    b, l, _ = z3.shape
    c = SSM_C
    g = w1.shape[0]
    nch = l // tc
    n = tc * c
    p2 = 2 * SSM_P
    wl = gb * c
    dt = jnp.tile(d_skip.reshape(g, 1, c), (1, 1, tc)).astype(F32)
    col0 = COL_CU * LANES // wl
    return pl.pallas_call(
        functools.partial(_ssm_fused_kernel, tc=tc, gb=gb),
        out_shape=jax.ShapeDtypeStruct((b, l, g * c), BF16),
        grid=(g // gb, b),
        in_specs=[pl.BlockSpec((1, l, wl), lambda gi, bi: (bi, 0, col0 + gi)),
                  pl.BlockSpec((gb, n, p2), lambda gi, bi: (gi, 0, 0)),
                  pl.BlockSpec((gb, n, p2), lambda gi, bi: (gi, 0, 0)),
                  pl.BlockSpec((gb, n, n), lambda gi, bi: (gi, 0, 0)),
                  pl.BlockSpec((gb, 1, p2), lambda gi, bi: (gi, 0, 0)),
                  pl.BlockSpec((gb, 1, n), lambda gi, bi: (gi, 0, 0))],
        out_specs=pl.BlockSpec((1, l, wl), lambda gi, bi: (bi, 0, gi)),
        scratch_shapes=[pltpu.VMEM((l, wl), F32),
                        pltpu.VMEM((gb, nch, n), F32),
                        pltpu.VMEM((nch * gb, p2), F32),
                        pltpu.VMEM((nch * gb, p2), F32),
                        pltpu.VMEM((nch * gb, p2), F32),
                        pltpu.VMEM((gb, nch, n), F32),
                        pltpu.VMEM((l, wl), F32)],
        compiler_params=_cparams(("parallel", "arbitrary")),
        name="ssm_fused",
    )(z3, w1, mt, tzt, at, dt)


def _outproj_kernel(h_ref, ya_ref, yb_ref, ys_ref, wg_ref, w_ref, o_ref, *, tn):
    ka, kb = ya_ref.shape[1], yb_ref.shape[1]
    ys = ys_ref[...]
    gate = jnp.dot(ys, wg_ref[...], preferred_element_type=F32)
    yc = (ys.astype(F32) * jax.nn.sigmoid(gate)).astype(BF16)
    for j in range(h_ref.shape[1] // tn):
        sl = slice(j * tn, (j + 1) * tn)
        acc = jnp.dot(ya_ref[...], w_ref[:ka, sl], preferred_element_type=F32)
        acc += jnp.dot(yb_ref[...], w_ref[ka:ka + kb, sl], preferred_element_type=F32)
        acc += jnp.dot(yc, w_ref[ka + kb:, sl], preferred_element_type=F32)
        o_ref[:, sl] = h_ref[:, sl] + acc


def _outproj(h2d, ya, yb, ys, w_glu, w):
    t, d = h2d.shape
    ka, kb, kc = ya.shape[1], yb.shape[1], ys.shape[1]
    tm = min(t, 512)
    row = lambda m: (m, 0)
    return pl.pallas_call(
        functools.partial(_outproj_kernel, tn=min(d, 512)),
        out_shape=jax.ShapeDtypeStruct((t, d), F32),
        grid=(t // tm,),
        in_specs=[pl.BlockSpec((tm, d), row),
                  pl.BlockSpec((tm, ka), row),
                  pl.BlockSpec((tm, kb), row),
                  pl.BlockSpec((tm, kc), row),
                  pl.BlockSpec((kc, kc), lambda m: (0, 0), pipeline_mode=pl.Buffered(1)),
                  pl.BlockSpec((ka + kb + kc, d), lambda m: (0, 0), pipeline_mode=pl.Buffered(1))],
        out_specs=pl.BlockSpec((tm, d), row),
        compiler_params=_cparams(("parallel",)),
        name="glu_out_proj",
    )(h2d, ya, yb, ys, w_glu, w)


def _mlp_kernel(h_ref, g_ref, wu_ref, wd_ref, o_ref, xn_ref, acc_ref):
    f = pl.program_id(1)

    @pl.when(f == 0)
    def _():
        xn_ref[...] = _rms(h_ref[...], g_ref[...]).astype(BF16)
        acc_ref[...] = h_ref[...]

    hid = jnp.dot(xn_ref[...], wu_ref[...], preferred_element_type=F32)
    hid = jnp.square(jnp.maximum(hid, 0.0)).astype(BF16)
    acc_ref[...] += jnp.dot(hid, wd_ref[...], preferred_element_type=F32)

    @pl.when(f == pl.num_programs(1) - 1)
    def _():
        o_ref[...] = acc_ref[...]


def _mlp(h2d, g, w_up, w_down):
    t, d = h2d.shape
    ff = w_up.shape[1]
    tm = min(t, 512)
    tf = min(ff, MLP_TF)
    return pl.pallas_call(
        _mlp_kernel,
        out_shape=jax.ShapeDtypeStruct((t, d), F32),
        grid=(t // tm, ff // tf),
        in_specs=[pl.BlockSpec((tm, d), lambda m, f: (m, 0)),
                  pl.BlockSpec((1, d), lambda m, f: (0, 0)),
                  pl.BlockSpec((d, tf), lambda m, f: (0, f)),
                  pl.BlockSpec((tf, d), lambda m, f: (f, 0))],
        out_specs=pl.BlockSpec((tm, d), lambda m, f: (m, 0)),
        scratch_shapes=[pltpu.VMEM((tm, d), BF16), pltpu.VMEM((tm, d), F32)],
        compiler_params=_cparams(("parallel", "arbitrary")),
        name="mlp_relu2",
    )(h2d, g.reshape(1, d), w_up, w_down)


def _ple_kernel(h_ref, g_ref, p_ref, wg_ref, wp_ref, fg_ref, o_ref, *, tn, final):
    h = h_ref[...]
    xn = _rms(h, g_ref[...]).astype(BF16)
    pb = p_ref[...].astype(BF16)
    d = h.shape[1]
    for j in range(d // tn):
        sl = slice(j * tn, (j + 1) * tn)
        gate = jax.nn.sigmoid(jnp.dot(xn, wg_ref[:, sl], preferred_element_type=F32))
        proj = jnp.dot(pb, wp_ref[:, sl], preferred_element_type=F32)
        o_ref[:, sl] = h[:, sl] + gate * proj
    if final:
        o_ref[...] = _rms(o_ref[...], fg_ref[...])


def _ple(h2d, g, p2d, wg, wp, final_g, final):
    t, d = h2d.shape
    pd = p2d.shape[1]
    tm = min(t, 512)
    return pl.pallas_call(
        functools.partial(_ple_kernel, tn=min(d, 512), final=final),
        out_shape=jax.ShapeDtypeStruct((t, d), F32),
        grid=(t // tm,),
        in_specs=[pl.BlockSpec((tm, d), lambda m: (m, 0)),
                  pl.BlockSpec((1, d), lambda m: (0, 0)),
                  pl.BlockSpec((tm, pd), lambda m: (m, 0)),
                  pl.BlockSpec((d, d), lambda m: (0, 0), pipeline_mode=pl.Buffered(1)),
                  pl.BlockSpec((pd, d), lambda m: (0, 0), pipeline_mode=pl.Buffered(1)),
                  pl.BlockSpec((1, d), lambda m: (0, 0))],
        out_specs=pl.BlockSpec((tm, d), lambda m: (m, 0)),
        compiler_params=_cparams(("parallel",)),
        name="ple_gate",
    )(h2d, g.reshape(1, d), p2d, wg, wp, final_g.reshape(1, d))


SSM_CHUNK = 32
SSM_GROUP_BLOCK = 8


def kernel(x, p, positions, norm_mix_g, w_in, w_out, diff_lq1, diff_lk1, diff_lq2, diff_lk2,
           diff_subln_g, ssm_lambda_re, ssm_lambda_im, ssm_log_step, ssm_B_re, ssm_B_im,
           ssm_C_re, ssm_C_im, ssm_D, ssm_w_glu, norm_mlp_g, w_up, w_down, norm_ple_g,
           w_ple_gate, w_ple_proj, final_g):
    b, l, d = x.shape
    depth = w_in.shape[0]
    t = b * l
    tc = min(SSM_CHUNK, l)
    tabs = _rope_tables(positions)
    h = x.reshape(t, d)
    for i in range(depth):
        lambda_init = 0.8 - 0.6 * math.exp(-0.3 * i)
        z = _inproj(h, norm_mix_g[i], _permute_w_in(w_in[i]), tabs)
        z3 = z.reshape(b, l, Z_WIDTH)
        y_a = _diff_attn(z3, diff_lq1[i], diff_lk1[i], diff_lq2[i], diff_lk2[i],
                         diff_subln_g[i], lambda_init)
        y_b = _dsa_attn(z3)
        ops = _ssm_prep(ssm_lambda_re[i], ssm_lambda_im[i], ssm_log_step[i], ssm_B_re[i],
                        ssm_B_im[i], ssm_C_re[i], ssm_C_im[i], tc, SSM_GROUP_BLOCK)
        y_s = _s5(z3, ops, ssm_D[i], tc, SSM_GROUP_BLOCK)
        h = _outproj(h, y_a.reshape(t, -1), y_b.reshape(t, -1), y_s.reshape(t, -1),
                     ssm_w_glu[i].astype(BF16), w_out[i].astype(BF16))
        h = _mlp(h, norm_mlp_g[i], w_up[i].astype(BF16), w_down[i].astype(BF16))
        h = _ple(h, norm_ple_g[i], p[i].reshape(t, -1), w_ple_gate[i].astype(BF16),
                 w_ple_proj[i].astype(BF16), final_g, final=(i == depth - 1))
    return h.reshape(b, l, d)
```

```python
import functools
import math

import jax
import jax.numpy as jnp
from jax import lax
from jax.experimental import pallas as pl
from jax.experimental.pallas import tpu as pltpu

F32 = jnp.float32
BF16 = jnp.bfloat16
I32 = jnp.int32

LANES = 128
EPS = 1e-6
ROPE_THETA = 10000.0

DIFF_QK = 64
DIFF_V = 128
DIFF_HEADS = 4
DSA_DIM = 128
DSA_HEADS = 4
IDX_HEADS = 8
IDX_DIM = 64
TOPK_MAX = 256
SSM_C = 16
SSM_P = 64

COL_AQ, COL_AK, COL_IQ, COL_BQ, COL_AV = 0, 4, 8, 12, 16
COL_BK, COL_BV, COL_IKW, COL_CU = 20, 21, 22, 24
Z_WIDTH = 32 * LANES
IN_TN = 512

MLP_TF = 1024
DIFF_TQ = 512
DIFF_TK = 512
DSA_TQ = 256
DSA_TK = 512
NEG_BIG = -1e30
LOG2E = math.log2(math.e)
VMEM_LIMIT = 56 * 1024 * 1024


def _cparams(sem):
    return pltpu.CompilerParams(dimension_semantics=sem, vmem_limit_bytes=VMEM_LIMIT)


def _rope_tables_kernel(pos_ref, f64_ref, f128_ref, c64_ref, s64_ref, c128_ref, s128_ref):
    pos = pos_ref[...]
    a64 = pos * f64_ref[...]
    a128 = pos * f128_ref[...]
    lane = lax.broadcasted_iota(I32, a64.shape, 1)
    c64_ref[...] = jnp.cos(a64)
    sn = jnp.sin(a64)
    s64_ref[...] = jnp.where((lane & 32) == 0, -sn, sn)
    c128_ref[...] = jnp.cos(a128)
    sn = jnp.sin(a128)
    s128_ref[...] = jnp.where((lane & 64) == 0, -sn, sn)


def _rope_tables(positions):
    t = positions.size
    pos = positions.reshape(t, 1).astype(F32)
    fr64 = ROPE_THETA ** (-jnp.arange(0, 64, 2, dtype=F32) / 64)
    fr128 = ROPE_THETA ** (-jnp.arange(0, 128, 2, dtype=F32) / 128)
    f64 = jnp.tile(fr64, 4).reshape(1, LANES)
    f128 = jnp.tile(fr128, 2).reshape(1, LANES)
    tm = min(t, 1024)
    tab = jax.ShapeDtypeStruct((t, LANES), F32)
    row = pl.BlockSpec((tm, LANES), lambda i: (i, 0))
    return pl.pallas_call(
        _rope_tables_kernel,
        out_shape=(tab, tab, tab, tab),
        grid=(t // tm,),
        in_specs=[pl.BlockSpec((tm, 1), lambda i: (i, 0)),
                  pl.BlockSpec((1, LANES), lambda i: (0, 0)),
                  pl.BlockSpec((1, LANES), lambda i: (0, 0))],
        out_specs=(row, row, row, row),
        compiler_params=_cparams(("parallel",)),
        name="rope_tables",
    )(pos, f64, f128)


def _rms(x, g):
    ms = jnp.mean(x * x, axis=-1, keepdims=True)
    return x * lax.rsqrt(ms + EPS) * g


def _swap_halves(a, half):
    lane = lax.broadcasted_iota(I32, a.shape, 1)
    return jnp.where((lane & half) == 0,
                     pltpu.roll(a, LANES - half, 1), pltpu.roll(a, half, 1))


def _inproj_kernel(x_ref, g_ref, w_ref, c64_ref, s64_ref, c128_ref, s128_ref, z_ref, *, tn):
    xn = _rms(x_ref[...], g_ref[...]).astype(BF16)

    def rope(a, half):
        if half == 32:
            return a * c64_ref[...] + _swap_halves(a, 32) * s64_ref[...]
        return a * c128_ref[...] + _swap_halves(a, 64) * s128_ref[...]

    def epilogue(col, a):
        if col < COL_BQ:
            return rope(a, 32)
        if col < COL_AV or col == COL_BK:
            return rope(a, 64)
        if col == COL_IKW:
            lane = lax.broadcasted_iota(I32, a.shape, 1)
            return jnp.where(lane < IDX_DIM, rope(a, 32), a)
        return a

    groups = tn // LANES
    for n in range(z_ref.shape[1] // tn):
        acc = jnp.dot(xn, w_ref[:, n * tn:(n + 1) * tn], preferred_element_type=F32)
        for j in range(groups):
            col = n * groups + j
            z_ref[:, col * LANES:(col + 1) * LANES] = epilogue(
                col, acc[:, j * LANES:(j + 1) * LANES]).astype(z_ref.dtype)


def _inproj(h2d, g, w, tabs):
    t, d = h2d.shape
    tm = min(t, 512)
    c64, s64, c128, s128 = tabs
    row = lambda m: (m, 0)
    tab = pl.BlockSpec((tm, LANES), row)
    return pl.pallas_call(
        functools.partial(_inproj_kernel, tn=IN_TN),
        out_shape=jax.ShapeDtypeStruct((t, Z_WIDTH), BF16),
        grid=(t // tm,),
        in_specs=[pl.BlockSpec((tm, d), row),
                  pl.BlockSpec((1, d), lambda m: (0, 0)),
                  pl.BlockSpec((d, Z_WIDTH), lambda m: (0, 0), pipeline_mode=pl.Buffered(1)),
                  tab, tab, tab, tab],
        out_specs=pl.BlockSpec((tm, Z_WIDTH), row),
        compiler_params=_cparams(("parallel",)),
        name="norm_inproj_rope",
    )(h2d, g.reshape(1, d), w, c64, s64, c128, s128)


def _permute_w_in(w):
    d = w.shape[0]
    pad = jnp.zeros((d, 56 + LANES), w.dtype)
    return jnp.concatenate(
        [w[:, 0:512], w[:, 512:1024], w[:, 2304:2816], w[:, 1536:2048], w[:, 1024:1536],
         w[:, 2048:2176], w[:, 2176:2304], w[:, 2816:2888], pad, w[:, 2888:3912]],
        axis=1).astype(BF16)


def _flash_step(carry, s, vb):
    m, l, acc = carry
    m_new = jnp.maximum(m, jnp.max(s, axis=1, keepdims=True))
    alpha = jnp.exp2(m - m_new)
    p = jnp.exp2(s - m_new)
    l = alpha * l + jnp.sum(p, axis=1, keepdims=True)
    acc = alpha * acc + jnp.dot(p.astype(BF16), vb, preferred_element_type=F32)
    return m_new, l, acc


def _flash_init(rows, dv):
    return (jnp.full((rows, 1), NEG_BIG, F32), jnp.zeros((rows, 1), F32),
            jnp.zeros((rows, dv), F32))


def _dot_nt(a, b):
    return lax.dot_general(a, b, (((1,), (1,)), ((), ())), preferred_element_type=F32)


def _diff_attn_kernel(q_ref, k_ref, vt_ref, lq1_ref, lk1_ref, lq2_ref, lk2_ref, g_ref, o_ref,
                      *, tq, tk, lambda_init):
    i = pl.program_id(2)
    q_t = jnp.transpose(q_ref[0].astype(F32)) * (DIFF_QK ** -0.5)
    dim = lax.broadcasted_iota(I32, q_t.shape, 0)
    q2 = jnp.concatenate([jnp.where(dim < DIFF_QK, q_t, 0.0),
                          jnp.where(dim >= DIFF_QK, q_t, 0.0)], axis=1).astype(BF16)

    def block(j, carry, diag):
        m, l, acc = carry
        kb = k_ref[0, pl.ds(j * tk, tk), :]
        vtb = vt_ref[0, 0, :, pl.ds(j * tk, tk)]
        s = jnp.dot(kb, q2, preferred_element_type=F32) * LOG2E
        if diag:
            key = j * tk + lax.broadcasted_iota(I32, s.shape, 0)
            qi = lax.broadcasted_iota(I32, s.shape, 1)
            qi = i * tq + jnp.where(qi >= tq, qi - tq, qi)
            s = jnp.where(key <= qi, s, NEG_BIG)
        m_new = jnp.maximum(m, jnp.max(s, axis=0, keepdims=True))
        alpha = jnp.exp2(m - m_new)
        p = jnp.exp2(s - m_new)
        l = alpha * l + jnp.sum(p, axis=0, keepdims=True)
        acc = alpha * acc + jnp.dot(vtb, p.astype(BF16), preferred_element_type=F32)
        return m_new, l, acc

    init = (jnp.full((1, 2 * tq), NEG_BIG, F32), jnp.zeros((1, 2 * tq), F32),
            jnp.zeros((DIFF_V, 2 * tq), F32))
    nfull = (i * tq) // tk
    carry = lax.fori_loop(0, nfull, lambda j, c: block(j, c, False), init)
    _, l, acc = block(nfull, carry, True)
    o = acc / l
    lam = (jnp.exp(jnp.sum(lq1_ref[...] * lk1_ref[...], keepdims=True))
           - jnp.exp(jnp.sum(lq2_ref[...] * lk2_ref[...], keepdims=True)) + lambda_init)
    out = jnp.transpose(o[:, :tq] - lam * o[:, tq:])
    out = _rms(out, g_ref[...]) * (1.0 - lambda_init)
    o_ref[0] = out.astype(o_ref.dtype)


def _diff_attn(z3, lq1, lk1, lq2, lk2, subln_g, lambda_init):
    b, l, _ = z3.shape
    tq = min(l, DIFF_TQ)
    tk = min(l, DIFF_TK)
    vec = pl.BlockSpec((1, DIFF_QK), lambda bi, h, i: (0, 0))
    v = z3[:, :, COL_AV * LANES:(COL_AV + DIFF_HEADS) * LANES]
    v_t = v.reshape(b, l, DIFF_HEADS, DIFF_V).transpose(0, 2, 3, 1)
    return pl.pallas_call(
        functools.partial(_diff_attn_kernel, tq=tq, tk=tk, lambda_init=lambda_init),
        out_shape=jax.ShapeDtypeStruct((b, l, DIFF_HEADS * DIFF_V), BF16),
        grid=(b, DIFF_HEADS, l // tq),
        in_specs=[pl.BlockSpec((1, tq, LANES), lambda bi, h, i: (bi, i, COL_AQ + h)),
                  pl.BlockSpec((1, l, LANES), lambda bi, h, i: (bi, 0, COL_AK + h)),
                  pl.BlockSpec((1, 1, DIFF_V, l), lambda bi, h, i: (bi, h, 0, 0)),
                  vec, vec, vec, vec,
                  pl.BlockSpec((1, DIFF_V), lambda bi, h, i: (0, 0))],
        out_specs=pl.BlockSpec((1, tq, DIFF_V), lambda bi, h, i: (bi, i, h)),
        compiler_params=_cparams(("parallel", "parallel", "arbitrary")),
        name="diff_attention",
    )(z3, z3, v_t, lq1.reshape(1, -1), lk1.reshape(1, -1), lq2.reshape(1, -1),
      lk2.reshape(1, -1), subln_g.reshape(1, -1))


def _sortable_key(s):
    bits = lax.bitcast_convert_type(s, I32)
    return bits ^ ((bits >> 31) & jnp.int32(0x7FFFFFFF))


def _dsa_kernel(bq_ref, iq_ref, iwq_ref, bk_ref, bvt_ref, ik_ref, o_ref, hi_ref, lo_ref,
                *, tq, tk, topk, seq_bits):
    i = pl.program_id(1)
    nkb = ((i + 1) * tq + tk - 1) // tk
    low16 = -(2 ** 15)
    i16 = jnp.int16

    def as16(v):
        return v.astype(i16)

    krow = lax.broadcasted_iota(I32, (tk, tq), 0)
    qcol = i * tq + lax.broadcasted_iota(I32, (tk, tq), 1)

    iq_t = jnp.transpose(iq_ref[0].astype(F32))
    iq_all = jnp.concatenate([iq_t[h * IDX_DIM:(h + 1) * IDX_DIM, :] for h in range(IDX_HEADS)],
                             axis=1).astype(BF16)
    iw_t = jnp.transpose(iwq_ref[0].astype(F32))[IDX_DIM:IDX_DIM + IDX_HEADS, :]
    iw_t = iw_t * (IDX_HEADS ** -0.5) * (IDX_DIM ** -0.5)

    def score_block(j, _):
        ikb = ik_ref[0, pl.ds(j * tk, tk), :][:, :IDX_DIM]
        r_all = jnp.dot(ikb, iq_all, preferred_element_type=F32)
        sc = jnp.zeros((tk, tq), F32)
        for h in range(IDX_HEADS):
            sc = sc + jnp.maximum(r_all[:, h * tq:(h + 1) * tq], 0.0) * iw_t[h:h + 1, :]
        sc = jnp.where(j * tk + krow <= qcol, sc + 0.0, -jnp.inf)
        key = _sortable_key(sc)
        hi_ref[j] = as16(key >> 16)
        lo_ref[j] = as16((key & 0xFFFF) + low16)
        return 0

    lax.fori_loop(0, nkb, score_block, 0)

    one16, zero16 = jnp.ones((), i16), jnp.zeros((), i16)

    def counts(block_fn, n):
        def blk(j, cnts):
            out = []
            for pred, cnt in zip(block_fn(j), cnts):
                m = jnp.where(pred, one16, zero16)
                parts = [m[r * 16:(r + 1) * 16, :] for r in range(tk // 16)]
                while len(parts) > 1:
                    parts = [a + b for a, b in zip(parts[::2], parts[1::2])]
                out.append(cnt + parts[0])
            return tuple(out)
        cnts = lax.fori_loop(0, nkb, blk, tuple(jnp.zeros((16, tq), i16) for _ in range(n)))
        return [jnp.sum(cnt.astype(I32), axis=0, keepdims=True) for cnt in cnts]

    def count(pred_fn):
        return counts(lambda j: (pred_fn(j),), 1)[0]

    def radix_select(ref, target, c0=None):
        if c0 is None:
            c0 = count(lambda j: ref[j] >= zero16)
        thr = jnp.where(c0 >= target, 0, low16).astype(I32)

        def bit_step(it, thr):
            cand = thr | (jnp.int32(1) << (14 - it))
            c16 = as16(cand)
            return jnp.where(count(lambda j: ref[j] >= c16) >= target, cand, thr)

        return lax.fori_loop(0, 15, bit_step, thr)

    t_hi = as16(radix_select(hi_ref, topk))
    def above_and_mask_lo(j):
        hi = hi_ref[j]
        lo = jnp.where(hi == t_hi, lo_ref[j], jnp.int16(low16))
        lo_ref[j] = lo
        return hi > t_hi, lo >= zero16

    c_above, c0_lo = counts(above_and_mask_lo, 2)
    need_lo = topk - c_above
    t_lo = as16(radix_select(lo_ref, need_lo, c0_lo))

    def in_tie(j):
        return (hi_ref[j] == t_hi) & (lo_ref[j] == t_lo)

    c_gt_lo, c_tie = counts(lambda j: (lo_ref[j] > t_lo, in_tie(j)), 2)
    need = need_lo - c_gt_lo
    has_excess = jnp.max(jnp.where(c_tie > need, 1, 0)) > 0
    krow16 = as16(krow)

    def tie_limit():
        def step(it, q):
            cand = q + (jnp.int32(1) << (seq_bits - 1 - it))
            c16 = as16(cand)
            c = count(lambda j: in_tie(j) & (krow16 + as16(j * tk) < c16))
            return jnp.where(c < need, cand, q)
        return lax.fori_loop(0, seq_bits, step, jnp.zeros((1, tq), I32))

    jlim16 = as16(lax.cond(has_excess, tie_limit, lambda: jnp.full((1, tq), 2 ** seq_bits, I32)))

    q_t = jnp.transpose(bq_ref[0].astype(F32))
    q_all = jnp.concatenate([q_t[h * DSA_DIM:(h + 1) * DSA_DIM, :] for h in range(DSA_HEADS)],
                            axis=1).astype(BF16)
    scale = DSA_DIM ** -0.5 * LOG2E
    zero_b, neg_b = jnp.zeros((), BF16), jnp.full((), NEG_BIG, BF16)
    qcol16 = as16(qcol)

    def attn_block(j, carry):
        kb = bk_ref[0, pl.ds(j * tk, tk), :]
        vtb = bvt_ref[0, :, pl.ds(j * tk, tk)]
        hi, lo = hi_ref[j], lo_ref[j]
        kidx = krow16 + as16(j * tk)
        sel = (hi > t_hi) | ((hi == t_hi) & ((lo > t_lo) | ((lo == t_lo) & (kidx <= jlim16))))
        sel = sel & (kidx <= qcol16)
        bias = jnp.where(sel, zero_b, neg_b).astype(F32)
        m, l, acc = carry
        s = (jnp.dot(kb, q_all, preferred_element_type=F32) * scale
             + jnp.concatenate([bias] * DSA_HEADS, axis=1))
        m_new = jnp.maximum(m, jnp.max(s, axis=0, keepdims=True))
        alpha = jnp.exp2(m - m_new)
        p = jnp.exp2(s - m_new)
        l = alpha * l + jnp.sum(p, axis=0, keepdims=True)
        pb = p.astype(BF16)
        pv = jnp.concatenate(
            [jnp.dot(vtb, pb[:, h * tq:(h + 1) * tq], preferred_element_type=F32)
             for h in range(DSA_HEADS)], axis=1)
        return m_new, l, alpha * acc + pv

    wide_q = DSA_HEADS * tq
    init = (jnp.full((1, wide_q), NEG_BIG, F32), jnp.zeros((1, wide_q), F32),
            jnp.zeros((DSA_DIM, wide_q), F32))
    _, l, acc = lax.fori_loop(0, nkb, attn_block, init)
    o = acc / l
    for h in range(DSA_HEADS):
        o_ref[0, :, h * DSA_DIM:(h + 1) * DSA_DIM] = jnp.transpose(
            o[:, h * tq:(h + 1) * tq]).astype(o_ref.dtype)


def _dsa_attn(z3):
    b, l, _ = z3.shape
    tq = min(l, DSA_TQ)
    tk = min(l, DSA_TK)
    topk = min(TOPK_MAX, l // 4)
    seq_bits = max(1, (l - 1).bit_length())
    wide = 4
    v_t = jnp.swapaxes(z3[:, :, COL_BV * LANES:(COL_BV + 1) * LANES], 1, 2)
    return pl.pallas_call(
        functools.partial(_dsa_kernel, tq=tq, tk=tk, topk=topk, seq_bits=seq_bits),
        out_shape=jax.ShapeDtypeStruct((b, l, DSA_HEADS * DSA_DIM), BF16),
        grid=(b, l // tq),
        in_specs=[pl.BlockSpec((1, tq, 4 * LANES), lambda bi, i: (bi, i, COL_BQ // wide)),
                  pl.BlockSpec((1, tq, 4 * LANES), lambda bi, i: (bi, i, COL_IQ // wide)),
                  pl.BlockSpec((1, tq, LANES), lambda bi, i: (bi, i, COL_IKW)),
                  pl.BlockSpec((1, l, LANES), lambda bi, i: (bi, 0, COL_BK)),
                  pl.BlockSpec((1, DSA_DIM, l), lambda bi, i: (bi, 0, 0)),
                  pl.BlockSpec((1, l, LANES), lambda bi, i: (bi, 0, COL_IKW))],
        out_specs=pl.BlockSpec((1, tq, DSA_HEADS * DSA_DIM), lambda bi, i: (bi, i, 0)),
        scratch_shapes=[pltpu.VMEM((l // tk, tk, tq), jnp.int16),
                        pltpu.VMEM((l // tk, tk, tq), jnp.int16)],
        compiler_params=_cparams(("parallel", "arbitrary")),
        name="dsa_attention",
    )(z3, z3, z3, z3, v_t, z3)


def _cmul(ar, ai, br, bi):
    return ar * br - ai * bi, ar * bi + ai * br


def _ssm_prep_kernel(lre_ref, lim_ref, lstep_ref, btr_ref, bti_ref, cr_ref, ci_ref,
                     w1_ref, mt_ref, tzt_ref, at_ref, *, tc, gb):
    c = SSM_C
    p = SSM_P
    for gi in range(gb):
        lr = lre_ref[gi].reshape(1, 1, p)
        li = lim_ref[gi].reshape(1, 1, p)
        step = jnp.exp(lstep_ref[gi]).reshape(1, 1, 1)
        mag = jnp.exp(lr * step)
        a_re, a_im = mag * jnp.cos(li * step), mag * jnp.sin(li * step)
        den = lr * lr + li * li
        nr, ni = a_re - 1.0, a_im
        f_re, f_im = (nr * lr + ni * li) / den, (ni * lr - nr * li) / den
        bt_r, bt_i = btr_ref[gi][None], bti_ref[gi][None]
        bb_re = f_re * bt_r - f_im * bt_i
        bb_im = f_re * bt_i + f_im * bt_r
        fr, fi = jnp.ones_like(a_re), jnp.zeros_like(a_im)
        rr, ri = fr, fi
        pr, pi = a_re, a_im
        n = 1
        while n < tc:
            xr, xi = _cmul(fr, fi, pr, pi)
            fr, fi = jnp.concatenate([fr, xr], 0), jnp.concatenate([fi, xi], 0)
            xr, xi = _cmul(rr, ri, pr, pi)
            rr, ri = jnp.concatenate([xr, rr], 0), jnp.concatenate([xi, ri], 0)
            pr, pi = _cmul(pr, pi, pr, pi)
            n *= 2
        at_ref[gi] = jnp.concatenate([pr[0], pi[0]], axis=1)
        wr, wi = _cmul(rr, ri, bb_re, bb_im)
        w1 = jnp.concatenate([wr, wi], axis=2).reshape(tc * c, 2 * p)
        w1_ref[gi] = w1.astype(w1_ref.dtype)
        f1r, f1i = _cmul(fr, fi, a_re, a_im)
        c_re, c_im = cr_ref[gi][None], ci_ref[gi][None]
        mr, mi = _cmul(f1r, f1i, c_re, c_im)
        mt_ref[gi] = jnp.concatenate([mr, -mi], axis=2).reshape(tc * c, 2 * p).astype(mt_ref.dtype)
        er, ei = _cmul(fr, fi, c_re, c_im)
        e2 = jnp.concatenate([er, -ei], axis=2).reshape(tc * c, 2 * p)
        bcat = jnp.concatenate([bb_re[0], bb_im[0]], axis=1)
        kflat = lax.dot_general(e2, bcat, (((1,), (1,)), ((), ())),
                                precision=lax.Precision.HIGHEST,
                                preferred_element_type=F32)
        per = LANES // c
        kpad = jnp.concatenate([kflat, jnp.zeros((tc * c, LANES - c), F32)], axis=1)
        rolled = [kpad] + [pltpu.roll(kpad, r * c, 1) for r in range(1, per)]
        for a in range(tc // per):
            tile = None
            for r in range(per):
                s = a * per + r
                piece = rolled[r][:(tc - s) * c, :]
                if s:
                    piece = jnp.concatenate([jnp.zeros((s * c, LANES), F32), piece], axis=0)
                tile = piece if tile is None else tile + piece
            tzt_ref[gi, :, a * LANES:(a + 1) * LANES] = tile.astype(tzt_ref.dtype)


def _ssm_prep(lam_re, lam_im, log_step, b_re, b_im, c_re, c_im, tc, gb):
    g, p = lam_re.shape
    c = SSM_C
    n = tc * c
    vecp = pl.BlockSpec((gb, 1, p), lambda i: (i, 0, 0))
    mat = pl.BlockSpec((gb, c, p), lambda i: (i, 0, 0))
    op = pl.BlockSpec((gb, n, 2 * p), lambda i: (i, 0, 0))
    return pl.pallas_call(
        functools.partial(_ssm_prep_kernel, tc=tc, gb=gb),
        out_shape=(jax.ShapeDtypeStruct((g, n, 2 * p), BF16),
                   jax.ShapeDtypeStruct((g, n, 2 * p), BF16),
                   jax.ShapeDtypeStruct((g, n, n), BF16),
                   jax.ShapeDtypeStruct((g, 1, 2 * p), F32)),
        grid=(g // gb,),
        in_specs=[vecp, vecp, pl.BlockSpec((gb, 1, 1), lambda i: (i, 0, 0)), mat, mat, mat, mat],
        out_specs=(op, op, pl.BlockSpec((gb, n, n), lambda i: (i, 0, 0)),
                   pl.BlockSpec((gb, 1, 2 * p), lambda i: (i, 0, 0))),
        compiler_params=_cparams(("parallel",)),
        name="ssm_prep",
    )(lam_re.reshape(g, 1, p), lam_im.reshape(g, 1, p), log_step.reshape(g, 1, 1),
      jnp.swapaxes(b_re, 1, 2), jnp.swapaxes(b_im, 1, 2), c_re, c_im)


def _ssm_fused_kernel(u_ref, w1_ref, mt_ref, tzt_ref, at_ref, d_ref, y_ref,
                      uf_ref, ufl_ref, s_ref, s2_ref, x_ref, yfl_ref, yt_ref, *, tc, gb):
    c = SSM_C
    l = uf_ref.shape[0]
    nch = l // tc
    uf_ref[...] = u_ref[0].astype(F32)
    for s in range(tc):
        tile = uf_ref[pl.ds(s, nch, stride=tc), :]
        for g in range(gb):
            ufl_ref[g, :, s * c:(s + 1) * c] = tile[:, g * c:(g + 1) * c]
    for g in range(gb):
        sg = jnp.dot(ufl_ref[g].astype(BF16), w1_ref[g], preferred_element_type=F32)
        s_ref[pl.ds(g, nch, stride=gb), :] = sg
    a = at_ref[...].reshape(gb, 2 * SSM_P)
    lane = lax.broadcasted_iota(I32, a.shape, 1)
    a_sw = pltpu.roll(a, SSM_P, 1)
    a_re = jnp.where(lane < SSM_P, a, a_sw)
    a_im_s = jnp.where(lane < SSM_P, -a_sw, a)

    s2_ref[...] = pltpu.roll(s_ref[...], SSM_P, 1)

    def step(ci, carry):
        x1, x2 = carry
        r0 = pl.multiple_of(ci * gb, gb)
        x_ref[pl.ds(r0, gb), :] = x1
        n1 = x1 * a_re + x2 * a_im_s + s_ref[pl.ds(r0, gb), :]
        n2 = x2 * a_re - x1 * a_im_s + s2_ref[pl.ds(r0, gb), :]
        return n1, n2

    zero = jnp.zeros((gb, 2 * SSM_P), F32)
    lax.fori_loop(0, nch, step, (zero, zero))
    for g in range(gb):
        u = ufl_ref[g]
        xg = x_ref[pl.ds(g, nch, stride=gb), :]
        y = _dot_nt(u.astype(BF16), tzt_ref[g]) + _dot_nt(xg.astype(BF16), mt_ref[g])
        yfl_ref[g] = jax.nn.gelu(y + u * d_ref[g])
    for t in range(tc):
        tile = jnp.concatenate([yfl_ref[g, :, t * c:(t + 1) * c] for g in range(gb)], axis=1)
        yt_ref[pl.ds(t, nch, stride=tc), :] = tile
    y_ref[0] = yt_ref[...].astype(y_ref.dtype)


def _s5(z3, ops, d_skip, tc, gb):
    w1, mt, tzt, at = ops
    b, l, _ = z3.shape
    c = SSM_C
    g = w1.shape[0]
    nch = l // tc
    n = tc * c
    p2 = 2 * SSM_P
    wl = gb * c
    dt = jnp.tile(d_skip.reshape(g, 1, c), (1, 1, tc)).astype(F32)
    col0 = COL_CU * LANES // wl
    return pl.pallas_call(
        functools.partial(_ssm_fused_kernel, tc=tc, gb=gb),
        out_shape=jax.ShapeDtypeStruct((b, l, g * c), BF16),
        grid=(g // gb, b),
        in_specs=[pl.BlockSpec((1, l, wl), lambda gi, bi: (bi, 0, col0 + gi)),
                  pl.BlockSpec((gb, n, p2), lambda gi, bi: (gi, 0, 0)),
                  pl.BlockSpec((gb, n, p2), lambda gi, bi: (gi, 0, 0)),
                  pl.BlockSpec((gb, n, n), lambda gi, bi: (gi, 0, 0)),
                  pl.BlockSpec((gb, 1, p2), lambda gi, bi: (gi, 0, 0)),
                  pl.BlockSpec((gb, 1, n), lambda gi, bi: (gi, 0, 0))],
        out_specs=pl.BlockSpec((1, l, wl), lambda gi, bi: (bi, 0, gi)),
        scratch_shapes=[pltpu.VMEM((l, wl), F32),
                        pltpu.VMEM((gb, nch, n), F32),
                        pltpu.VMEM((nch * gb, p2), F32),
                        pltpu.VMEM((nch * gb, p2), F32),
                        pltpu.VMEM((nch * gb, p2), F32),
                        pltpu.VMEM((gb, nch, n), F32),
                        pltpu.VMEM((l, wl), F32)],
        compiler_params=_cparams(("parallel", "arbitrary")),
        name="ssm_fused",
    )(z3, w1, mt, tzt, at, dt)


def _outproj_kernel(h_ref, ya_ref, yb_ref, ys_ref, wg_ref, w_ref, gn_ref, o_ref, xn_ref, *, tn):
    ka, kb = ya_ref.shape[1], yb_ref.shape[1]
    ys = ys_ref[...]
    gate = jnp.dot(ys, wg_ref[...], preferred_element_type=F32)
    yc = (ys.astype(F32) * jax.nn.sigmoid(gate)).astype(BF16)
    for j in range(h_ref.shape[1] // tn):
        sl = slice(j * tn, (j + 1) * tn)
        acc = jnp.dot(ya_ref[...], w_ref[:ka, sl], preferred_element_type=F32)
        acc += jnp.dot(yb_ref[...], w_ref[ka:ka + kb, sl], preferred_element_type=F32)
        acc += jnp.dot(yc, w_ref[ka + kb:, sl], preferred_element_type=F32)
        o_ref[:, sl] = h_ref[:, sl] + acc
    xn_ref[...] = _rms(o_ref[...], gn_ref[...]).astype(xn_ref.dtype)


def _outproj(h2d, ya, yb, ys, w_glu, w, g_next):
    t, d = h2d.shape
    ka, kb, kc = ya.shape[1], yb.shape[1], ys.shape[1]
    tm = min(t, 512)
    row = lambda m: (m, 0)
    return pl.pallas_call(
        functools.partial(_outproj_kernel, tn=min(d, 512)),
        out_shape=(jax.ShapeDtypeStruct((t, d), F32), jax.ShapeDtypeStruct((t, d), BF16)),
        grid=(t // tm,),
        in_specs=[pl.BlockSpec((tm, d), row),
                  pl.BlockSpec((tm, ka), row),
                  pl.BlockSpec((tm, kb), row),
                  pl.BlockSpec((tm, kc), row),
                  pl.BlockSpec((kc, kc), lambda m: (0, 0), pipeline_mode=pl.Buffered(1)),
                  pl.BlockSpec((ka + kb + kc, d), lambda m: (0, 0), pipeline_mode=pl.Buffered(1)),
                  pl.BlockSpec((1, d), lambda m: (0, 0))],
        out_specs=(pl.BlockSpec((tm, d), row), pl.BlockSpec((tm, d), row)),
        compiler_params=_cparams(("parallel",)),
        name="glu_out_proj",
    )(h2d, ya, yb, ys, w_glu, w, g_next.reshape(1, d))


def _mlp_kernel(h_ref, xn_ref, wu_ref, wd_ref, o_ref, acc_ref):
    f = pl.program_id(1)

    @pl.when(f == 0)
    def _():
        acc_ref[...] = h_ref[...]

    hid = jnp.dot(xn_ref[...], wu_ref[...], preferred_element_type=F32)
    hid = jnp.square(jnp.maximum(hid, 0.0)).astype(BF16)
    acc_ref[...] += jnp.dot(hid, wd_ref[...], preferred_element_type=F32)

    @pl.when(f == pl.num_programs(1) - 1)
    def _():
        o_ref[...] = acc_ref[...]


def _mlp(h2d, xn2d, w_up, w_down):
    t, d = h2d.shape
    ff = w_up.shape[1]
    tm = min(t, 512)
    tf = min(ff, MLP_TF)
    return pl.pallas_call(
        _mlp_kernel,
        out_shape=jax.ShapeDtypeStruct((t, d), F32),
        grid=(t // tm, ff // tf),
        in_specs=[pl.BlockSpec((tm, d), lambda m, f: (m, 0)),
                  pl.BlockSpec((tm, d), lambda m, f: (m, 0)),
                  pl.BlockSpec((d, tf), lambda m, f: (0, f)),
                  pl.BlockSpec((tf, d), lambda m, f: (f, 0))],
        out_specs=pl.BlockSpec((tm, d), lambda m, f: (m, 0)),
        scratch_shapes=[pltpu.VMEM((tm, d), F32)],
        compiler_params=_cparams(("parallel", "arbitrary")),
        name="mlp_relu2",
    )(h2d, xn2d, w_up, w_down)


def _ple_kernel(h_ref, g_ref, p_ref, wg_ref, wp_ref, fg_ref, o_ref, *, tn, final):
    h = h_ref[...]
    xn = _rms(h, g_ref[...]).astype(BF16)
    pb = p_ref[...].astype(BF16)
    d = h.shape[1]
    for j in range(d // tn):
        sl = slice(j * tn, (j + 1) * tn)
        gate = jax.nn.sigmoid(jnp.dot(xn, wg_ref[:, sl], preferred_element_type=F32))
        proj = jnp.dot(pb, wp_ref[:, sl], preferred_element_type=F32)
        o_ref[:, sl] = h[:, sl] + gate * proj
    if final:
        o_ref[...] = _rms(o_ref[...], fg_ref[...])


def _ple(h2d, g, p2d, wg, wp, final_g, final):
    t, d = h2d.shape
    pd = p2d.shape[1]
    tm = min(t, 512)
    return pl.pallas_call(
        functools.partial(_ple_kernel, tn=min(d, 512), final=final),
        out_shape=jax.ShapeDtypeStruct((t, d), F32),
        grid=(t // tm,),
        in_specs=[pl.BlockSpec((tm, d), lambda m: (m, 0)),
                  pl.BlockSpec((1, d), lambda m: (0, 0)),
                  pl.BlockSpec((tm, pd), lambda m: (m, 0)),
                  pl.BlockSpec((d, d), lambda m: (0, 0), pipeline_mode=pl.Buffered(1)),
                  pl.BlockSpec((pd, d), lambda m: (0, 0), pipeline_mode=pl.Buffered(1)),
                  pl.BlockSpec((1, d), lambda m: (0, 0))],
        out_specs=pl.BlockSpec((tm, d), lambda m: (m, 0)),
        compiler_params=_cparams(("parallel",)),
        name="ple_gate",
    )(h2d, g.reshape(1, d), p2d, wg, wp, final_g.reshape(1, d))


SSM_CHUNK = 32
SSM_GROUP_BLOCK = 8


def kernel(x, p, positions, norm_mix_g, w_in, w_out, diff_lq1, diff_lk1, diff_lq2, diff_lk2,
           diff_subln_g, ssm_lambda_re, ssm_lambda_im, ssm_log_step, ssm_B_re, ssm_B_im,
           ssm_C_re, ssm_C_im, ssm_D, ssm_w_glu, norm_mlp_g, w_up, w_down, norm_ple_g,
           w_ple_gate, w_ple_proj, final_g):
    b, l, d = x.shape
    depth = w_in.shape[0]
    t = b * l
    tc = min(SSM_CHUNK, l)
    tabs = _rope_tables(positions)
    h = x.reshape(t, d)
    for i in range(depth):
        lambda_init = 0.8 - 0.6 * math.exp(-0.3 * i)
        z = _inproj(h, norm_mix_g[i], _permute_w_in(w_in[i]), tabs)
        z3 = z.reshape(b, l, Z_WIDTH)
        y_a = _diff_attn(z3, diff_lq1[i], diff_lk1[i], diff_lq2[i], diff_lk2[i],
                         diff_subln_g[i], lambda_init)
        y_b = _dsa_attn(z3)
        ops = _ssm_prep(ssm_lambda_re[i], ssm_lambda_im[i], ssm_log_step[i], ssm_B_re[i],
                        ssm_B_im[i], ssm_C_re[i], ssm_C_im[i], tc, SSM_GROUP_BLOCK)
        y_s = _s5(z3, ops, ssm_D[i], tc, SSM_GROUP_BLOCK)
        h, xn = _outproj(h, y_a.reshape(t, -1), y_b.reshape(t, -1), y_s.reshape(t, -1),
                         ssm_w_glu[i].astype(BF16), w_out[i].astype(BF16), norm_mlp_g[i])
        h = _mlp(h, xn, w_up[i].astype(BF16), w_down[i].astype(BF16))
        h = _ple(h, norm_ple_g[i], p[i].reshape(t, -1), w_ple_gate[i].astype(BF16),
                 w_ple_proj[i].astype(BF16), final_g, final=(i == depth - 1))
    return h.reshape(b, l, d)
```

```python
import functools
import math

import jax
import jax.numpy as jnp
from jax import lax
from jax.experimental import pallas as pl
from jax.experimental.pallas import tpu as pltpu

F32 = jnp.float32
BF16 = jnp.bfloat16
I32 = jnp.int32

LANES = 128
EPS = 1e-6
ROPE_THETA = 10000.0

DIFF_QK = 64
DIFF_V = 128
DIFF_HEADS = 4
DSA_DIM = 128
DSA_HEADS = 4
IDX_HEADS = 8
IDX_DIM = 64
TOPK_MAX = 256
SSM_C = 16
SSM_P = 64

COL_AQ, COL_AK, COL_IQ, COL_BQ, COL_AV = 0, 4, 8, 12, 16
COL_BK, COL_BV, COL_IKW, COL_CU = 20, 21, 22, 24
Z_WIDTH = 32 * LANES
IN_TN = 512

MLP_TF = 1024
DIFF_TQ = 512
DIFF_TK = 512
DSA_TQ = 256
DSA_TK = 512
NEG_BIG = -1e30
LOG2E = math.log2(math.e)
VMEM_LIMIT = 56 * 1024 * 1024


def _cparams(sem):
    return pltpu.CompilerParams(dimension_semantics=sem, vmem_limit_bytes=VMEM_LIMIT)


def _rope_tables_kernel(pos_ref, f64_ref, f128_ref, c64_ref, s64_ref, c128_ref, s128_ref):
    pos = pos_ref[...]
    a64 = pos * f64_ref[...]
    a128 = pos * f128_ref[...]
    lane = lax.broadcasted_iota(I32, a64.shape, 1)
    c64_ref[...] = jnp.cos(a64)
    sn = jnp.sin(a64)
    s64_ref[...] = jnp.where((lane & 32) == 0, -sn, sn)
    c128_ref[...] = jnp.cos(a128)
    sn = jnp.sin(a128)
    s128_ref[...] = jnp.where((lane & 64) == 0, -sn, sn)


def _rope_tables(positions):
    t = positions.size
    pos = positions.reshape(t, 1).astype(F32)
    fr64 = ROPE_THETA ** (-jnp.arange(0, 64, 2, dtype=F32) / 64)
    fr128 = ROPE_THETA ** (-jnp.arange(0, 128, 2, dtype=F32) / 128)
    f64 = jnp.tile(fr64, 4).reshape(1, LANES)
    f128 = jnp.tile(fr128, 2).reshape(1, LANES)
    tm = min(t, 1024)
    tab = jax.ShapeDtypeStruct((t, LANES), F32)
    row = pl.BlockSpec((tm, LANES), lambda i: (i, 0))
    return pl.pallas_call(
        _rope_tables_kernel,
        out_shape=(tab, tab, tab, tab),
        grid=(t // tm,),
        in_specs=[pl.BlockSpec((tm, 1), lambda i: (i, 0)),
                  pl.BlockSpec((1, LANES), lambda i: (0, 0)),
                  pl.BlockSpec((1, LANES), lambda i: (0, 0))],
        out_specs=(row, row, row, row),
        compiler_params=_cparams(("parallel",)),
        name="rope_tables",
    )(pos, f64, f128)


def _rms(x, g):
    ms = jnp.mean(x * x, axis=-1, keepdims=True)
    return x * lax.rsqrt(ms + EPS) * g


def _swap_halves(a, half):
    lane = lax.broadcasted_iota(I32, a.shape, 1)
    return jnp.where((lane & half) == 0,
                     pltpu.roll(a, LANES - half, 1), pltpu.roll(a, half, 1))


def _inproj_kernel(x_ref, g_ref, w_ref, c64_ref, s64_ref, c128_ref, s128_ref, z_ref, *, tn):
    xn = _rms(x_ref[...], g_ref[...]).astype(BF16)

    def rope(a, half):
        if half == 32:
            return a * c64_ref[...] + _swap_halves(a, 32) * s64_ref[...]
        return a * c128_ref[...] + _swap_halves(a, 64) * s128_ref[...]

    def epilogue(col, a):
        if col < COL_BQ:
            return rope(a, 32)
        if col < COL_AV or col == COL_BK:
            return rope(a, 64)
        if col == COL_IKW:
            lane = lax.broadcasted_iota(I32, a.shape, 1)
            return jnp.where(lane < IDX_DIM, rope(a, 32), a)
        return a

    groups = tn // LANES
    for n in range(z_ref.shape[1] // tn):
        acc = jnp.dot(xn, w_ref[:, n * tn:(n + 1) * tn], preferred_element_type=F32)
        for j in range(groups):
            col = n * groups + j
            z_ref[:, col * LANES:(col + 1) * LANES] = epilogue(
                col, acc[:, j * LANES:(j + 1) * LANES]).astype(z_ref.dtype)


def _inproj(h2d, g, w, tabs):
    t, d = h2d.shape
    tm = min(t, 512)
    c64, s64, c128, s128 = tabs
    row = lambda m: (m, 0)
    tab = pl.BlockSpec((tm, LANES), row)
    return pl.pallas_call(
        functools.partial(_inproj_kernel, tn=IN_TN),
        out_shape=jax.ShapeDtypeStruct((t, Z_WIDTH), BF16),
        grid=(t // tm,),
        in_specs=[pl.BlockSpec((tm, d), row),
                  pl.BlockSpec((1, d), lambda m: (0, 0)),
                  pl.BlockSpec((d, Z_WIDTH), lambda m: (0, 0), pipeline_mode=pl.Buffered(1)),
                  tab, tab, tab, tab],
        out_specs=pl.BlockSpec((tm, Z_WIDTH), row),
        compiler_params=_cparams(("parallel",)),
        name="norm_inproj_rope",
    )(h2d, g.reshape(1, d), w, c64, s64, c128, s128)


def _permute_w_in(w):
    d = w.shape[0]
    pad = jnp.zeros((d, 56 + LANES), w.dtype)
    return jnp.concatenate(
        [w[:, 0:512], w[:, 512:1024], w[:, 2304:2816], w[:, 1536:2048], w[:, 1024:1536],
         w[:, 2048:2176], w[:, 2176:2304], w[:, 2816:2888], pad, w[:, 2888:3912]],
        axis=1).astype(BF16)


def _flash_step(carry, s, vb):
    m, l, acc = carry
    m_new = jnp.maximum(m, jnp.max(s, axis=1, keepdims=True))
    alpha = jnp.exp2(m - m_new)
    p = jnp.exp2(s - m_new)
    l = alpha * l + jnp.sum(p, axis=1, keepdims=True)
    acc = alpha * acc + jnp.dot(p.astype(BF16), vb, preferred_element_type=F32)
    return m_new, l, acc


def _flash_init(rows, dv):
    return (jnp.full((rows, 1), NEG_BIG, F32), jnp.zeros((rows, 1), F32),
            jnp.zeros((rows, dv), F32))


def _dot_nt(a, b):
    return lax.dot_general(a, b, (((1,), (1,)), ((), ())), preferred_element_type=F32)


def _diff_attn_kernel(q_ref, k_ref, vt_ref, lq1_ref, lk1_ref, lq2_ref, lk2_ref, g_ref, o_ref,
                      *, tq, tk, lambda_init):
    i = pl.program_id(2)
    q_t = jnp.transpose(q_ref[0].astype(F32)) * (DIFF_QK ** -0.5)
    dim = lax.broadcasted_iota(I32, q_t.shape, 0)
    q2 = jnp.concatenate([jnp.where(dim < DIFF_QK, q_t, 0.0),
                          jnp.where(dim >= DIFF_QK, q_t, 0.0)], axis=1).astype(BF16)

    def block(j, carry, diag):
        m, l, acc = carry
        kb = k_ref[0, pl.ds(j * tk, tk), :]
        vtb = vt_ref[0, 0, :, pl.ds(j * tk, tk)]
        s = jnp.dot(kb, q2, preferred_element_type=F32) * LOG2E
        if diag:
            key = j * tk + lax.broadcasted_iota(I32, s.shape, 0)
            qi = lax.broadcasted_iota(I32, s.shape, 1)
            qi = i * tq + jnp.where(qi >= tq, qi - tq, qi)
            s = jnp.where(key <= qi, s, NEG_BIG)
        m_new = jnp.maximum(m, jnp.max(s, axis=0, keepdims=True))
        alpha = jnp.exp2(m - m_new)
        p = jnp.exp2(s - m_new)
        l = alpha * l + jnp.sum(p, axis=0, keepdims=True)
        acc = alpha * acc + jnp.dot(vtb, p.astype(BF16), preferred_element_type=F32)
        return m_new, l, acc

    init = (jnp.full((1, 2 * tq), NEG_BIG, F32), jnp.zeros((1, 2 * tq), F32),
            jnp.zeros((DIFF_V, 2 * tq), F32))
    nfull = (i * tq) // tk
    carry = lax.fori_loop(0, nfull, lambda j, c: block(j, c, False), init)
    _, l, acc = block(nfull, carry, True)
    o = acc / l
    lam = (jnp.exp(jnp.sum(lq1_ref[...] * lk1_ref[...], keepdims=True))
           - jnp.exp(jnp.sum(lq2_ref[...] * lk2_ref[...], keepdims=True)) + lambda_init)
    out = jnp.transpose(o[:, :tq] - lam * o[:, tq:])
    out = _rms(out, g_ref[...]) * (1.0 - lambda_init)
    o_ref[0] = out.astype(o_ref.dtype)


def _diff_attn(z3, lq1, lk1, lq2, lk2, subln_g, lambda_init):
    b, l, _ = z3.shape
    tq = min(l, DIFF_TQ)
    tk = min(l, DIFF_TK)
    vec = pl.BlockSpec((1, DIFF_QK), lambda bi, h, i: (0, 0))
    v = z3[:, :, COL_AV * LANES:(COL_AV + DIFF_HEADS) * LANES]
    v_t = v.reshape(b, l, DIFF_HEADS, DIFF_V).transpose(0, 2, 3, 1)
    return pl.pallas_call(
        functools.partial(_diff_attn_kernel, tq=tq, tk=tk, lambda_init=lambda_init),
        out_shape=jax.ShapeDtypeStruct((b, l, DIFF_HEADS * DIFF_V), BF16),
        grid=(b, DIFF_HEADS, l // tq),
        in_specs=[pl.BlockSpec((1, tq, LANES), lambda bi, h, i: (bi, i, COL_AQ + h)),
                  pl.BlockSpec((1, l, LANES), lambda bi, h, i: (bi, 0, COL_AK + h)),
                  pl.BlockSpec((1, 1, DIFF_V, l), lambda bi, h, i: (bi, h, 0, 0)),
                  vec, vec, vec, vec,
                  pl.BlockSpec((1, DIFF_V), lambda bi, h, i: (0, 0))],
        out_specs=pl.BlockSpec((1, tq, DIFF_V), lambda bi, h, i: (bi, i, h)),
        compiler_params=_cparams(("parallel", "parallel", "arbitrary")),
        name="diff_attention",
    )(z3, z3, v_t, lq1.reshape(1, -1), lk1.reshape(1, -1), lq2.reshape(1, -1),
      lk2.reshape(1, -1), subln_g.reshape(1, -1))


def _sortable_key(s):
    bits = lax.bitcast_convert_type(s, I32)
    return bits ^ ((bits >> 31) & jnp.int32(0x7FFFFFFF))


def _dsa_kernel(bq_ref, iq_ref, iwq_ref, bk_ref, bvt_ref, ik_ref, o_ref, hi_ref, lo_ref,
                *, tq, tk, topk, seq_bits):
    i = pl.program_id(1)
    nkb = ((i + 1) * tq + tk - 1) // tk
    low16 = -(2 ** 15)
    i16 = jnp.int16

    def as16(v):
        return v.astype(i16)

    krow = lax.broadcasted_iota(I32, (tk, tq), 0)
    qcol = i * tq + lax.broadcasted_iota(I32, (tk, tq), 1)

    iq_t = jnp.transpose(iq_ref[0].astype(F32))
    iq_all = jnp.concatenate([iq_t[h * IDX_DIM:(h + 1) * IDX_DIM, :] for h in range(IDX_HEADS)],
                             axis=1).astype(BF16)
    iw_t = jnp.transpose(iwq_ref[0].astype(F32))[IDX_DIM:IDX_DIM + IDX_HEADS, :]
    iw_t = iw_t * (IDX_HEADS ** -0.5) * (IDX_DIM ** -0.5)

    def score_block(j, _):
        ikb = ik_ref[0, pl.ds(j * tk, tk), :][:, :IDX_DIM]
        r_all = jnp.dot(ikb, iq_all, preferred_element_type=F32)
        sc = jnp.zeros((tk, tq), F32)
        for h in range(IDX_HEADS):
            sc = sc + jnp.maximum(r_all[:, h * tq:(h + 1) * tq], 0.0) * iw_t[h:h + 1, :]
        sc = jnp.where(j * tk + krow <= qcol, sc + 0.0, -jnp.inf)
        key = _sortable_key(sc)
        hi_ref[j] = as16(key >> 16)
        lo_ref[j] = as16((key & 0xFFFF) + low16)
        return 0

    lax.fori_loop(0, nkb, score_block, 0)

    one16, zero16 = jnp.ones((), i16), jnp.zeros((), i16)

    def counts(block_fn, n):
        def blk(j, cnts):
            out = []
            for pred, cnt in zip(block_fn(j), cnts):
                m = jnp.where(pred, one16, zero16)
                parts = [m[r * 16:(r + 1) * 16, :] for r in range(tk // 16)]
                while len(parts) > 1:
                    parts = [a + b for a, b in zip(parts[::2], parts[1::2])]
                out.append(cnt + parts[0])
            return tuple(out)
        cnts = lax.fori_loop(0, nkb, blk, tuple(jnp.zeros((16, tq), i16) for _ in range(n)))
        return [jnp.sum(cnt.astype(I32), axis=0, keepdims=True) for cnt in cnts]

    def count(pred_fn):
        return counts(lambda j: (pred_fn(j),), 1)[0]

    def radix_select(ref, target, c0=None):
        if c0 is None:
            c0 = count(lambda j: ref[j] >= zero16)
        thr = jnp.where(c0 >= target, 0, low16).astype(I32)

        def bit_step(it, thr):
            cand = thr | (jnp.int32(1) << (14 - it))
            c16 = as16(cand)
            return jnp.where(count(lambda j: ref[j] >= c16) >= target, cand, thr)

        return lax.fori_loop(0, 15, bit_step, thr)

    t_hi = as16(radix_select(hi_ref, topk))
    def above_and_mask_lo(j):
        hi = hi_ref[j]
        lo = jnp.where(hi == t_hi, lo_ref[j], jnp.int16(low16))
        lo_ref[j] = lo
        return hi > t_hi, lo >= zero16

    c_above, c0_lo = counts(above_and_mask_lo, 2)
    need_lo = topk - c_above
    t_lo = as16(radix_select(lo_ref, need_lo, c0_lo))

    def in_tie(j):
        return (hi_ref[j] == t_hi) & (lo_ref[j] == t_lo)

    c_gt_lo, c_tie = counts(lambda j: (lo_ref[j] > t_lo, in_tie(j)), 2)
    need = need_lo - c_gt_lo
    has_excess = jnp.max(jnp.where(c_tie > need, 1, 0)) > 0
    krow16 = as16(krow)

    def tie_limit():
        def step(it, q):
            cand = q + (jnp.int32(1) << (seq_bits - 1 - it))
            c16 = as16(cand)
            c = count(lambda j: in_tie(j) & (krow16 + as16(j * tk) < c16))
            return jnp.where(c < need, cand, q)
        return lax.fori_loop(0, seq_bits, step, jnp.zeros((1, tq), I32))

    jlim16 = as16(lax.cond(has_excess, tie_limit, lambda: jnp.full((1, tq), 2 ** seq_bits, I32)))

    q_t = jnp.transpose(bq_ref[0].astype(F32))
    q_all = jnp.concatenate([q_t[h * DSA_DIM:(h + 1) * DSA_DIM, :] for h in range(DSA_HEADS)],
                            axis=1).astype(BF16)
    scale = DSA_DIM ** -0.5 * LOG2E
    zero_b, neg_b = jnp.zeros((), BF16), jnp.full((), NEG_BIG, BF16)
    qcol16 = as16(qcol)

    def attn_block(j, carry):
        kb = bk_ref[0, pl.ds(j * tk, tk), :]
        vtb = bvt_ref[0, :, pl.ds(j * tk, tk)]
        hi, lo = hi_ref[j], lo_ref[j]
        kidx = krow16 + as16(j * tk)
        sel = (hi > t_hi) | ((hi == t_hi) & ((lo > t_lo) | ((lo == t_lo) & (kidx <= jlim16))))
        sel = sel & (kidx <= qcol16)
        bias = jnp.where(sel, zero_b, neg_b).astype(F32)
        m, l, acc = carry
        s = (jnp.dot(kb, q_all, preferred_element_type=F32) * scale
             + jnp.concatenate([bias] * DSA_HEADS, axis=1))
        m_new = jnp.maximum(m, jnp.max(s, axis=0, keepdims=True))
        alpha = jnp.exp2(m - m_new)
        p = jnp.exp2(s - m_new)
        l = alpha * l + jnp.sum(p, axis=0, keepdims=True)
        pb = p.astype(BF16)
        pv = jnp.concatenate(
            [jnp.dot(vtb, pb[:, h * tq:(h + 1) * tq], preferred_element_type=F32)
             for h in range(DSA_HEADS)], axis=1)
        return m_new, l, alpha * acc + pv

    wide_q = DSA_HEADS * tq
    init = (jnp.full((1, wide_q), NEG_BIG, F32), jnp.zeros((1, wide_q), F32),
            jnp.zeros((DSA_DIM, wide_q), F32))
    _, l, acc = lax.fori_loop(0, nkb, attn_block, init)
    o = acc / l
    for h in range(DSA_HEADS):
        o_ref[0, :, h * DSA_DIM:(h + 1) * DSA_DIM] = jnp.transpose(
            o[:, h * tq:(h + 1) * tq]).astype(o_ref.dtype)


def _dsa_attn(z3):
    b, l, _ = z3.shape
    tq = min(l, DSA_TQ)
    tk = min(l, DSA_TK)
    topk = min(TOPK_MAX, l // 4)
    seq_bits = max(1, (l - 1).bit_length())
    wide = 4
    v_t = jnp.swapaxes(z3[:, :, COL_BV * LANES:(COL_BV + 1) * LANES], 1, 2)
    return pl.pallas_call(
        functools.partial(_dsa_kernel, tq=tq, tk=tk, topk=topk, seq_bits=seq_bits),
        out_shape=jax.ShapeDtypeStruct((b, l, DSA_HEADS * DSA_DIM), BF16),
        grid=(b, l // tq),
        in_specs=[pl.BlockSpec((1, tq, 4 * LANES), lambda bi, i: (bi, i, COL_BQ // wide)),
                  pl.BlockSpec((1, tq, 4 * LANES), lambda bi, i: (bi, i, COL_IQ // wide)),
                  pl.BlockSpec((1, tq, LANES), lambda bi, i: (bi, i, COL_IKW)),
                  pl.BlockSpec((1, l, LANES), lambda bi, i: (bi, 0, COL_BK)),
                  pl.BlockSpec((1, DSA_DIM, l), lambda bi, i: (bi, 0, 0)),
                  pl.BlockSpec((1, l, LANES), lambda bi, i: (bi, 0, COL_IKW))],
        out_specs=pl.BlockSpec((1, tq, DSA_HEADS * DSA_DIM), lambda bi, i: (bi, i, 0)),
        scratch_shapes=[pltpu.VMEM((l // tk, tk, tq), jnp.int16),
                        pltpu.VMEM((l // tk, tk, tq), jnp.int16)],
        compiler_params=_cparams(("parallel", "arbitrary")),
        name="dsa_attention",
    )(z3, z3, z3, z3, v_t, z3)


def _cmul(ar, ai, br, bi):
    return ar * br - ai * bi, ar * bi + ai * br


def _ssm_prep_kernel(lre_ref, lim_ref, lstep_ref, btr_ref, bti_ref, cr_ref, ci_ref,
                     w1_ref, mt_ref, tzt_ref, at_ref, *, tc, gb):
    c = SSM_C
    p = SSM_P
    for gi in range(gb):
        lr = lre_ref[gi].reshape(1, 1, p)
        li = lim_ref[gi].reshape(1, 1, p)
        step = jnp.exp(lstep_ref[gi]).reshape(1, 1, 1)
        mag = jnp.exp(lr * step)
        a_re, a_im = mag * jnp.cos(li * step), mag * jnp.sin(li * step)
        den = lr * lr + li * li
        nr, ni = a_re - 1.0, a_im
        f_re, f_im = (nr * lr + ni * li) / den, (ni * lr - nr * li) / den
        bt_r, bt_i = btr_ref[gi][None], bti_ref[gi][None]
        bb_re = f_re * bt_r - f_im * bt_i
        bb_im = f_re * bt_i + f_im * bt_r
        fr, fi = jnp.ones_like(a_re), jnp.zeros_like(a_im)
        rr, ri = fr, fi
        pr, pi = a_re, a_im
        n = 1
        while n < tc:
            xr, xi = _cmul(fr, fi, pr, pi)
            fr, fi = jnp.concatenate([fr, xr], 0), jnp.concatenate([fi, xi], 0)
            xr, xi = _cmul(rr, ri, pr, pi)
            rr, ri = jnp.concatenate([xr, rr], 0), jnp.concatenate([xi, ri], 0)
            pr, pi = _cmul(pr, pi, pr, pi)
            n *= 2
        at_ref[gi] = jnp.concatenate([pr[0], pi[0]], axis=1)
        wr, wi = _cmul(rr, ri, bb_re, bb_im)
        w1 = jnp.concatenate([wr, wi], axis=2).reshape(tc * c, 2 * p)
        w1_ref[gi] = w1.astype(w1_ref.dtype)
        f1r, f1i = _cmul(fr, fi, a_re, a_im)
        c_re, c_im = cr_ref[gi][None], ci_ref[gi][None]
        mr, mi = _cmul(f1r, f1i, c_re, c_im)
        mt_ref[gi] = jnp.concatenate([mr, -mi], axis=2).reshape(tc * c, 2 * p).astype(mt_ref.dtype)
        er, ei = _cmul(fr, fi, c_re, c_im)
        e2 = jnp.concatenate([er, -ei], axis=2).reshape(tc * c, 2 * p)
        bcat = jnp.concatenate([bb_re[0], bb_im[0]], axis=1)
        kflat = lax.dot_general(e2, bcat, (((1,), (1,)), ((), ())),
                                precision=lax.Precision.HIGHEST,
                                preferred_element_type=F32)
        per = LANES // c
        kpad = jnp.concatenate([kflat, jnp.zeros((tc * c, LANES - c), F32)], axis=1)
        rolled = [kpad] + [pltpu.roll(kpad, r * c, 1) for r in range(1, per)]
        for a in range(tc // per):
            tile = None
            for r in range(per):
                s = a * per + r
                piece = rolled[r][:(tc - s) * c, :]
                if s:
                    piece = jnp.concatenate([jnp.zeros((s * c, LANES), F32), piece], axis=0)
                tile = piece if tile is None else tile + piece
            tzt_ref[gi, :, a * LANES:(a + 1) * LANES] = tile.astype(tzt_ref.dtype)


def _ssm_prep(lam_re, lam_im, log_step, b_re, b_im, c_re, c_im, tc, gb):
    g, p = lam_re.shape
    c = SSM_C
    n = tc * c
    vecp = pl.BlockSpec((gb, 1, p), lambda i: (i, 0, 0))
    mat = pl.BlockSpec((gb, c, p), lambda i: (i, 0, 0))
    op = pl.BlockSpec((gb, n, 2 * p), lambda i: (i, 0, 0))
    return pl.pallas_call(
        functools.partial(_ssm_prep_kernel, tc=tc, gb=gb),
        out_shape=(jax.ShapeDtypeStruct((g, n, 2 * p), BF16),
                   jax.ShapeDtypeStruct((g, n, 2 * p), BF16),
                   jax.ShapeDtypeStruct((g, n, n), BF16),
                   jax.ShapeDtypeStruct((g, 1, 2 * p), F32)),
        grid=(g // gb,),
        in_specs=[vecp, vecp, pl.BlockSpec((gb, 1, 1), lambda i: (i, 0, 0)), mat, mat, mat, mat],
        out_specs=(op, op, pl.BlockSpec((gb, n, n), lambda i: (i, 0, 0)),
                   pl.BlockSpec((gb, 1, 2 * p), lambda i: (i, 0, 0))),
        compiler_params=_cparams(("parallel",)),
        name="ssm_prep",
    )(lam_re.reshape(g, 1, p), lam_im.reshape(g, 1, p), log_step.reshape(g, 1, 1),
      jnp.swapaxes(b_re, 1, 2), jnp.swapaxes(b_im, 1, 2), c_re, c_im)


def _ssm_fused_kernel(u_ref, w1_ref, mt_ref, tzt_ref, at_ref, d_ref, y_ref,
                      uf_ref, ufl_ref, s_ref, s2_ref, x_ref, yfl_ref, yt_ref, *, tc, gb):
    c = SSM_C
    l = uf_ref.shape[0]
    nch = l // tc
    uf_ref[...] = u_ref[0].astype(F32)
    for s in range(tc):
        tile = uf_ref[pl.ds(s, nch, stride=tc), :]
        for g in range(gb):
            ufl_ref[g, :, s * c:(s + 1) * c] = tile[:, g * c:(g + 1) * c]
    for g in range(gb):
        sg = jnp.dot(ufl_ref[g].astype(BF16), w1_ref[g], preferred_element_type=F32)
        s_ref[pl.ds(g, nch, stride=gb), :] = sg
    a = at_ref[...].reshape(gb, 2 * SSM_P)
    lane = lax.broadcasted_iota(I32, a.shape, 1)
    a_sw = pltpu.roll(a, SSM_P, 1)
    a_re = jnp.where(lane < SSM_P, a, a_sw)
    a_im_s = jnp.where(lane < SSM_P, -a_sw, a)

    s2_ref[...] = pltpu.roll(s_ref[...], SSM_P, 1)

    def step(ci, carry):
        x1, x2 = carry
        r0 = pl.multiple_of(ci * gb, gb)
        x_ref[pl.ds(r0, gb), :] = x1
        n1 = x1 * a_re + x2 * a_im_s + s_ref[pl.ds(r0, gb), :]
        n2 = x2 * a_re - x1 * a_im_s + s2_ref[pl.ds(r0, gb), :]
        return n1, n2

    zero = jnp.zeros((gb, 2 * SSM_P), F32)
    lax.fori_loop(0, nch, step, (zero, zero))
    for g in range(gb):
        u = ufl_ref[g]
        xg = x_ref[pl.ds(g, nch, stride=gb), :]
        y = _dot_nt(u.astype(BF16), tzt_ref[g]) + _dot_nt(xg.astype(BF16), mt_ref[g])
        yfl_ref[g] = jax.nn.gelu(y + u * d_ref[g])
    for t in range(tc):
        tile = jnp.concatenate([yfl_ref[g, :, t * c:(t + 1) * c] for g in range(gb)], axis=1)
        yt_ref[pl.ds(t, nch, stride=tc), :] = tile
    y_ref[0] = yt_ref[...].astype(y_ref.dtype)


def _s5(z3, ops, d_skip, tc, gb):
    w1, mt, tzt, at = ops
    b, l, _ = z3.shape
    c = SSM_C
    g = w1.shape[0]
    nch = l // tc
    n = tc * c
    p2 = 2 * SSM_P
    wl = gb * c
    dt = jnp.tile(d_skip.reshape(g, 1, c), (1, 1, tc)).astype(F32)
    col0 = COL_CU * LANES // wl
    return pl.pallas_call(
        functools.partial(_ssm_fused_kernel, tc=tc, gb=gb),
        out_shape=jax.ShapeDtypeStruct((b, l, g * c), BF16),
        grid=(g // gb, b),
        in_specs=[pl.BlockSpec((1, l, wl), lambda gi, bi: (bi, 0, col0 + gi)),
                  pl.BlockSpec((gb, n, p2), lambda gi, bi: (gi, 0, 0)),
                  pl.BlockSpec((gb, n, p2), lambda gi, bi: (gi, 0, 0)),
                  pl.BlockSpec((gb, n, n), lambda gi, bi: (gi, 0, 0)),
                  pl.BlockSpec((gb, 1, p2), lambda gi, bi: (gi, 0, 0)),
                  pl.BlockSpec((gb, 1, n), lambda gi, bi: (gi, 0, 0))],
        out_specs=pl.BlockSpec((1, l, wl), lambda gi, bi: (bi, 0, gi)),
        scratch_shapes=[pltpu.VMEM((l, wl), F32),
                        pltpu.VMEM((gb, nch, n), F32),
                        pltpu.VMEM((nch * gb, p2), F32),
                        pltpu.VMEM((nch * gb, p2), F32),
                        pltpu.VMEM((nch * gb, p2), F32),
                        pltpu.VMEM((gb, nch, n), F32),
                        pltpu.VMEM((l, wl), F32)],
        compiler_params=_cparams(("parallel", "arbitrary")),
        name="ssm_fused",
    )(z3, w1, mt, tzt, at, dt)


def _outproj_kernel(h_ref, ya_ref, yb_ref, ys_ref, wg_ref, w_ref, gn_ref, o_ref, xn_ref, *, tn):
    ka, kb = ya_ref.shape[1], yb_ref.shape[1]
    ys = ys_ref[...]
    gate = jnp.dot(ys, wg_ref[...], preferred_element_type=F32)
    yc = (ys.astype(F32) * jax.nn.sigmoid(gate)).astype(BF16)
    for j in range(h_ref.shape[1] // tn):
        sl = slice(j * tn, (j + 1) * tn)
        acc = jnp.dot(ya_ref[...], w_ref[:ka, sl], preferred_element_type=F32)
        acc += jnp.dot(yb_ref[...], w_ref[ka:ka + kb, sl], preferred_element_type=F32)
        acc += jnp.dot(yc, w_ref[ka + kb:, sl], preferred_element_type=F32)
        o_ref[:, sl] = h_ref[:, sl] + acc
    xn_ref[...] = _rms(o_ref[...], gn_ref[...]).astype(xn_ref.dtype)


def _outproj(h2d, ya, yb, ys, w_glu, w, g_next):
    t, d = h2d.shape
    ka, kb, kc = ya.shape[1], yb.shape[1], ys.shape[1]
    tm = min(t, 512)
    row = lambda m: (m, 0)
    return pl.pallas_call(
        functools.partial(_outproj_kernel, tn=min(d, 512)),
        out_shape=(jax.ShapeDtypeStruct((t, d), F32), jax.ShapeDtypeStruct((t, d), BF16)),
        grid=(t // tm,),
        in_specs=[pl.BlockSpec((tm, d), row),
                  pl.BlockSpec((tm, ka), row),
                  pl.BlockSpec((tm, kb), row),
                  pl.BlockSpec((tm, kc), row),
                  pl.BlockSpec((kc, kc), lambda m: (0, 0), pipeline_mode=pl.Buffered(1)),
                  pl.BlockSpec((ka + kb + kc, d), lambda m: (0, 0), pipeline_mode=pl.Buffered(1)),
                  pl.BlockSpec((1, d), lambda m: (0, 0))],
        out_specs=(pl.BlockSpec((tm, d), row), pl.BlockSpec((tm, d), row)),
        compiler_params=_cparams(("parallel",)),
        name="glu_out_proj",
    )(h2d, ya, yb, ys, w_glu, w, g_next.reshape(1, d))


def _mlp_kernel(h_ref, xn_ref, wu_ref, wd_ref, o_ref, acc_ref):
    f = pl.program_id(1)

    @pl.when(f == 0)
    def _():
        acc_ref[...] = h_ref[...]

    hid = jnp.dot(xn_ref[...], wu_ref[...], preferred_element_type=F32)
    hid = jnp.square(jnp.maximum(hid, 0.0)).astype(BF16)
    acc_ref[...] += jnp.dot(hid, wd_ref[...], preferred_element_type=F32)

    @pl.when(f == pl.num_programs(1) - 1)
    def _():
        o_ref[...] = acc_ref[...]


def _mlp(h2d, xn2d, w_up, w_down):
    t, d = h2d.shape
    ff = w_up.shape[1]
    tm = min(t, 512)
    tf = min(ff, MLP_TF)
    return pl.pallas_call(
        _mlp_kernel,
        out_shape=jax.ShapeDtypeStruct((t, d), F32),
        grid=(t // tm, ff // tf),
        in_specs=[pl.BlockSpec((tm, d), lambda m, f: (m, 0)),
                  pl.BlockSpec((tm, d), lambda m, f: (m, 0)),
                  pl.BlockSpec((d, tf), lambda m, f: (0, f)),
                  pl.BlockSpec((tf, d), lambda m, f: (f, 0))],
        out_specs=pl.BlockSpec((tm, d), lambda m, f: (m, 0)),
        scratch_shapes=[pltpu.VMEM((tm, d), F32)],
        compiler_params=_cparams(("parallel", "arbitrary")),
        name="mlp_relu2",
    )(h2d, xn2d, w_up, w_down)


def _ple_kernel(h_ref, g_ref, p_ref, wg_ref, wp_ref, fg_ref, o_ref, *, tn, final):
    h = h_ref[...]
    xn = _rms(h, g_ref[...]).astype(BF16)
    pb = p_ref[...].astype(BF16)
    d = h.shape[1]
    for j in range(d // tn):
        sl = slice(j * tn, (j + 1) * tn)
        gate = jax.nn.sigmoid(jnp.dot(xn, wg_ref[:, sl], preferred_element_type=F32))
        proj = jnp.dot(pb, wp_ref[:, sl], preferred_element_type=F32)
        o_ref[:, sl] = h[:, sl] + gate * proj
    if final:
        o_ref[...] = _rms(o_ref[...], fg_ref[...])


def _ple(h2d, g, p2d, wg, wp, final_g, final):
    t, d = h2d.shape
    pd = p2d.shape[1]
    tm = min(t, 512)
    return pl.pallas_call(
        functools.partial(_ple_kernel, tn=min(d, 512), final=final),
        out_shape=jax.ShapeDtypeStruct((t, d), F32),
        grid=(t // tm,),
        in_specs=[pl.BlockSpec((tm, d), lambda m: (m, 0)),
                  pl.BlockSpec((1, d), lambda m: (0, 0)),
                  pl.BlockSpec((tm, pd), lambda m: (m, 0)),
                  pl.BlockSpec((d, d), lambda m: (0, 0), pipeline_mode=pl.Buffered(1)),
                  pl.BlockSpec((pd, d), lambda m: (0, 0), pipeline_mode=pl.Buffered(1)),
                  pl.BlockSpec((1, d), lambda m: (0, 0))],
        out_specs=pl.BlockSpec((tm, d), lambda m: (m, 0)),
        compiler_params=_cparams(("parallel",)),
        name="ple_gate",
    )(h2d, g.reshape(1, d), p2d, wg, wp, final_g.reshape(1, d))


SSM_CHUNK = 16
SSM_GROUP_BLOCK = 8


def kernel(x, p, positions, norm_mix_g, w_in, w_out, diff_lq1, diff_lk1, diff_lq2, diff_lk2,
           diff_subln_g, ssm_lambda_re, ssm_lambda_im, ssm_log_step, ssm_B_re, ssm_B_im,
           ssm_C_re, ssm_C_im, ssm_D, ssm_w_glu, norm_mlp_g, w_up, w_down, norm_ple_g,
           w_ple_gate, w_ple_proj, final_g):
    b, l, d = x.shape
    depth = w_in.shape[0]
    t = b * l
    tc = min(SSM_CHUNK, l)
    tabs = _rope_tables(positions)
    h = x.reshape(t, d)
    for i in range(depth):
        lambda_init = 0.8 - 0.6 * math.exp(-0.3 * i)
        z = _inproj(h, norm_mix_g[i], _permute_w_in(w_in[i]), tabs)
        z3 = z.reshape(b, l, Z_WIDTH)
        y_a = _diff_attn(z3, diff_lq1[i], diff_lk1[i], diff_lq2[i], diff_lk2[i],
                         diff_subln_g[i], lambda_init)
        y_b = _dsa_attn(z3)
        ops = _ssm_prep(ssm_lambda_re[i], ssm_lambda_im[i], ssm_log_step[i], ssm_B_re[i],
                        ssm_B_im[i], ssm_C_re[i], ssm_C_im[i], tc, SSM_GROUP_BLOCK)
        y_s = _s5(z3, ops, ssm_D[i], tc, SSM_GROUP_BLOCK)
        h, xn = _outproj(h, y_a.reshape(t, -1), y_b.reshape(t, -1), y_s.reshape(t, -1),
                         ssm_w_glu[i].astype(BF16), w_out[i].astype(BF16), norm_mlp_g[i])
        h = _mlp(h, xn, w_up[i].astype(BF16), w_down[i].astype(BF16))
        h = _ple(h, norm_ple_g[i], p[i].reshape(t, -1), w_ple_gate[i].astype(BF16),
                 w_ple_proj[i].astype(BF16), final_g, final=(i == depth - 1))
    return h.reshape(b, l, d)
```
